```python
import jax
import jax.numpy as jnp
from jax import lax
import numpy as np

D_MODEL = 1024
BATCH = 16
SEQ = 2048
DEPTH = 2

GRID_W = 64
CTX_LEN = 256
N_MOD = 6

NA_HEADS = 8
NA_HEAD_DIM = 64
NA_WIN_ROWS = 8
NA_WIN_COLS = 16
NA_QCOLS = 16
NA_KCOLS = NA_QCOLS + NA_WIN_COLS

MLA_HEADS = 8
MLA_NOPE = 64
MLA_ROPE = 32
MLA_QK_DIM = MLA_NOPE + MLA_ROPE
MLA_V_DIM = 64
MLA_Q_LORA = 384
MLA_KV_LORA = 256
MLA_QBLOCK = 128
ROPE_BASE = 10000.0

NA_WIDTH = NA_HEADS * NA_HEAD_DIM
MLA_WIDTH = MLA_HEADS * MLA_V_DIM
IN_COLS = 3 * NA_WIDTH + MLA_Q_LORA + MLA_KV_LORA + MLA_ROPE + 2 * D_MODEL
IN_SPLITS = (3 * NA_WIDTH, 3 * NA_WIDTH + MLA_Q_LORA, 3 * NA_WIDTH + MLA_Q_LORA + MLA_KV_LORA, 3 * NA_WIDTH + MLA_Q_LORA + MLA_KV_LORA + MLA_ROPE)

N_EXPERTS = 64
N_GROUPS = 8
EXPERTS_PER_GROUP = N_EXPERTS // N_GROUPS
TOP_K = 2
D_EXPERT = 1024
MOE_BLOCK = 128

RMS_EPS = 1e-6
NEG_INF = -1e30

kernel_name = 'hybrid_dit_natten_mla_grouped_moe'


def rmsnorm(x, g):
    x32 = x.astype(jnp.float32)
    y = x32 * lax.rsqrt(jnp.mean(x32 * x32, axis=-1, keepdims=True) + RMS_EPS)
    return (y * g.astype(jnp.float32)).astype(x.dtype)


def modulate(h, shift, scale):
    return h * (1 + scale) + shift


def rotate(x, pos):
    m = x.shape[-1] // 2
    inv = ROPE_BASE ** (-jnp.arange(m, dtype=jnp.float32) / m)
    ang = pos.astype(jnp.float32)[:, None] * inv[None, :]
    cos = jnp.cos(ang)[None, :, None, :]
    sin = jnp.sin(ang)[None, :, None, :]
    x32 = x.astype(jnp.float32)
    x1, x2 = x32[..., :m], x32[..., m:]
    return jnp.concatenate([x1 * cos - x2 * sin, x1 * sin + x2 * cos], axis=-1).astype(x.dtype)


def axial_rope_tail(x, prow, pcol):
    half = MLA_ROPE // 2
    xn, xr = x[..., :MLA_NOPE], x[..., MLA_NOPE:]
    xr = jnp.concatenate([rotate(xr[..., :half], prow), rotate(xr[..., half:], pcol)], axis=-1)
    return jnp.concatenate([xn, xr], axis=-1)


def mixer_inputs(h, w_in_l, na_q_g_l, na_k_g_l, cq_g, w_uq_l, ckv_g, w_ukv_l, q_g, k_g):
    B, N, _ = h.shape
    z = jnp.einsum('bnd,dc->bnc', h, w_in_l)
    za, zcq, zckv, zkr, zg = jnp.split(z, IN_SPLITS, axis=-1)
    za = za.reshape(B, N, 3, NA_HEADS, NA_HEAD_DIM)
    qa = rmsnorm(za[:, :, 0], na_q_g_l)
    ka = rmsnorm(za[:, :, 1], na_k_g_l)
    va = za[:, :, 2]
    q = jnp.einsum('bnr,rc->bnc', rmsnorm(zcq, cq_g), w_uq_l).reshape(B, N, MLA_HEADS, MLA_QK_DIM)
    kv = jnp.einsum('bnr,rc->bnc', rmsnorm(zckv, ckv_g), w_ukv_l).reshape(B, N, MLA_HEADS, MLA_NOPE + MLA_V_DIM)
    k_rope = jnp.broadcast_to(zkr[:, :, None, :], (B, N, MLA_HEADS, MLA_ROPE))
    qb = rmsnorm(q, q_g)
    kb = rmsnorm(jnp.concatenate([kv[..., :MLA_NOPE], k_rope], axis=-1), k_g)
    vb = kv[..., MLA_NOPE:]
    ga, gb = jnp.split(zg, 2, axis=-1)
    return qa, ka, va, qb, kb, vb, ga, gb


def neighbourhood_attention(q, k, v, k_ctx, v_ctx, rpb, n_rows):
    B, S, H, dh = q.shape
    kr = min(NA_WIN_ROWS, n_rows)
    ncb = GRID_W // NA_QCOLS
    scale = dh ** -0.5
    j = jnp.arange(ncb)
    kcs = jnp.clip(j * NA_QCOLS - NA_WIN_COLS // 2, 0, GRID_W - NA_KCOLS)
    col_idx = kcs[:, None] + jnp.arange(NA_KCOLS)[None, :]
    qcol = j[:, None] * NA_QCOLS + jnp.arange(NA_QCOLS)[None, :]
    cs = jnp.clip(qcol - NA_WIN_COLS // 2, 0, GRID_W - NA_WIN_COLS)
    kcol = col_idx[:, None, :]
    valid = (kcol >= cs[..., None]) & (kcol < cs[..., None] + NA_WIN_COLS)
    dc_idx = jnp.clip(kcol - qcol[..., None] + NA_WIN_COLS - 1, 0, 2 * NA_WIN_COLS - 2)
    rpb_cols = rpb[:, :, dc_idx].astype(jnp.float32)
    k_g = k.reshape(B, n_rows, GRID_W, H, dh)
    v_g = v.reshape(B, n_rows, GRID_W, H, dh)
    q_rows = q.reshape(B, n_rows, ncb, NA_QCOLS, H, dh).transpose(1, 0, 2, 3, 4, 5)
    n_loc = kr * NA_KCOLS

    def row_step(args):
        r, qr = args
        rs = jnp.clip(r - kr // 2, 0, n_rows - kr)
        k_blk = jnp.take(lax.dynamic_slice_in_dim(k_g, rs, kr, axis=1), col_idx, axis=2)
        v_blk = jnp.take(lax.dynamic_slice_in_dim(v_g, rs, kr, axis=1), col_idx, axis=2)
        dr_idx = rs + jnp.arange(kr) - r + NA_WIN_ROWS - 1
        bias = jnp.take(rpb_cols, dr_idx, axis=1).transpose(0, 2, 3, 1, 4)
        s_loc = jnp.einsum('bnqhd,brnkhd->bhnqrk', qr, k_blk).astype(jnp.float32) * scale + bias
        s_loc = jnp.where(valid[:, :, None, :], s_loc, NEG_INF)
        s_ctx = jnp.einsum('bnqhd,bchd->bhnqc', qr, k_ctx).astype(jnp.float32) * scale
        s = jnp.concatenate([s_loc.reshape(B, H, ncb, NA_QCOLS, n_loc), s_ctx], axis=-1)
        p = jax.nn.softmax(s, axis=-1).astype(v.dtype)
        p_loc = p[..., :n_loc].reshape(B, H, ncb, NA_QCOLS, kr, NA_KCOLS)
        o = jnp.einsum('bhnqrk,brnkhd->bnqhd', p_loc, v_blk)
        return o + jnp.einsum('bhnqc,bchd->bnqhd', p[..., n_loc:], v_ctx)

    o = lax.map(row_step, (jnp.arange(n_rows), q_rows))
    return o.transpose(1, 0, 2, 3, 4, 5).reshape(B, S, H * dh)


def mla_attention(q, k, v, k_ctx, v_ctx):
    B, S, H, dq = q.shape
    scale = dq ** -0.5
    nb = S // MLA_QBLOCK
    q_blocks = q.reshape(B, nb, MLA_QBLOCK, H, dq).transpose(1, 0, 2, 3, 4)

    def block_step(qi):
        s = jnp.concatenate([jnp.einsum('bqhd,bkhd->bhqk', qi, k), jnp.einsum('bqhd,bchd->bhqc', qi, k_ctx)], axis=-1)
        p = jax.nn.softmax(s.astype(jnp.float32) * scale, axis=-1).astype(v.dtype)
        return jnp.einsum('bhqk,bkhd->bqhd', p[..., :S], v) + jnp.einsum('bhqc,bchd->bqhd', p[..., S:], v_ctx)

    o = lax.map(block_step, q_blocks)
    return o.transpose(1, 0, 2, 3, 4).reshape(B, S, H * v.shape[-1])


def dense_attention(q, k, v):
    B, N, H, dq = q.shape
    s = jnp.einsum('bqhd,bkhd->bhqk', q, k).astype(jnp.float32) * dq ** -0.5
    p = jax.nn.softmax(s, axis=-1).astype(v.dtype)
    return jnp.einsum('bhqk,bkhd->bqhd', p, v).reshape(B, N, H * v.shape[-1])


def merge_branches(oa, ob, ga, gb, w_oa_l, w_ob_l, w_out_l):
    y = jax.nn.sigmoid(ga) * jnp.einsum('bnc,cd->bnd', oa, w_oa_l) + jax.nn.sigmoid(gb) * jnp.einsum('bnc,cd->bnd', ob, w_ob_l)
    return jnp.einsum('bnd,de->bne', y, w_out_l)


def moe_ffn(h, w_router, router_bias, w1_l, w3_l, w2_l):
    T, D = h.shape
    M = T * TOP_K
    s = jax.nn.sigmoid(jnp.einsum('td,de->te', h, w_router).astype(jnp.float32))
    biased = (s + router_bias.astype(jnp.float32)).reshape(T, N_GROUPS, EXPERTS_PER_GROUP)
    group_score = lax.top_k(biased, TOP_K)[0].sum(axis=-1)
    g_sel = jnp.argmax(group_score, axis=-1)
    in_group = jnp.take_along_axis(biased, g_sel[:, None, None], axis=1)[:, 0]
    _, local = lax.top_k(in_group, TOP_K)
    expert = (g_sel[:, None] * EXPERTS_PER_GROUP + local).astype(jnp.int32)
    gw = jnp.take_along_axis(s, expert, axis=1)
    gw = gw / jnp.sum(gw, axis=-1, keepdims=True)
    e_flat = expert.reshape(-1)
    tok = jnp.repeat(jnp.arange(T, dtype=jnp.int32), TOP_K)
    order = jnp.argsort(e_flat)
    e_sorted = e_flat[order]
    tok_sorted = tok[order]
    w_sorted = gw.reshape(-1)[order]
    counts = jnp.bincount(e_flat, length=N_EXPERTS).astype(jnp.int32)
    padded = ((counts + MOE_BLOCK - 1) // MOE_BLOCK) * MOE_BLOCK
    start = jnp.cumsum(counts) - counts
    pad_end = jnp.cumsum(padded)
    pad_start = pad_end - padded
    dest = pad_start[e_sorted] + (jnp.arange(M, dtype=jnp.int32) - start[e_sorted])
    n_blocks = -(-M // MOE_BLOCK) + N_EXPERTS
    slot_tok = jnp.full((n_blocks * MOE_BLOCK,), T, jnp.int32).at[dest].set(tok_sorted)
    h_pad = jnp.concatenate([h, jnp.zeros((1, D), h.dtype)], axis=0)
    xin = h_pad[slot_tok].reshape(n_blocks, MOE_BLOCK, D)
    block_expert = jnp.minimum(jnp.searchsorted(pad_end, jnp.arange(n_blocks, dtype=jnp.int32) * MOE_BLOCK, side='right'), N_EXPERTS - 1)

    def expert_block(args):
        xb, e = args
        return (jax.nn.silu(xb @ w1_l[e]) * (xb @ w3_l[e])) @ w2_l[e]

    y = lax.map(expert_block, (xin, block_expert)).reshape(-1, D)
    y_sorted = y[dest] * w_sorted[:, None].astype(h.dtype)
    return jax.ops.segment_sum(y_sorted, tok_sorted, num_segments=T)


def setup_inputs(seed: int = 0) -> dict:
    key = jax.random.key(seed)
    ks = jax.random.split(key, 32)
    f32 = jnp.float32
    L, D = DEPTH, D_MODEL

    def nrm(k, shape, scale):
        return jax.random.normal(k, shape, f32) * scale

    return {
        'x': nrm(ks[0], (BATCH, SEQ, D), 1.0),
        'c': nrm(ks[1], (BATCH, D), 1.0),
        'ctx': nrm(ks[2], (BATCH, CTX_LEN, D), 1.0),
        'c_ctx': nrm(ks[3], (D,), 1.0),
        'w_ada': nrm(ks[4], (L, D, N_MOD * D), 0.5 * D ** -0.5),
        'b_ada': nrm(ks[5], (L, N_MOD * D), 0.02),
        'norm_mix_g': 1.0 + nrm(ks[6], (L, D), 0.1),
        'norm_ffn_g': 1.0 + nrm(ks[7], (L, D), 0.1),
        'w_in': nrm(ks[8], (L, D, IN_COLS), D ** -0.5),
        'na_q_g': 1.0 + nrm(ks[9], (L, NA_HEAD_DIM), 0.1),
        'na_k_g': 1.0 + nrm(ks[10], (L, NA_HEAD_DIM), 0.1),
        'na_rpb': nrm(ks[11], (L, NA_HEADS, 2 * NA_WIN_ROWS - 1, 2 * NA_WIN_COLS - 1), 0.5),
        'mla_cq_g': 1.0 + nrm(ks[12], (L, MLA_Q_LORA), 0.1),
        'w_uq': nrm(ks[13], (L, MLA_Q_LORA, MLA_HEADS * MLA_QK_DIM), MLA_Q_LORA ** -0.5),
        'mla_ckv_g': 1.0 + nrm(ks[14], (L, MLA_KV_LORA), 0.1),
        'w_ukv': nrm(ks[15], (L, MLA_KV_LORA, MLA_HEADS * (MLA_NOPE + MLA_V_DIM)), MLA_KV_LORA ** -0.5),
        'mla_q_g': 1.0 + nrm(ks[16], (L, MLA_QK_DIM), 0.1),
        'mla_k_g': 1.0 + nrm(ks[17], (L, MLA_QK_DIM), 0.1),
        'w_oa': nrm(ks[18], (L, NA_WIDTH, D), NA_WIDTH ** -0.5),
        'w_ob': nrm(ks[19], (L, MLA_WIDTH, D), MLA_WIDTH ** -0.5),
        'w_out': nrm(ks[20], (L, D, D), D ** -0.5),
        'w_router': nrm(ks[21], (D, N_EXPERTS), D ** -0.5),
        'router_bias': nrm(ks[22], (N_EXPERTS,), 0.01),
        'w1': nrm(ks[23], (L, N_EXPERTS, D, D_EXPERT), D ** -0.5),
        'w3': nrm(ks[24], (L, N_EXPERTS, D, D_EXPERT), D ** -0.5),
        'w2': nrm(ks[25], (L, N_EXPERTS, D_EXPERT, D), D_EXPERT ** -0.5),
    }


def reference(x, c, ctx, c_ctx, w_ada, b_ada, norm_mix_g, norm_ffn_g, w_in, na_q_g, na_k_g, na_rpb,
              mla_cq_g, w_uq, mla_ckv_g, w_ukv, mla_q_g, mla_k_g, w_oa, w_ob, w_out,
              w_router, router_bias, w1, w3, w2):
    B, S, D = x.shape
    n_rows = S // GRID_W
    pos = jnp.arange(S, dtype=jnp.int32)
    prow = pos // GRID_W
    pcol = pos % GRID_W
    xc = ctx
    sc = jax.nn.silu(c)
    scc = jax.nn.silu(c_ctx)[None, :]
    for l in range(DEPTH):
        last = l == DEPTH - 1
        mod_x = (sc @ w_ada[l] + b_ada[l])[:, None, :]
        mod_c = (scc @ w_ada[l] + b_ada[l])[:, None, :]
        sh_m, sc_m, g_m, sh_f, sc_f, g_f = jnp.split(mod_x, N_MOD, axis=-1)
        shc_m, scc_m, gc_m, shc_f, scc_f, gc_f = jnp.split(mod_c, N_MOD, axis=-1)

        hx = modulate(rmsnorm(x, norm_mix_g[l]), sh_m, sc_m)
        hc = modulate(rmsnorm(xc, norm_mix_g[l]), shc_m, scc_m)
        lw = (w_in[l], na_q_g[l], na_k_g[l], mla_cq_g[l], w_uq[l], mla_ckv_g[l], w_ukv[l], mla_q_g[l], mla_k_g[l])
        qa_x, ka_x, va_x, qb_x, kb_x, vb_x, ga_x, gb_x = mixer_inputs(hx, *lw)
        qa_c, ka_c, va_c, qb_c, kb_c, vb_c, ga_c, gb_c = mixer_inputs(hc, *lw)
        qb_x = axial_rope_tail(qb_x, prow, pcol)
        kb_x = axial_rope_tail(kb_x, prow, pcol)
        oa_x = neighbourhood_attention(qa_x, ka_x, va_x, ka_c, va_c, na_rpb[l], n_rows)
        ob_x = mla_attention(qb_x, kb_x, vb_x, kb_c, vb_c)
        x = x + g_m * merge_branches(oa_x, ob_x, ga_x, gb_x, w_oa[l], w_ob[l], w_out[l])
        if not last:
            oa_c = dense_attention(qa_c, ka_c, va_c)
            ob_c = dense_attention(qb_c, kb_c, vb_c)
            xc = xc + gc_m * merge_branches(oa_c, ob_c, ga_c, gb_c, w_oa[l], w_ob[l], w_out[l])

        hx2 = modulate(rmsnorm(x, norm_ffn_g[l]), sh_f, sc_f).reshape(-1, D)
        if last:
            y = moe_ffn(hx2, w_router, router_bias, w1[l], w3[l], w2[l])
            x = x + g_f * y.reshape(B, S, D)
        else:
            hc2 = modulate(rmsnorm(xc, norm_ffn_g[l]), shc_f, scc_f).reshape(-1, D)
            y = moe_ffn(jnp.concatenate([hx2, hc2], axis=0), w_router, router_bias, w1[l], w3[l], w2[l])
            x = x + g_f * y[:B * S].reshape(B, S, D)
            xc = xc + gc_f * y[B * S:].reshape(B, xc.shape[1], D)
    return x
```

```python
import functools

import jax
import jax.numpy as jnp
from jax import lax
from jax.experimental import pallas as pl
from jax.experimental.pallas import tpu as pltpu

F32 = jnp.float32
BF16 = jnp.bfloat16

D_MODEL = 1024
DEPTH = 2
GRID_W = 64
N_MOD = 6

NA_HEADS = 8
NA_HEAD_DIM = 64
NA_WIN_ROWS = 8
NA_WIN_COLS = 16
NA_WIDTH = NA_HEADS * NA_HEAD_DIM

MLA_HEADS = 8
MLA_NOPE = 64
MLA_ROPE = 32
MLA_QK_DIM = MLA_NOPE + MLA_ROPE
MLA_V_DIM = 64
MLA_Q_LORA = 384
MLA_KV_LORA = 256
MLA_WIDTH = MLA_HEADS * MLA_V_DIM
ROPE_BASE = 10000.0

N_EXPERTS = 64
EXPERTS_PER_GROUP = 8
N_GROUPS = N_EXPERTS // EXPERTS_PER_GROUP
TOP_K = 2

RMS_EPS = 1e-6
NEG_INF = -1e30

LANES = 128
HEAD_PAD = LANES
MLA_QK_PAD = MLA_HEADS * HEAD_PAD

C_NA = 3 * NA_WIDTH
C_CQ = C_NA + MLA_Q_LORA
C_CKV = C_CQ + MLA_KV_LORA
C_KR = C_CKV + HEAD_PAD
C_ALL = C_KR + 2 * D_MODEL

TM_PROJ = 256
TM_MERGE = 256
TQ_MLA = 256
NA_QROWS = 4
NA_KROWS = 12
TR_ROUTE = 512
BM_EXP = 256
TM_MOVE = 256
VMEM_LIMIT = 56 * 1024 * 1024


def _sigmoid(v):
    return 1.0 / (1.0 + jnp.exp(-v))


def _rms(v, n):
    return v * lax.rsqrt(jnp.sum(v * v, axis=-1, keepdims=True) * (1.0 / n) + RMS_EPS)


def _split_bf16(a):
    hi = a.astype(BF16)
    lo = (a - hi.astype(F32)).astype(BF16)
    return hi, lo


def _dot(a, b):
    return jnp.dot(a, b, preferred_element_type=F32)


def _dot_nt(a, b):
    return lax.dot_general(a, b, (((1,), (1,)), ((), ())), preferred_element_type=F32)


def _dot_split_lhs(a, b):
    hi, lo = _split_bf16(a)
    return _dot(hi, b) + _dot(lo, b)


def _params(sem, vmem=None):
    return pltpu.CompilerParams(dimension_semantics=sem, vmem_limit_bytes=vmem)


def _ada_kernel(c_ref, w_ref, b_ref, o_ref):
    cv = c_ref[...]
    s = cv * _sigmoid(cv)
    s_hi, s_lo = _split_bf16(s)
    w_hi, w_lo = _split_bf16(w_ref[...])
    o_ref[...] = _dot(s_hi, w_hi) + _dot(s_lo, w_hi) + _dot(s_hi, w_lo) + b_ref[...]


def _ada(cvec, w_ada, b_ada3, l):
    rows = cvec.shape[0]
    n = N_MOD * D_MODEL
    tn = 512
    return pl.pallas_call(
        _ada_kernel,
        grid=(n // tn,),
        in_specs=[
            pl.BlockSpec((rows, D_MODEL), lambda j: (0, 0)),
            pl.BlockSpec((None, D_MODEL, tn), lambda j: (l, 0, j)),
            pl.BlockSpec((None, 1, tn), lambda j: (l, 0, j)),
        ],
        out_specs=pl.BlockSpec((rows, tn), lambda j: (0, j)),
        out_shape=jax.ShapeDtypeStruct((rows, n), F32),
        compiler_params=_params(("arbitrary",)),
        name="ada_mod",
    )(cvec, w_ada, b_ada3)


def _rope(v, cos, s1, s2):
    return v * cos + pltpu.roll(v, LANES - 8, axis=1) * s1 + pltpu.roll(v, 8, axis=1) * s2


def _proj_kernel(x_ref, sh_ref, sc_ref, g_ref, win_ref, naqg_ref, nakg_ref, cqg_ref, ckvg_ref,
                 wuq_ref, wuk_ref, wuv_ref, qg_ref, kg_ref, cos_ref, s1_ref, s2_ref, e_ref, et_ref,
                 qa_ref, ka_ref, va_ref, qb_ref, kb_ref, vb_ref, ga_ref, gb_ref):
    x = x_ref[...]
    h = _rms(x, D_MODEL) * g_ref[...]
    h = h * (1.0 + sc_ref[...]) + sh_ref[...]
    hb = h.astype(BF16)

    e = e_ref[...]
    et = et_ref[...]

    def headnorm(z, g):
        ss = _dot_split_lhs(z * z, e)
        r = lax.rsqrt(ss * (1.0 / NA_HEAD_DIM) + RMS_EPS)
        return z * _dot_split_lhs(r, et) * g

    zq = _dot(hb, win_ref[:, 0:NA_WIDTH])
    qa_ref[...] = (headnorm(zq, naqg_ref[...]) * (NA_HEAD_DIM ** -0.5)).astype(BF16)
    zk = _dot(hb, win_ref[:, NA_WIDTH:2 * NA_WIDTH])
    ka_ref[...] = headnorm(zk, nakg_ref[...]).astype(BF16)
    va_ref[...] = _dot(hb, win_ref[:, 2 * NA_WIDTH:C_NA]).astype(BF16)

    cos = cos_ref[...]
    s1 = s1_ref[...]
    s2 = s2_ref[...]

    cq = _rms(_dot(hb, win_ref[:, C_NA:C_CQ]), MLA_Q_LORA) * cqg_ref[...]
    q = _dot(cq.astype(BF16), wuq_ref[...])
    qg = qg_ref[...]
    for hh in range(MLA_HEADS):
        sl = slice(hh * HEAD_PAD, (hh + 1) * HEAD_PAD)
        qn = _rms(q[:, sl], MLA_QK_DIM) * qg
        qb_ref[:, sl] = (_rope(qn, cos, s1, s2) * (MLA_QK_DIM ** -0.5)).astype(BF16)

    ckv = (_rms(_dot(hb, win_ref[:, C_CQ:C_CKV]), MLA_KV_LORA) * ckvg_ref[...]).astype(BF16)
    kn = _dot(ckv, wuk_ref[...])
    kr = _dot(hb, win_ref[:, C_CKV:C_KR])
    kg = kg_ref[...]
    for hh in range(MLA_HEADS):
        sl = slice(hh * HEAD_PAD, (hh + 1) * HEAD_PAD)
        kh = _rms(kn[:, sl] + kr, MLA_QK_DIM) * kg
        kb_ref[:, sl] = _rope(kh, cos, s1, s2).astype(BF16)
    vb_ref[...] = _dot(ckv, wuv_ref[...]).astype(BF16)

    ga_ref[...] = _sigmoid(_dot(hb, win_ref[:, C_KR:C_KR + D_MODEL])).astype(BF16)
    gb_ref[...] = _sigmoid(_dot(hb, win_ref[:, C_KR + D_MODEL:C_ALL])).astype(BF16)


def _proj(x, mod3, mod_row, gain3, l, lw, tabs):
    bsz, sx, _ = x.shape
    tm = TM_PROJ
    nt = sx // tm
    row = (lambda b: b) if mod_row is None else (lambda b: mod_row)

    def full(shape):
        return pl.BlockSpec(shape, lambda b, i: (0,) * len(shape))

    def tok(width):
        return pl.BlockSpec((None, tm, width), lambda b, i: (b, i, 0))

    def tab():
        return pl.BlockSpec((tm, LANES), lambda b, i: (i, 0))

    in_specs = [
        tok(D_MODEL),
        pl.BlockSpec((None, 1, D_MODEL), lambda b, i: (row(b), 0, 0)),
        pl.BlockSpec((None, 1, D_MODEL), lambda b, i: (row(b), 0, 1)),
        pl.BlockSpec((None, 1, D_MODEL), lambda b, i: (l, 0, 0)),
        full((D_MODEL, C_ALL)),
        full((1, NA_WIDTH)), full((1, NA_WIDTH)),
        full((1, MLA_Q_LORA)), full((1, MLA_KV_LORA)),
        full((MLA_Q_LORA, MLA_QK_PAD)), full((MLA_KV_LORA, MLA_QK_PAD)), full((MLA_KV_LORA, MLA_WIDTH)),
        full((1, HEAD_PAD)), full((1, HEAD_PAD)),
        tab(), tab(), tab(),
        full((NA_WIDTH, LANES)), full((LANES, NA_WIDTH)),
    ]
    widths = (NA_WIDTH, NA_WIDTH, NA_WIDTH, MLA_QK_PAD, MLA_QK_PAD, MLA_WIDTH, D_MODEL, D_MODEL)
    return pl.pallas_call(
        _proj_kernel,
        grid=(bsz, nt),
        in_specs=in_specs,
        out_specs=[tok(w) for w in widths],
        out_shape=[jax.ShapeDtypeStruct((bsz, sx, w), BF16) for w in widths],
        compiler_params=_params(("arbitrary", "arbitrary"), VMEM_LIMIT),
        name="mixer_inputs",
    )(x, mod3, mod3, gain3, lw["w_in"], lw["na_q_g"], lw["na_k_g"], lw["cq_g"], lw["ckv_g"],
      lw["w_uq"], lw["w_uk"], lw["w_uv"], lw["q_g"], lw["k_g"], tabs[0], tabs[1], tabs[2],
      lw["seg"], lw["seg_t"])


def _pair_masks(rows):
    lane = lax.broadcasted_iota(jnp.int32, (rows, LANES), 1)
    return lane < NA_HEAD_DIM


def _na_kernel(q_ref, k0_ref, k1_ref, k2_ref, v0_ref, v1_ref, v2_ref, kc_ref, vc_ref, bias_ref, o_ref):
    nq = q_ref.shape[0]
    q = q_ref[...]
    low = _pair_masks(nq)
    zero = jnp.zeros_like(q)
    qs = jnp.concatenate([jnp.where(low, q, zero), jnp.where(low, zero, q)], axis=0)
    kw = jnp.concatenate([k0_ref[...], k1_ref[...], k2_ref[...]], axis=0)
    vw = jnp.concatenate([v0_ref[...], v1_ref[...], v2_ref[...]], axis=0)
    nk = kw.shape[0]
    s = _dot_nt(qs, kw) + bias_ref[...].reshape(2 * nq, nk)
    sc = _dot_nt(qs, kc_ref[...])
    m = jnp.maximum(jnp.max(s, axis=-1, keepdims=True), jnp.max(sc, axis=-1, keepdims=True))
    p = jnp.exp(s - m)
    pc = jnp.exp(sc - m)
    denom = jnp.sum(p, axis=-1, keepdims=True) + jnp.sum(pc, axis=-1, keepdims=True)
    o = (_dot(p.astype(BF16), vw) + _dot(pc.astype(BF16), vc_ref[...])) / denom
    o_ref[...] = jnp.where(low, o[:nq], o[nq:]).astype(BF16)


def _na_attention(qa, ka, va, kc, vc, bias):
    bsz, s, _ = qa.shape
    nq = NA_QROWS * GRID_W
    nqb = s // nq
    kblk = nq
    npairs = NA_WIDTH // LANES
    last_kb = (s // GRID_W - NA_KROWS) // NA_QROWS

    def kstart(qb):
        return jnp.clip(qb - 1, 0, last_kb)

    def variant(qb):
        return jnp.where(qb == 0, 0, jnp.where(qb == nqb - 1, 2, 1))

    def kspec(j):
        return pl.BlockSpec((None, kblk, LANES), lambda p, qb, b: (b, kstart(qb) + j, p))

    cspec = pl.BlockSpec((None, kc.shape[1], LANES), lambda p, qb, b: (b, 0, p))
    return pl.pallas_call(
        _na_kernel,
        grid=(npairs, nqb, bsz),
        in_specs=[pl.BlockSpec((None, nq, LANES), lambda p, qb, b: (b, qb, p)),
                  kspec(0), kspec(1), kspec(2), kspec(0), kspec(1), kspec(2), cspec, cspec,
                  pl.BlockSpec((None, 2, nq, NA_KROWS * GRID_W), lambda p, qb, b: (variant(qb), p, 0, 0))],
        out_specs=pl.BlockSpec((None, nq, LANES), lambda p, qb, b: (b, qb, p)),
        out_shape=jax.ShapeDtypeStruct((bsz, s, NA_WIDTH), BF16),
        compiler_params=_params(("arbitrary", "arbitrary", "arbitrary"), VMEM_LIMIT),
        name="na_attention",
    )(qa, ka, ka, ka, va, va, va, kc, vc, bias)


def _softmax_pv(s_list, v_list):
    m = functools.reduce(jnp.maximum, [jnp.max(s, axis=-1, keepdims=True) for s in s_list])
    ps = [jnp.exp(s - m) for s in s_list]
    denom = functools.reduce(lambda a, b: a + b, [jnp.sum(p, axis=-1, keepdims=True) for p in ps])
    o = functools.reduce(lambda a, b: a + b, [_dot(p.astype(BF16), v) for p, v in zip(ps, v_list)])
    return o / denom


def _mla_kernel(q_ref, k_ref, v_ref, kc_ref, vc_ref, o_ref):
    v = v_ref[...]
    vc = vc_ref[...]
    outs = []
    for hh in range(2):
        sl = slice(hh * HEAD_PAD, (hh + 1) * HEAD_PAD)
        q = q_ref[:, sl]
        outs.append(_softmax_pv([_dot_nt(q, k_ref[:, sl]), _dot_nt(q, kc_ref[:, sl])], [v, vc]))
    o_ref[...] = jnp.where(_pair_masks(q_ref.shape[0]), outs[0], outs[1]).astype(BF16)


def _mla_attention(qb, kb, vb, kcb, vcb):
    bsz, s, _ = qb.shape
    nc = kcb.shape[1]
    tq = TQ_MLA
    npairs = MLA_WIDTH // LANES
    return pl.pallas_call(
        _mla_kernel,
        grid=(bsz, npairs, s // tq),
        in_specs=[pl.BlockSpec((None, tq, 2 * HEAD_PAD), lambda b, p, i: (b, i, p)),
                  pl.BlockSpec((None, s, 2 * HEAD_PAD), lambda b, p, i: (b, 0, p)),
                  pl.BlockSpec((None, s, LANES), lambda b, p, i: (b, 0, p)),
                  pl.BlockSpec((None, nc, 2 * HEAD_PAD), lambda b, p, i: (b, 0, p)),
                  pl.BlockSpec((None, nc, LANES), lambda b, p, i: (b, 0, p))],
        out_specs=pl.BlockSpec((None, tq, LANES), lambda b, p, i: (b, i, p)),
        out_shape=jax.ShapeDtypeStruct((bsz, s, MLA_WIDTH), BF16),
        compiler_params=_params(("arbitrary", "arbitrary", "arbitrary"), VMEM_LIMIT),
        name="mla_attention",
    )(qb, kb, vb, kcb, vcb)


def _ctx_kernel(qa_ref, ka_ref, va_ref, qb_ref, kb_ref, vb_ref, oa_ref, ob_ref):
    n = qa_ref.shape[0]
    low = _pair_masks(n)
    qa = qa_ref[...]
    zero = jnp.zeros_like(qa)
    ka = ka_ref[...]
    va = va_ref[...]
    o_lo = _softmax_pv([_dot_nt(jnp.where(low, qa, zero), ka)], [va])
    o_hi = _softmax_pv([_dot_nt(jnp.where(low, zero, qa), ka)], [va])
    oa_ref[...] = jnp.where(low, o_lo, o_hi).astype(BF16)
    vb = vb_ref[...]
    outs = []
    for hh in range(2):
        sl = slice(hh * HEAD_PAD, (hh + 1) * HEAD_PAD)
        outs.append(_softmax_pv([_dot_nt(qb_ref[:, sl], kb_ref[:, sl])], [vb]))
    ob_ref[...] = jnp.where(low, outs[0], outs[1]).astype(BF16)


def _ctx_attention(qa, ka, va, qb, kb, vb):
    bsz, n, _ = qa.shape
    npairs = NA_WIDTH // LANES
    narrow = pl.BlockSpec((None, n, LANES), lambda b, p: (b, 0, p))
    wide = pl.BlockSpec((None, n, 2 * HEAD_PAD), lambda b, p: (b, 0, p))
    return pl.pallas_call(
        _ctx_kernel,
        grid=(bsz, npairs),
        in_specs=[narrow, narrow, narrow, wide, wide, narrow],
        out_specs=[narrow, narrow],
        out_shape=[jax.ShapeDtypeStruct((bsz, n, NA_WIDTH), BF16),
                   jax.ShapeDtypeStruct((bsz, n, MLA_WIDTH), BF16)],
        compiler_params=_params(("arbitrary", "arbitrary")),
        name="ctx_attention",
    )(qa, ka, va, qb, kb, vb)


def _merge_kernel(x_ref, oa_ref, ob_ref, ga_ref, gb_ref, woa_ref, wob_ref, wout_ref, gm_ref,
                  g_ref, sh_ref, sc_ref, wrh_ref, wrl_ref, xo_ref, h_ref, lg_ref):
    y = (ga_ref[...].astype(F32) * _dot(oa_ref[...], woa_ref[...])
         + gb_ref[...].astype(F32) * _dot(ob_ref[...], wob_ref[...]))
    xn = x_ref[...] + gm_ref[...] * _dot(y.astype(BF16), wout_ref[...])
    xo_ref[...] = xn
    h = _rms(xn, D_MODEL) * g_ref[...]
    h = h * (1.0 + sc_ref[...]) + sh_ref[...]
    h_ref[...] = h
    h_hi, h_lo = _split_bf16(h)
    wrh = wrh_ref[...]
    lg_ref[...] = _dot_nt(wrh, h_hi) + _dot_nt(wrh, h_lo) + _dot_nt(wrl_ref[...], h_hi)


def _merge(x, oa, ob, ga, gb, mod3, mod_row, gain3, l, lw):
    bsz, sx, _ = x.shape
    tm = TM_MERGE
    nt = sx // tm
    row = (lambda b: b) if mod_row is None else (lambda b: mod_row)

    def full(shape):
        return pl.BlockSpec(shape, lambda b, i: (0,) * len(shape))

    def tok(width):
        return pl.BlockSpec((None, tm, width), lambda b, i: (b, i, 0))

    def mod(chunk):
        return pl.BlockSpec((None, 1, D_MODEL), lambda b, i: (row(b), 0, chunk))

    return pl.pallas_call(
        _merge_kernel,
        grid=(bsz, nt),
        in_specs=[tok(D_MODEL), tok(NA_WIDTH), tok(MLA_WIDTH), tok(D_MODEL), tok(D_MODEL),
                  full((NA_WIDTH, D_MODEL)), full((MLA_WIDTH, D_MODEL)), full((D_MODEL, D_MODEL)),
                  mod(2),
                  pl.BlockSpec((None, 1, D_MODEL), lambda b, i: (l, 0, 0)),
                  mod(3), mod(4),
                  full((N_EXPERTS, D_MODEL)), full((N_EXPERTS, D_MODEL))],
        out_specs=[tok(D_MODEL),
                   pl.BlockSpec((tm, D_MODEL), lambda b, i: (b * nt + i, 0)),
                   pl.BlockSpec((N_EXPERTS, tm), lambda b, i: (0, b * nt + i))],
        out_shape=[jax.ShapeDtypeStruct((bsz, sx, D_MODEL), F32),
                   jax.ShapeDtypeStruct((bsz * sx, D_MODEL), F32),
                   jax.ShapeDtypeStruct((N_EXPERTS, bsz * sx), F32)],
        compiler_params=_params(("arbitrary", "arbitrary"), VMEM_LIMIT),
        name="merge_branches",
    )(x, oa, ob, ga, gb, lw["w_oa"], lw["w_ob"], lw["w_out"], mod3, gain3, mod3, mod3,
      lw["wr_hi"], lw["wr_lo"])


def _route_kernel(lg_ref, bias_ref, tri_ref, eid_ref, gw_ref, rank_ref, cnt_ref, run_ref):
    @pl.when(pl.program_id(0) == 0)
    def _():
        run_ref[...] = jnp.zeros_like(run_ref)

    tr = lg_ref.shape[1]
    s = _sigmoid(lg_ref[...])
    biased = s + bias_ref[...]
    sub = lax.broadcasted_iota(jnp.int32, (EXPERTS_PER_GROUP, tr), 0).astype(F32)
    none = float(EXPERTS_PER_GROUP)
    best = jnp.full((1, tr), -jnp.inf, F32)
    e1 = jnp.zeros((1, tr), F32)
    e2 = jnp.zeros((1, tr), F32)
    for g in range(N_GROUPS):
        bg = biased[g * EXPERTS_PER_GROUP:(g + 1) * EXPERTS_PER_GROUP]
        m1 = jnp.max(bg, axis=0, keepdims=True)
        i1 = jnp.min(jnp.where(bg == m1, sub, none), axis=0, keepdims=True)
        rest = jnp.where(sub == i1, -jnp.inf, bg)
        m2 = jnp.max(rest, axis=0, keepdims=True)
        i2 = jnp.min(jnp.where(rest == m2, sub, none), axis=0, keepdims=True)
        score = m1 + m2
        better = score > best
        best = jnp.where(better, score, best)
        e1 = jnp.where(better, g * EXPERTS_PER_GROUP + i1, e1)
        e2 = jnp.where(better, g * EXPERTS_PER_GROUP + i2, e2)

    rowid = lax.broadcasted_iota(jnp.int32, (N_EXPERTS, tr), 0).astype(F32)
    is1 = rowid == e1
    is2 = rowid == e2
    w1 = jnp.sum(jnp.where(is1, s, 0.0), axis=0, keepdims=True)
    w2 = jnp.sum(jnp.where(is2, s, 0.0), axis=0, keepdims=True)
    tot = w1 + w2
    gw_ref[0:1, :] = w1 / tot
    gw_ref[1:2, :] = w2 / tot
    eid_ref[0:1, :] = e1.astype(jnp.int32)
    eid_ref[1:2, :] = e2.astype(jnp.int32)

    onehot = jnp.where(is1 | is2, 1.0, 0.0)
    before = _dot(onehot.astype(BF16), tri_ref[...]) + run_ref[...]
    rank_ref[0:1, :] = jnp.sum(jnp.where(is1, before, 0.0), axis=0, keepdims=True).astype(jnp.int32)
    rank_ref[1:2, :] = jnp.sum(jnp.where(is2, before, 0.0), axis=0, keepdims=True).astype(jnp.int32)
    run_ref[...] = run_ref[...] + jnp.sum(onehot, axis=1, keepdims=True)
    cnt_ref[...] = run_ref[...]


def _route(logits, bias_col, tri):
    t = logits.shape[1]
    tr = TR_ROUTE
    pair = pl.BlockSpec((TOP_K, tr), lambda i: (0, i))
    return pl.pallas_call(
        _route_kernel,
        grid=(t // tr,),
        in_specs=[pl.BlockSpec((N_EXPERTS, tr), lambda i: (0, i)),
                  pl.BlockSpec((N_EXPERTS, 1), lambda i: (0, 0)),
                  pl.BlockSpec((tr, tr), lambda i: (0, 0))],
        out_specs=[pair, pair, pair, pl.BlockSpec((N_EXPERTS, 1), lambda i: (0, 0))],
        out_shape=[jax.ShapeDtypeStruct((TOP_K, t), jnp.int32),
                   jax.ShapeDtypeStruct((TOP_K, t), F32),
                   jax.ShapeDtypeStruct((TOP_K, t), jnp.int32),
                   jax.ShapeDtypeStruct((N_EXPERTS, 1), F32)],
        scratch_shapes=[pltpu.VMEM((N_EXPERTS, 1), F32)],
        compiler_params=_params(("arbitrary",)),
        name="route",
    )(logits, bias_col, tri)


def _dispatch_kernel(dest_ref, h_ref, slots_in_ref, slots_ref, sem, *, n_tok):
    del slots_in_ref
    tm = h_ref.shape[0]
    base = pl.program_id(0) * tm

    def copy(r, k):
        d = dest_ref[k * n_tok + base + r]
        return pltpu.make_async_copy(h_ref.at[pl.ds(r, 1)], slots_ref.at[pl.ds(d, 1)], sem)

    def issue(r, carry):
        copy(r, 0).start()
        copy(r, 1).start()
        return carry

    def drain(r, carry):
        copy(r, 0).wait()
        copy(r, 1).wait()
        return carry

    lax.fori_loop(0, tm, issue, 0)
    lax.fori_loop(0, tm, drain, 0)


def _dispatch(h, dest_flat, slots):
    n_tok = h.shape[0]
    tm = TM_MOVE
    return pl.pallas_call(
        functools.partial(_dispatch_kernel, n_tok=n_tok),
        grid_spec=pltpu.PrefetchScalarGridSpec(
            num_scalar_prefetch=1,
            grid=(n_tok // tm,),
            in_specs=[pl.BlockSpec((tm, D_MODEL), lambda i, dest: (i, 0)),
                      pl.BlockSpec(memory_space=pl.ANY)],
            out_specs=pl.BlockSpec(memory_space=pl.ANY),
            scratch_shapes=[pltpu.SemaphoreType.DMA],
        ),
        out_shape=jax.ShapeDtypeStruct(slots.shape, slots.dtype),
        input_output_aliases={2: 0},
        compiler_params=_params(("arbitrary",)),
        name="moe_dispatch",
    )(dest_flat, h, slots)


def _expert_kernel(be_ref, nused_ref, x_ref, w1_ref, w3_ref, w2_ref, y_ref, w1b, w3b, w2b):
    i = pl.program_id(0)
    used = i < nused_ref[0]

    @pl.when(used)
    def _():
        prev = be_ref[jnp.maximum(i - 1, 0)]

        @pl.when((i == 0) | (be_ref[i] != prev))
        def _():
            w1b[...] = w1_ref[...].astype(BF16)
            w3b[...] = w3_ref[...].astype(BF16)
            w2b[...] = w2_ref[...].astype(BF16)

        xb = x_ref[...].astype(BF16)
        a = _dot(xb, w1b[...])
        b = _dot(xb, w3b[...])
        y_ref[...] = _dot((a * _sigmoid(a) * b).astype(BF16), w2b[...])

    @pl.when(jnp.logical_not(used))
    def _():
        y_ref[...] = jnp.zeros_like(y_ref)


def _experts(slots, block_expert, nused, w1, w3, w2, l):
    n_slots = slots.shape[0]
    bm = BM_EXP
    de = w1.shape[-1]
    wspec_in = pl.BlockSpec((None, None, D_MODEL, de), lambda i, be, nu: (l, be[i], 0, 0))
    wspec_out = pl.BlockSpec((None, None, de, D_MODEL), lambda i, be, nu: (l, be[i], 0, 0))
    return pl.pallas_call(
        _expert_kernel,
        grid_spec=pltpu.PrefetchScalarGridSpec(
            num_scalar_prefetch=2,
            grid=(n_slots // bm,),
            in_specs=[pl.BlockSpec((bm, D_MODEL), lambda i, be, nu: (i, 0)), wspec_in, wspec_in, wspec_out],
            out_specs=pl.BlockSpec((bm, D_MODEL), lambda i, be, nu: (i, 0)),
            scratch_shapes=[pltpu.VMEM((D_MODEL, de), BF16), pltpu.VMEM((D_MODEL, de), BF16),
                            pltpu.VMEM((de, D_MODEL), BF16)],
        ),
        out_shape=jax.ShapeDtypeStruct((n_slots, D_MODEL), F32),
        compiler_params=_params(("arbitrary",), VMEM_LIMIT),
        name="moe_experts",
    )(block_expert, nused, slots, w1, w3, w2)


def _combine_kernel(dest_ref, x_ref, gw_ref, gf_ref, y_hbm, o_ref, ybuf, sem, *, n_tok):
    tm = x_ref.shape[0]
    base = pl.program_id(0) * tm

    def copy(r, k):
        d = dest_ref[k * n_tok + base + r]
        return pltpu.make_async_copy(y_hbm.at[pl.ds(d, 1)], ybuf.at[k, pl.ds(r, 1)], sem)

    def issue(r, carry):
        copy(r, 0).start()
        copy(r, 1).start()
        return carry

    def drain(r, carry):
        copy(r, 0).wait()
        copy(r, 1).wait()
        return carry

    lax.fori_loop(0, tm, issue, 0)
    lax.fori_loop(0, tm, drain, 0)
    gw = gw_ref[...]
    y = gw[:, 0:1] * ybuf[0] + gw[:, 1:2] * ybuf[1]
    o_ref[...] = x_ref[...] + gf_ref[...] * y


def _combine(x, y_slots, dest_flat, gw_t, mod3, mod_row):
    bsz, sx, _ = x.shape
    n_tok = bsz * sx
    tm = TM_MOVE
    nt = sx // tm
    row = (lambda i: i // nt) if mod_row is None else (lambda i: mod_row)
    out = pl.pallas_call(
        functools.partial(_combine_kernel, n_tok=n_tok),
        grid_spec=pltpu.PrefetchScalarGridSpec(
            num_scalar_prefetch=1,
            grid=(n_tok // tm,),
            in_specs=[pl.BlockSpec((tm, D_MODEL), lambda i, dest: (i, 0)),
                      pl.BlockSpec((tm, TOP_K), lambda i, dest: (i, 0)),
                      pl.BlockSpec((None, 1, D_MODEL), lambda i, dest: (row(i), 0, 5)),
                      pl.BlockSpec(memory_space=pl.ANY)],
            out_specs=pl.BlockSpec((tm, D_MODEL), lambda i, dest: (i, 0)),
            scratch_shapes=[pltpu.VMEM((TOP_K, tm, D_MODEL), F32), pltpu.SemaphoreType.DMA],
        ),
        out_shape=jax.ShapeDtypeStruct((n_tok, D_MODEL), F32),
        compiler_params=_params(("arbitrary",)),
        name="moe_combine",
    )(dest_flat, x.reshape(n_tok, D_MODEL), gw_t, mod3, y_slots)
    return out.reshape(bsz, sx, D_MODEL)


def _layer_weights(l, w_in, na_q_g, na_k_g, mla_cq_g, w_uq, mla_ckv_g, w_ukv, mla_q_g, mla_k_g,
                   w_oa, w_ob, w_out, w_router):
    wi = w_in[l]
    c0 = C_NA + MLA_Q_LORA + MLA_KV_LORA
    kr_cols = jnp.zeros((D_MODEL, HEAD_PAD), F32).at[:, MLA_NOPE:MLA_QK_DIM].set(wi[:, c0:c0 + MLA_ROPE])
    w_in_arr = jnp.concatenate([wi[:, :c0], kr_cols, wi[:, c0 + MLA_ROPE:]], axis=1).astype(BF16)

    def pad_heads(w, width):
        w = w.reshape(w.shape[0], MLA_HEADS, width)
        return jnp.pad(w, ((0, 0), (0, 0), (0, HEAD_PAD - width))).reshape(w.shape[0], MLA_QK_PAD)

    ukv = w_ukv[l].reshape(MLA_KV_LORA, MLA_HEADS, MLA_NOPE + MLA_V_DIM)
    seg = (jnp.arange(NA_WIDTH)[:, None] // NA_HEAD_DIM == jnp.arange(LANES)[None, :]).astype(BF16)
    wr_t = w_router.T
    wr_hi = wr_t.astype(BF16)
    return {
        "w_in": w_in_arr,
        "na_q_g": jnp.tile(na_q_g[l], NA_HEADS)[None, :],
        "na_k_g": jnp.tile(na_k_g[l], NA_HEADS)[None, :],
        "cq_g": mla_cq_g[l][None, :],
        "ckv_g": mla_ckv_g[l][None, :],
        "w_uq": pad_heads(w_uq[l], MLA_QK_DIM).astype(BF16),
        "w_uk": pad_heads(ukv[:, :, :MLA_NOPE].reshape(MLA_KV_LORA, -1), MLA_NOPE).astype(BF16),
        "w_uv": ukv[:, :, MLA_NOPE:].reshape(MLA_KV_LORA, MLA_WIDTH).astype(BF16),
        "q_g": jnp.pad(mla_q_g[l], (0, HEAD_PAD - MLA_QK_DIM))[None, :],
        "k_g": jnp.pad(mla_k_g[l], (0, HEAD_PAD - MLA_QK_DIM))[None, :],
        "seg": seg,
        "seg_t": seg.T,
        "w_oa": w_oa[l].astype(BF16),
        "w_ob": w_ob[l].astype(BF16),
        "w_out": w_out[l].astype(BF16),
        "wr_hi": wr_hi,
        "wr_lo": (wr_t - wr_hi.astype(F32)).astype(BF16),
    }


def _rope_tables(s):
    half = MLA_ROPE // 4
    pos = jnp.arange(s, dtype=jnp.int32)
    inv = ROPE_BASE ** (-jnp.arange(half, dtype=F32) / half)
    ang_r = (pos // GRID_W).astype(F32)[:, None] * inv[None, :]
    ang_c = (pos % GRID_W).astype(F32)[:, None] * inv[None, :]
    zeros = jnp.zeros((s, half), F32)
    lead = jnp.zeros((s, MLA_NOPE), F32)
    tail = jnp.zeros((s, HEAD_PAD - MLA_QK_DIM), F32)
    cos = jnp.concatenate([lead + 1.0, jnp.cos(ang_r), jnp.cos(ang_r), jnp.cos(ang_c), jnp.cos(ang_c), tail + 1.0], 1)
    s1 = jnp.concatenate([lead, -jnp.sin(ang_r), zeros, -jnp.sin(ang_c), zeros, tail], 1)
    s2 = jnp.concatenate([lead, zeros, jnp.sin(ang_r), zeros, jnp.sin(ang_c), tail], 1)
    return cos, s1, s2


def _na_bias_tables(rpb, n_rows):
    j = jnp.arange(NA_QROWS)[:, None]
    i = jnp.arange(NA_KROWS)[None, :]
    qc = jnp.arange(GRID_W)[:, None]
    kc = jnp.arange(GRID_W)[None, :]
    cs = jnp.clip(qc - NA_WIN_COLS // 2, 0, GRID_W - NA_WIN_COLS)
    cvalid = (kc >= cs) & (kc < cs + NA_WIN_COLS)
    ohc = jax.nn.one_hot(jnp.clip(kc - qc + NA_WIN_COLS - 1, 0, 2 * NA_WIN_COLS - 2), 2 * NA_WIN_COLS - 1, dtype=F32)
    last_q = n_rows - NA_QROWS
    tabs = []
    for r0 in (0, 2 * NA_QROWS, last_q):
        ks = min(max(r0 - NA_WIN_ROWS // 2, 0), n_rows - NA_KROWS)
        qr = r0 + j
        kr = ks + i
        rs = jnp.clip(qr - NA_WIN_ROWS // 2, 0, n_rows - NA_WIN_ROWS)
        rvalid = (kr >= rs) & (kr < rs + NA_WIN_ROWS)
        ohr = jax.nn.one_hot(jnp.clip(kr - qr + NA_WIN_ROWS - 1, 0, 2 * NA_WIN_ROWS - 2), 2 * NA_WIN_ROWS - 1, dtype=F32)
        t = jnp.einsum("jia,hab,qkb->hjqik", ohr, rpb.astype(F32), ohc, precision=lax.Precision.HIGHEST)
        valid = rvalid[None, :, None, :, None] & cvalid[None, None, :, None, :]
        tabs.append(jnp.where(valid, t, NEG_INF).reshape(rpb.shape[0], NA_QROWS * GRID_W, NA_KROWS * GRID_W))
    return jnp.stack(tabs)


def _slot_tables(eid, rank, counts):
    bm = BM_EXP
    counts = counts[:, 0].astype(jnp.int32)
    padded = ((counts + bm - 1) // bm) * bm
    pad_end = jnp.cumsum(padded)
    pad_start = pad_end - padded
    experts = jnp.arange(N_EXPERTS, dtype=jnp.int32)
    dest = rank + jnp.sum(jnp.where(eid[..., None] == experts, pad_start, 0), axis=-1)
    m = eid.shape[1] * TOP_K
    n_blocks = -(-m // bm) + N_EXPERTS
    blk = jnp.arange(n_blocks, dtype=jnp.int32) * bm
    block_expert = jnp.minimum(jnp.sum(pad_end[None, :] <= blk[:, None], axis=1), N_EXPERTS - 1).astype(jnp.int32)
    nused = (pad_end[-1:] // bm).astype(jnp.int32)
    return dest.astype(jnp.int32), block_expert, nused, n_blocks * bm


def kernel(x, c, ctx, c_ctx, w_ada, b_ada, norm_mix_g, norm_ffn_g, w_in, na_q_g, na_k_g, na_rpb,
           mla_cq_g, w_uq, mla_ckv_g, w_ukv, mla_q_g, mla_k_g, w_oa, w_ob, w_out,
           w_router, router_bias, w1, w3, w2):
    bsz, s, d = x.shape
    n_ctx = ctx.shape[1]
    n_rows = s // GRID_W
    ctx_row = bsz
    pad_rows = -(bsz + 1) % 8
    cvec = jnp.concatenate([c, c_ctx[None, :], jnp.zeros((pad_rows, d), F32)], axis=0)
    b_ada3 = b_ada[:, None, :]
    mix_g3 = norm_mix_g[:, None, :]
    ffn_g3 = norm_ffn_g[:, None, :]
    tabs_x = _rope_tables(s)
    tabs_c = (jnp.ones((n_ctx, LANES), F32), jnp.zeros((n_ctx, LANES), F32), jnp.zeros((n_ctx, LANES), F32))
    tri = (jnp.arange(TR_ROUTE)[:, None] < jnp.arange(TR_ROUTE)[None, :]).astype(BF16)
    bias_col = router_bias.astype(F32)[:, None]
    n_x = bsz * s

    xc = ctx
    for l in range(DEPTH):
        last = l == DEPTH - 1
        lw = _layer_weights(l, w_in, na_q_g, na_k_g, mla_cq_g, w_uq, mla_ckv_g, w_ukv, mla_q_g, mla_k_g,
                            w_oa, w_ob, w_out, w_router)
        mod3 = _ada(cvec, w_ada, b_ada3, l)[:, None, :]
        qa, ka, va, qb, kb, vb, ga, gb = _proj(x, mod3, None, mix_g3, l, lw, tabs_x)
        qa_c, ka_c, va_c, qb_c, kb_c, vb_c, ga_c, gb_c = _proj(xc, mod3, ctx_row, mix_g3, l, lw, tabs_c)
        oa = _na_attention(qa, ka, va, ka_c, va_c, _na_bias_tables(na_rpb[l], n_rows))
        ob = _mla_attention(qb, kb, vb, kb_c, vb_c)
        x, h_x, lg = _merge(x, oa, ob, ga, gb, mod3, None, ffn_g3, l, lw)
        if not last:
            oa_c, ob_c = _ctx_attention(qa_c, ka_c, va_c, qb_c, kb_c, vb_c)
            xc, h_c, lg_c = _merge(xc, oa_c, ob_c, ga_c, gb_c, mod3, ctx_row, ffn_g3, l, lw)
            lg = jnp.concatenate([lg, lg_c], axis=1)

        eid, gw, rank, counts = _route(lg, bias_col, tri)
        dest, block_expert, nused, n_slots = _slot_tables(eid, rank, counts)
        gw_t = gw.T
        slots = jnp.zeros((n_slots, d), F32)
        slots = _dispatch(h_x, dest[:, :n_x].reshape(-1), slots)
        if not last:
            slots = _dispatch(h_c, dest[:, n_x:].reshape(-1), slots)
        y_slots = _experts(slots, block_expert, nused, w1, w3, w2, l)
        x = _combine(x, y_slots, dest[:, :n_x].reshape(-1), gw_t[:n_x], mod3, None)
        if not last:
            xc = _combine(xc, y_slots, dest[:, n_x:].reshape(-1), gw_t[n_x:], mod3, ctx_row)
    return x
```

```python
import functools

import jax
import jax.numpy as jnp
from jax import lax
from jax.experimental import pallas as pl
from jax.experimental.pallas import tpu as pltpu

F32 = jnp.float32
BF16 = jnp.bfloat16

D_MODEL = 1024
DEPTH = 2
GRID_W = 64
N_MOD = 6

NA_HEADS = 8
NA_HEAD_DIM = 64
NA_WIN_ROWS = 8
NA_WIN_COLS = 16
NA_WIDTH = NA_HEADS * NA_HEAD_DIM

MLA_HEADS = 8
MLA_NOPE = 64
MLA_ROPE = 32
MLA_QK_DIM = MLA_NOPE + MLA_ROPE
MLA_V_DIM = 64
MLA_Q_LORA = 384
MLA_KV_LORA = 256
MLA_WIDTH = MLA_HEADS * MLA_V_DIM
ROPE_BASE = 10000.0

N_EXPERTS = 64
EXPERTS_PER_GROUP = 8
N_GROUPS = N_EXPERTS // EXPERTS_PER_GROUP
TOP_K = 2

RMS_EPS = 1e-6
NEG_INF = -1e30
LOG2E = 1.4426950408889634

LANES = 128
HEAD_PAD = LANES
MLA_QK_PAD = MLA_HEADS * HEAD_PAD

C_NA = 3 * NA_WIDTH
C_CQ = C_NA + MLA_Q_LORA
C_CKV = C_CQ + MLA_KV_LORA
C_KR = C_CKV + HEAD_PAD
C_ALL = C_KR + 2 * D_MODEL

TM_PROJ = 256
TM_MERGE = 512
TQ_MLA = 256
NA_QROWS = 4
TR_ROUTE = 512
BM_EXP = 512
TM_MOVE = 256
MOVE_UNROLL = 8
VMEM_LIMIT = 56 * 1024 * 1024


def _sigmoid(v):
    return 1.0 / (1.0 + jnp.exp(-v))


def _rms(v, n):
    return v * lax.rsqrt(jnp.sum(v * v, axis=-1, keepdims=True) * (1.0 / n) + RMS_EPS)


def _split_bf16(a):
    hi = a.astype(BF16)
    lo = (a - hi.astype(F32)).astype(BF16)
    return hi, lo


def _dot(a, b):
    return jnp.dot(a, b, preferred_element_type=F32)


def _dot_nt(a, b):
    return lax.dot_general(a, b, (((1,), (1,)), ((), ())), preferred_element_type=F32)


def _dot_split_lhs(a, b):
    hi, lo = _split_bf16(a)
    return _dot(hi, b) + _dot(lo, b)


def _params(sem, vmem=None):
    return pltpu.CompilerParams(dimension_semantics=sem, vmem_limit_bytes=vmem)


def _ada_kernel(c_ref, w_ref, b_ref, o_ref):
    cv = c_ref[...]
    s = cv * _sigmoid(cv)
    s_hi, s_lo = _split_bf16(s)
    w_hi, w_lo = _split_bf16(w_ref[...])
    o_ref[...] = _dot(s_hi, w_hi) + _dot(s_lo, w_hi) + _dot(s_hi, w_lo) + b_ref[...]


def _ada(cvec, w_ada, b_ada3, l):
    rows = cvec.shape[0]
    n = N_MOD * D_MODEL
    tn = 512
    return pl.pallas_call(
        _ada_kernel,
        grid=(n // tn,),
        in_specs=[
            pl.BlockSpec((rows, D_MODEL), lambda j: (0, 0)),
            pl.BlockSpec((None, D_MODEL, tn), lambda j: (l, 0, j)),
            pl.BlockSpec((None, 1, tn), lambda j: (l, 0, j)),
        ],
        out_specs=pl.BlockSpec((rows, tn), lambda j: (0, j)),
        out_shape=jax.ShapeDtypeStruct((rows, n), F32),
        compiler_params=_params(("arbitrary",)),
        name="ada_mod",
    )(cvec, w_ada, b_ada3)


def _rope(v, cos, s1, s2):
    return v * cos + pltpu.roll(v, LANES - 8, axis=1) * s1 + pltpu.roll(v, 8, axis=1) * s2


def _proj_kernel(x_ref, sh_ref, sc_ref, g_ref, win_ref, naqg_ref, nakg_ref, cqg_ref, ckvg_ref,
                 wuq_ref, wuk_ref, wuv_ref, qg_ref, kg_ref, cos_ref, s1_ref, s2_ref, e_ref, et_ref,
                 qa_ref, ka_ref, va_ref, qb_ref, kb_ref, vb_ref, ga_ref, gb_ref):
    x = x_ref[...]
    h = _rms(x, D_MODEL) * g_ref[...]
    h = h * (1.0 + sc_ref[...]) + sh_ref[...]
    hb = h.astype(BF16)

    e = e_ref[...]
    et = et_ref[...]

    def headnorm(z, g):
        ss = _dot_split_lhs(z * z, e)
        r = lax.rsqrt(ss * (1.0 / NA_HEAD_DIM) + RMS_EPS)
        return z * _dot_split_lhs(r, et) * g

    zq = _dot(hb, win_ref[:, 0:NA_WIDTH])
    qa_ref[...] = (headnorm(zq, naqg_ref[...]) * (NA_HEAD_DIM ** -0.5 * LOG2E)).astype(BF16)
    zk = _dot(hb, win_ref[:, NA_WIDTH:2 * NA_WIDTH])
    ka_ref[...] = headnorm(zk, nakg_ref[...]).astype(BF16)
    va_ref[...] = _dot(hb, win_ref[:, 2 * NA_WIDTH:C_NA]).astype(BF16)

    cos = cos_ref[...]
    s1 = s1_ref[...]
    s2 = s2_ref[...]

    cq = _rms(_dot(hb, win_ref[:, C_NA:C_CQ]), MLA_Q_LORA) * cqg_ref[...]
    q = _dot(cq.astype(BF16), wuq_ref[...])
    qg = qg_ref[...]
    for hh in range(MLA_HEADS):
        sl = slice(hh * HEAD_PAD, (hh + 1) * HEAD_PAD)
        qn = _rms(q[:, sl], MLA_QK_DIM) * qg
        qb_ref[:, sl] = (_rope(qn, cos, s1, s2) * (MLA_QK_DIM ** -0.5 * LOG2E)).astype(BF16)

    ckv = (_rms(_dot(hb, win_ref[:, C_CQ:C_CKV]), MLA_KV_LORA) * ckvg_ref[...]).astype(BF16)
    kn = _dot(ckv, wuk_ref[...])
    kr = _dot(hb, win_ref[:, C_CKV:C_KR])
    kg = kg_ref[...]
    for hh in range(MLA_HEADS):
        sl = slice(hh * HEAD_PAD, (hh + 1) * HEAD_PAD)
        kh = _rms(kn[:, sl] + kr, MLA_QK_DIM) * kg
        kb_ref[:, sl] = _rope(kh, cos, s1, s2).astype(BF16)
    vb_ref[...] = _dot(ckv, wuv_ref[...]).astype(BF16)

    ga_ref[...] = _sigmoid(_dot(hb, win_ref[:, C_KR:C_KR + D_MODEL])).astype(BF16)
    gb_ref[...] = _sigmoid(_dot(hb, win_ref[:, C_KR + D_MODEL:C_ALL])).astype(BF16)


def _proj(x, mod3, mod_row, gain3, l, lw, tabs):
    bsz, sx, _ = x.shape
    tm = min(TM_PROJ, sx)
    nt = sx // tm
    row = (lambda b: b) if mod_row is None else (lambda b: mod_row)

    def full(shape):
        return pl.BlockSpec(shape, lambda b, i: (0,) * len(shape))

    def tok(width):
        return pl.BlockSpec((None, tm, width), lambda b, i: (b, i, 0))

    def tab():
        return pl.BlockSpec((tm, LANES), lambda b, i: (i, 0))

    in_specs = [
        tok(D_MODEL),
        pl.BlockSpec((None, 1, D_MODEL), lambda b, i: (row(b), 0, 0)),
        pl.BlockSpec((None, 1, D_MODEL), lambda b, i: (row(b), 0, 1)),
        pl.BlockSpec((None, 1, D_MODEL), lambda b, i: (l, 0, 0)),
        full((D_MODEL, C_ALL)),
        full((1, NA_WIDTH)), full((1, NA_WIDTH)),
        full((1, MLA_Q_LORA)), full((1, MLA_KV_LORA)),
        full((MLA_Q_LORA, MLA_QK_PAD)), full((MLA_KV_LORA, MLA_QK_PAD)), full((MLA_KV_LORA, MLA_WIDTH)),
        full((1, HEAD_PAD)), full((1, HEAD_PAD)),
        tab(), tab(), tab(),
        full((NA_WIDTH, LANES)), full((LANES, NA_WIDTH)),
    ]
    widths = (NA_WIDTH, NA_WIDTH, NA_WIDTH, MLA_QK_PAD, MLA_QK_PAD, MLA_WIDTH, D_MODEL, D_MODEL)
    return pl.pallas_call(
        _proj_kernel,
        grid=(bsz, nt),
        in_specs=in_specs,
        out_specs=[tok(w) for w in widths],
        out_shape=[jax.ShapeDtypeStruct((bsz, sx, w), BF16) for w in widths],
        compiler_params=_params(("arbitrary", "arbitrary"), VMEM_LIMIT),
        name="mixer_inputs",
    )(x, mod3, mod3, gain3, lw["w_in"], lw["na_q_g"], lw["na_k_g"], lw["cq_g"], lw["ckv_g"],
      lw["w_uq"], lw["w_uk"], lw["w_uv"], lw["q_g"], lw["k_g"], tabs[0], tabs[1], tabs[2],
      lw["seg"], lw["seg_t"])


def _pair_masks(rows):
    lane = lax.broadcasted_iota(jnp.int32, (rows, LANES), 1)
    return lane < NA_HEAD_DIM


def _ones_pad(v, low):
    one = jnp.ones_like(v)
    return jnp.where(low, v, one), jnp.where(low, one, v)


def _pair_finish(o_a, o_b, low):
    num = jnp.where(low, o_a, o_b)
    den = pltpu.roll(jnp.where(low, o_b, o_a), NA_HEAD_DIM, axis=1)
    return (num / den).astype(BF16)


def _na_kernel(q_ref, k_ref, v_ref, kc_ref, vc_ref, bias_ref, o_ref, va_ref, vb_ref, vca_ref, vcb_ref):
    qb = pl.program_id(2)
    n_rows = k_ref.shape[0] // GRID_W
    nwin = NA_WIN_ROWS * GRID_W

    @pl.when(qb == 0)
    def _():
        va_ref[...], vb_ref[...] = _ones_pad(v_ref[...], _pair_masks(v_ref.shape[0]))
        vca_ref[...], vcb_ref[...] = _ones_pad(vc_ref[...], _pair_masks(vc_ref.shape[0]))

    low = _pair_masks(GRID_W)
    kc = kc_ref[...]
    rows = range(NA_QROWS)
    starts, scores = [], []
    for j in rows:
        r = qb * NA_QROWS + j
        rs = jnp.clip(r - NA_WIN_ROWS // 2, 0, n_rows - NA_WIN_ROWS)
        start = pl.multiple_of(rs * GRID_W, GRID_W)
        q = q_ref[j * GRID_W:(j + 1) * GRID_W, :]
        zero = jnp.zeros_like(q)
        qs = jnp.concatenate([jnp.where(low, q, zero), jnp.where(low, zero, q)], axis=0)
        s = _dot_nt(qs, k_ref[pl.ds(start, nwin), :]) + bias_ref[r - rs].reshape(2 * GRID_W, nwin)
        starts.append(start)
        scores.append((s, _dot_nt(qs, kc)))
    probs = []
    for s, sc in scores:
        m = jnp.maximum(jnp.max(s, axis=-1, keepdims=True), jnp.max(sc, axis=-1, keepdims=True))
        probs.append((jnp.exp2(s - m).astype(BF16), jnp.exp2(sc - m).astype(BF16)))
    outs = []
    for start, (p, pc) in zip(starts, probs):
        o_a = _dot(p[:GRID_W], va_ref[pl.ds(start, nwin), :]) + _dot(pc[:GRID_W], vca_ref[...])
        o_b = _dot(p[GRID_W:], vb_ref[pl.ds(start, nwin), :]) + _dot(pc[GRID_W:], vcb_ref[...])
        outs.append(_pair_finish(o_a, o_b, low))
    o_ref[...] = jnp.concatenate(outs, axis=0)


def _na_attention(qa, ka, va, kc, vc, bias):
    bsz, s, _ = qa.shape
    nc = kc.shape[1]
    nq = NA_QROWS * GRID_W
    npairs = NA_WIDTH // LANES
    nwin = NA_WIN_ROWS * GRID_W
    whole = pl.BlockSpec((None, s, LANES), lambda p, b, qb: (b, 0, p))
    cspec = pl.BlockSpec((None, nc, LANES), lambda p, b, qb: (b, 0, p))
    return pl.pallas_call(
        _na_kernel,
        grid=(npairs, bsz, s // nq),
        in_specs=[pl.BlockSpec((None, nq, LANES), lambda p, b, qb: (b, qb, p)),
                  whole, whole, cspec, cspec,
                  pl.BlockSpec((NA_WIN_ROWS, 2, GRID_W, nwin), lambda p, b, qb: (0, p, 0, 0))],
        out_specs=pl.BlockSpec((None, nq, LANES), lambda p, b, qb: (b, qb, p)),
        out_shape=jax.ShapeDtypeStruct((bsz, s, NA_WIDTH), BF16),
        scratch_shapes=[pltpu.VMEM((s, LANES), BF16), pltpu.VMEM((s, LANES), BF16),
                        pltpu.VMEM((nc, LANES), BF16), pltpu.VMEM((nc, LANES), BF16)],
        compiler_params=_params(("arbitrary", "arbitrary", "arbitrary"), VMEM_LIMIT),
        name="na_attention",
    )(qa, ka, va, kc, vc, bias)


def _mla_kernel(q_ref, k_ref, v_ref, kc_ref, vc_ref, o_ref, va_ref, vb_ref, vca_ref, vcb_ref):
    @pl.when(pl.program_id(2) == 0)
    def _():
        va_ref[...], vb_ref[...] = _ones_pad(v_ref[...], _pair_masks(v_ref.shape[0]))
        vca_ref[...], vcb_ref[...] = _ones_pad(vc_ref[...], _pair_masks(vc_ref.shape[0]))

    scores = []
    for hh in range(2):
        sl = slice(hh * HEAD_PAD, (hh + 1) * HEAD_PAD)
        q = q_ref[:, sl]
        scores.append((_dot_nt(q, k_ref[:, sl]), _dot_nt(q, kc_ref[:, sl])))
    probs = []
    for s, sc in scores:
        m = jnp.maximum(jnp.max(s, axis=-1, keepdims=True), jnp.max(sc, axis=-1, keepdims=True))
        probs.append((jnp.exp2(s - m).astype(BF16), jnp.exp2(sc - m).astype(BF16)))
    outs = [_dot(p, vx_ref[...]) + _dot(pc, vcx_ref[...])
            for (p, pc), (vx_ref, vcx_ref) in zip(probs, ((va_ref, vca_ref), (vb_ref, vcb_ref)))]
    o_ref[...] = _pair_finish(outs[0], outs[1], _pair_masks(q_ref.shape[0]))


def _mla_attention(qb, kb, vb, kcb, vcb):
    bsz, s, _ = qb.shape
    nc = kcb.shape[1]
    tq = TQ_MLA
    npairs = MLA_WIDTH // LANES
    return pl.pallas_call(
        _mla_kernel,
        grid=(bsz, npairs, s // tq),
        in_specs=[pl.BlockSpec((None, tq, 2 * HEAD_PAD), lambda b, p, i: (b, i, p)),
                  pl.BlockSpec((None, s, 2 * HEAD_PAD), lambda b, p, i: (b, 0, p)),
                  pl.BlockSpec((None, s, LANES), lambda b, p, i: (b, 0, p)),
                  pl.BlockSpec((None, nc, 2 * HEAD_PAD), lambda b, p, i: (b, 0, p)),
                  pl.BlockSpec((None, nc, LANES), lambda b, p, i: (b, 0, p))],
        out_specs=pl.BlockSpec((None, tq, LANES), lambda b, p, i: (b, i, p)),
        out_shape=jax.ShapeDtypeStruct((bsz, s, MLA_WIDTH), BF16),
        scratch_shapes=[pltpu.VMEM((s, LANES), BF16), pltpu.VMEM((s, LANES), BF16),
                        pltpu.VMEM((nc, LANES), BF16), pltpu.VMEM((nc, LANES), BF16)],
        compiler_params=_params(("arbitrary", "arbitrary", "arbitrary"), VMEM_LIMIT),
        name="mla_attention",
    )(qb, kb, vb, kcb, vcb)


def _softmax2_pv(s, v):
    p = jnp.exp2(s - jnp.max(s, axis=-1, keepdims=True))
    return _dot(p.astype(BF16), v) / jnp.sum(p, axis=-1, keepdims=True)


def _ctx_kernel(qa_ref, ka_ref, va_ref, qb_ref, kb_ref, vb_ref, oa_ref, ob_ref):
    n = qa_ref.shape[0]
    low = _pair_masks(n)
    qa = qa_ref[...]
    zero = jnp.zeros_like(qa)
    ka = ka_ref[...]
    va = va_ref[...]
    o_lo = _softmax2_pv(_dot_nt(jnp.where(low, qa, zero), ka), va)
    o_hi = _softmax2_pv(_dot_nt(jnp.where(low, zero, qa), ka), va)
    oa_ref[...] = jnp.where(low, o_lo, o_hi).astype(BF16)
    vb = vb_ref[...]
    outs = []
    for hh in range(2):
        sl = slice(hh * HEAD_PAD, (hh + 1) * HEAD_PAD)
        outs.append(_softmax2_pv(_dot_nt(qb_ref[:, sl], kb_ref[:, sl]), vb))
    ob_ref[...] = jnp.where(low, outs[0], outs[1]).astype(BF16)


def _ctx_attention(qa, ka, va, qb, kb, vb):
    bsz, n, _ = qa.shape
    npairs = NA_WIDTH // LANES
    narrow = pl.BlockSpec((None, n, LANES), lambda b, p: (b, 0, p))
    wide = pl.BlockSpec((None, n, 2 * HEAD_PAD), lambda b, p: (b, 0, p))
    return pl.pallas_call(
        _ctx_kernel,
        grid=(bsz, npairs),
        in_specs=[narrow, narrow, narrow, wide, wide, narrow],
        out_specs=[narrow, narrow],
        out_shape=[jax.ShapeDtypeStruct((bsz, n, NA_WIDTH), BF16),
                   jax.ShapeDtypeStruct((bsz, n, MLA_WIDTH), BF16)],
        compiler_params=_params(("arbitrary", "arbitrary")),
        name="ctx_attention",
    )(qa, ka, va, qb, kb, vb)


def _merge_kernel(x_ref, oa_ref, ob_ref, ga_ref, gb_ref, woa_ref, wob_ref, wout_ref, gm_ref,
                  g_ref, sh_ref, sc_ref, wrh_ref, wrl_ref, xo_ref, h_ref, lg_ref):
    y = (ga_ref[...].astype(F32) * _dot(oa_ref[...], woa_ref[...])
         + gb_ref[...].astype(F32) * _dot(ob_ref[...], wob_ref[...]))
    xn = x_ref[...] + gm_ref[...] * _dot(y.astype(BF16), wout_ref[...])
    xo_ref[...] = xn
    h = _rms(xn, D_MODEL) * g_ref[...]
    h = h * (1.0 + sc_ref[...]) + sh_ref[...]
    h_ref[...] = h
    h_hi, h_lo = _split_bf16(h)
    wrh = wrh_ref[...]
    lg_ref[...] = _dot_nt(wrh, h_hi) + _dot_nt(wrh, h_lo) + _dot_nt(wrl_ref[...], h_hi)


def _merge(x, oa, ob, ga, gb, mod3, mod_row, gain3, l, lw):
    bsz, sx, _ = x.shape
    tm = min(TM_MERGE, sx)
    nt = sx // tm
    row = (lambda b: b) if mod_row is None else (lambda b: mod_row)

    def full(shape):
        return pl.BlockSpec(shape, lambda b, i: (0,) * len(shape))

    def tok(width):
        return pl.BlockSpec((None, tm, width), lambda b, i: (b, i, 0))

    def mod(chunk):
        return pl.BlockSpec((None, 1, D_MODEL), lambda b, i: (row(b), 0, chunk))

    return pl.pallas_call(
        _merge_kernel,
        grid=(bsz, nt),
        in_specs=[tok(D_MODEL), tok(NA_WIDTH), tok(MLA_WIDTH), tok(D_MODEL), tok(D_MODEL),
                  full((NA_WIDTH, D_MODEL)), full((MLA_WIDTH, D_MODEL)), full((D_MODEL, D_MODEL)),
                  mod(2),
                  pl.BlockSpec((None, 1, D_MODEL), lambda b, i: (l, 0, 0)),
                  mod(3), mod(4),
                  full((N_EXPERTS, D_MODEL)), full((N_EXPERTS, D_MODEL))],
        out_specs=[tok(D_MODEL),
                   pl.BlockSpec((tm, D_MODEL), lambda b, i: (b * nt + i, 0)),
                   pl.BlockSpec((N_EXPERTS, tm), lambda b, i: (0, b * nt + i))],
        out_shape=[jax.ShapeDtypeStruct((bsz, sx, D_MODEL), F32),
                   jax.ShapeDtypeStruct((bsz * sx, D_MODEL), F32),
                   jax.ShapeDtypeStruct((N_EXPERTS, bsz * sx), F32)],
        compiler_params=_params(("arbitrary", "arbitrary"), VMEM_LIMIT),
        name="merge_branches",
    )(x, oa, ob, ga, gb, lw["w_oa"], lw["w_ob"], lw["w_out"], mod3, gain3, mod3, mod3,
      lw["wr_hi"], lw["wr_lo"])


def _route_kernel(lg_ref, bias_ref, tri_ref, eid_ref, gw_ref, rank_ref, cnt_ref, run_ref):
    @pl.when(pl.program_id(0) == 0)
    def _():
        run_ref[...] = jnp.zeros_like(run_ref)

    tr = lg_ref.shape[1]
    s = _sigmoid(lg_ref[...])
    biased = s + bias_ref[...]
    sub = lax.broadcasted_iota(jnp.int32, (EXPERTS_PER_GROUP, tr), 0).astype(F32)
    none = float(EXPERTS_PER_GROUP)
    best = jnp.full((1, tr), -jnp.inf, F32)
    e1 = jnp.zeros((1, tr), F32)
    e2 = jnp.zeros((1, tr), F32)
    for g in range(N_GROUPS):
        bg = biased[g * EXPERTS_PER_GROUP:(g + 1) * EXPERTS_PER_GROUP]
        m1 = jnp.max(bg, axis=0, keepdims=True)
        i1 = jnp.min(jnp.where(bg == m1, sub, none), axis=0, keepdims=True)
        rest = jnp.where(sub == i1, -jnp.inf, bg)
        m2 = jnp.max(rest, axis=0, keepdims=True)
        i2 = jnp.min(jnp.where(rest == m2, sub, none), axis=0, keepdims=True)
        score = m1 + m2
        better = score > best
        best = jnp.where(better, score, best)
        e1 = jnp.where(better, g * EXPERTS_PER_GROUP + i1, e1)
        e2 = jnp.where(better, g * EXPERTS_PER_GROUP + i2, e2)

    rowid = lax.broadcasted_iota(jnp.int32, (N_EXPERTS, tr), 0).astype(F32)
    is1 = rowid == e1
    is2 = rowid == e2
    w1 = jnp.sum(jnp.where(is1, s, 0.0), axis=0, keepdims=True)
    w2 = jnp.sum(jnp.where(is2, s, 0.0), axis=0, keepdims=True)
    tot = w1 + w2
    gw_ref[0:1, :] = w1 / tot
    gw_ref[1:2, :] = w2 / tot
    eid_ref[0:1, :] = e1.astype(jnp.int32)
    eid_ref[1:2, :] = e2.astype(jnp.int32)

    onehot = jnp.where(is1 | is2, 1.0, 0.0)
    before = _dot(onehot.astype(BF16), tri_ref[...]) + run_ref[...]
    rank_ref[0:1, :] = jnp.sum(jnp.where(is1, before, 0.0), axis=0, keepdims=True).astype(jnp.int32)
    rank_ref[1:2, :] = jnp.sum(jnp.where(is2, before, 0.0), axis=0, keepdims=True).astype(jnp.int32)
    run_ref[...] = run_ref[...] + jnp.sum(onehot, axis=1, keepdims=True)
    cnt_ref[...] = run_ref[...]


def _route(logits, bias_col, tri):
    t = logits.shape[1]
    tr = TR_ROUTE
    pair = pl.BlockSpec((TOP_K, tr), lambda i: (0, i))
    return pl.pallas_call(
        _route_kernel,
        grid=(t // tr,),
        in_specs=[pl.BlockSpec((N_EXPERTS, tr), lambda i: (0, i)),
                  pl.BlockSpec((N_EXPERTS, 1), lambda i: (0, 0)),
                  pl.BlockSpec((tr, tr), lambda i: (0, 0))],
        out_specs=[pair, pair, pair, pl.BlockSpec((N_EXPERTS, 1), lambda i: (0, 0))],
        out_shape=[jax.ShapeDtypeStruct((TOP_K, t), jnp.int32),
                   jax.ShapeDtypeStruct((TOP_K, t), F32),
                   jax.ShapeDtypeStruct((TOP_K, t), jnp.int32),
                   jax.ShapeDtypeStruct((N_EXPERTS, 1), F32)],
        scratch_shapes=[pltpu.VMEM((N_EXPERTS, 1), F32)],
        compiler_params=_params(("arbitrary",)),
        name="route",
    )(logits, bias_col, tri)


def _dispatch_kernel(dest_ref, h_ref, slots_in_ref, slots_ref, sem, *, n_tok):
    del slots_in_ref
    tm = h_ref.shape[0]
    base = pl.program_id(0) * tm

    def copy(r, k):
        d = dest_ref[k * n_tok + base + r]
        return pltpu.make_async_copy(h_ref.at[pl.ds(r, 1)], slots_ref.at[pl.ds(d, 1)], sem)

    def issue(r, carry):
        for k in range(TOP_K):
            copy(r, k).start(priority=k)
        return carry

    def drain(r, carry):
        for k in range(TOP_K):
            copy(r, k).wait()
        return carry

    lax.fori_loop(0, tm, issue, 0, unroll=MOVE_UNROLL)
    lax.fori_loop(0, tm, drain, 0, unroll=MOVE_UNROLL)


def _dispatch(h, dest_flat, slots):
    n_tok = h.shape[0]
    tm = TM_MOVE
    return pl.pallas_call(
        functools.partial(_dispatch_kernel, n_tok=n_tok),
        grid_spec=pltpu.PrefetchScalarGridSpec(
            num_scalar_prefetch=1,
            grid=(n_tok // tm,),
            in_specs=[pl.BlockSpec((tm, D_MODEL), lambda i, dest: (i, 0)),
                      pl.BlockSpec(memory_space=pl.ANY)],
            out_specs=pl.BlockSpec(memory_space=pl.ANY),
            scratch_shapes=[pltpu.SemaphoreType.DMA],
        ),
        out_shape=jax.ShapeDtypeStruct(slots.shape, slots.dtype),
        input_output_aliases={2: 0},
        compiler_params=_params(("arbitrary",)),
        name="moe_dispatch",
    )(dest_flat, h, slots)


def _expert_kernel(be_ref, nused_ref, x_ref, w1_ref, w3_ref, w2_ref, y_ref, w1b, w3b, w2b):
    i = pl.program_id(0)
    used = i < nused_ref[0]

    @pl.when(used)
    def _():
        prev = be_ref[jnp.maximum(i - 1, 0)]

        @pl.when((i == 0) | (be_ref[i] != prev))
        def _():
            w1b[...] = w1_ref[...].astype(BF16)
            w3b[...] = w3_ref[...].astype(BF16)
            w2b[...] = w2_ref[...].astype(BF16)

        xb = x_ref[...].astype(BF16)
        a = _dot(xb, w1b[...])
        b = _dot(xb, w3b[...])
        y_ref[...] = _dot((a * _sigmoid(a) * b).astype(BF16), w2b[...])

    @pl.when(jnp.logical_not(used))
    def _():
        y_ref[...] = jnp.zeros_like(y_ref)


def _experts(slots, block_expert, nused, w1, w3, w2, l):
    n_slots = slots.shape[0]
    bm = BM_EXP
    de = w1.shape[-1]
    wspec_in = pl.BlockSpec((None, None, D_MODEL, de), lambda i, be, nu: (l, be[i], 0, 0))
    wspec_out = pl.BlockSpec((None, None, de, D_MODEL), lambda i, be, nu: (l, be[i], 0, 0))
    return pl.pallas_call(
        _expert_kernel,
        grid_spec=pltpu.PrefetchScalarGridSpec(
            num_scalar_prefetch=2,
            grid=(n_slots // bm,),
            in_specs=[pl.BlockSpec((bm, D_MODEL), lambda i, be, nu: (i, 0)), wspec_in, wspec_in, wspec_out],
            out_specs=pl.BlockSpec((bm, D_MODEL), lambda i, be, nu: (i, 0)),
            scratch_shapes=[pltpu.VMEM((D_MODEL, de), BF16), pltpu.VMEM((D_MODEL, de), BF16),
                            pltpu.VMEM((de, D_MODEL), BF16)],
        ),
        out_shape=jax.ShapeDtypeStruct((n_slots, D_MODEL), F32),
        compiler_params=_params(("arbitrary",), VMEM_LIMIT),
        name="moe_experts",
    )(block_expert, nused, slots, w1, w3, w2)


def _combine_kernel(dest_ref, x_ref, gw_ref, gf_ref, y_hbm, o_ref, ybuf, sem, *, n_tok):
    tm = x_ref.shape[0]
    i = pl.program_id(0)
    nsteps = pl.num_programs(0)

    def copy(step, r, k):
        slot = step % 2
        d = dest_ref[k * n_tok + step * tm + r]
        return pltpu.make_async_copy(y_hbm.at[pl.ds(d, 1)], ybuf.at[slot, k, pl.ds(r, 1)], sem.at[slot])

    def issue(step):
        def body(r, carry):
            for k in range(TOP_K):
                copy(step, r, k).start(priority=k)
            return carry
        lax.fori_loop(0, tm, body, 0, unroll=MOVE_UNROLL)

    @pl.when(i == 0)
    def _():
        issue(i)

    @pl.when(i + 1 < nsteps)
    def _():
        issue(i + 1)

    def drain(r, carry):
        for k in range(TOP_K):
            copy(i, r, k).wait()
        return carry
    lax.fori_loop(0, tm, drain, 0, unroll=MOVE_UNROLL)

    slot = i % 2
    gw = gw_ref[...]
    y = gw[:, 0:1] * ybuf[slot, 0] + gw[:, 1:2] * ybuf[slot, 1]
    o_ref[...] = x_ref[...] + gf_ref[...] * y


def _combine(x, y_slots, dest_flat, gw_t, mod3, mod_row):
    bsz, sx, _ = x.shape
    n_tok = bsz * sx
    tm = TM_MOVE
    nt = sx // tm
    row = (lambda i: i // nt) if mod_row is None else (lambda i: mod_row)
    out = pl.pallas_call(
        functools.partial(_combine_kernel, n_tok=n_tok),
        grid_spec=pltpu.PrefetchScalarGridSpec(
            num_scalar_prefetch=1,
            grid=(n_tok // tm,),
            in_specs=[pl.BlockSpec((tm, D_MODEL), lambda i, dest: (i, 0)),
                      pl.BlockSpec((tm, TOP_K), lambda i, dest: (i, 0)),
                      pl.BlockSpec((None, 1, D_MODEL), lambda i, dest: (row(i), 0, 5)),
                      pl.BlockSpec(memory_space=pl.ANY)],
            out_specs=pl.BlockSpec((tm, D_MODEL), lambda i, dest: (i, 0)),
            scratch_shapes=[pltpu.VMEM((2, TOP_K, tm, D_MODEL), F32), pltpu.SemaphoreType.DMA((2,))],
        ),
        out_shape=jax.ShapeDtypeStruct((n_tok, D_MODEL), F32),
        compiler_params=_params(("arbitrary",)),
        name="moe_combine",
    )(dest_flat, x.reshape(n_tok, D_MODEL), gw_t, mod3, y_slots)
    return out.reshape(bsz, sx, D_MODEL)


def _layer_weights(l, w_in, na_q_g, na_k_g, mla_cq_g, w_uq, mla_ckv_g, w_ukv, mla_q_g, mla_k_g,
                   w_oa, w_ob, w_out, w_router):
    wi = w_in[l]
    c0 = C_NA + MLA_Q_LORA + MLA_KV_LORA
    kr_cols = jnp.zeros((D_MODEL, HEAD_PAD), F32).at[:, MLA_NOPE:MLA_QK_DIM].set(wi[:, c0:c0 + MLA_ROPE])
    w_in_arr = jnp.concatenate([wi[:, :c0], kr_cols, wi[:, c0 + MLA_ROPE:]], axis=1).astype(BF16)

    def pad_heads(w, width):
        w = w.reshape(w.shape[0], MLA_HEADS, width)
        return jnp.pad(w, ((0, 0), (0, 0), (0, HEAD_PAD - width))).reshape(w.shape[0], MLA_QK_PAD)

    ukv = w_ukv[l].reshape(MLA_KV_LORA, MLA_HEADS, MLA_NOPE + MLA_V_DIM)
    seg = (jnp.arange(NA_WIDTH)[:, None] // NA_HEAD_DIM == jnp.arange(LANES)[None, :]).astype(BF16)
    wr_t = w_router.T
    wr_hi = wr_t.astype(BF16)
    return {
        "w_in": w_in_arr,
        "na_q_g": jnp.tile(na_q_g[l], NA_HEADS)[None, :],
        "na_k_g": jnp.tile(na_k_g[l], NA_HEADS)[None, :],
        "cq_g": mla_cq_g[l][None, :],
        "ckv_g": mla_ckv_g[l][None, :],
        "w_uq": pad_heads(w_uq[l], MLA_QK_DIM).astype(BF16),
        "w_uk": pad_heads(ukv[:, :, :MLA_NOPE].reshape(MLA_KV_LORA, -1), MLA_NOPE).astype(BF16),
        "w_uv": ukv[:, :, MLA_NOPE:].reshape(MLA_KV_LORA, MLA_WIDTH).astype(BF16),
        "q_g": jnp.pad(mla_q_g[l], (0, HEAD_PAD - MLA_QK_DIM))[None, :],
        "k_g": jnp.pad(mla_k_g[l], (0, HEAD_PAD - MLA_QK_DIM))[None, :],
        "seg": seg,
        "seg_t": seg.T,
        "w_oa": w_oa[l].astype(BF16),
        "w_ob": w_ob[l].astype(BF16),
        "w_out": w_out[l].astype(BF16),
        "wr_hi": wr_hi,
        "wr_lo": (wr_t - wr_hi.astype(F32)).astype(BF16),
    }


def _rope_tables(s):
    half = MLA_ROPE // 4
    pos = jnp.arange(s, dtype=jnp.int32)
    inv = ROPE_BASE ** (-jnp.arange(half, dtype=F32) / half)
    ang_r = (pos // GRID_W).astype(F32)[:, None] * inv[None, :]
    ang_c = (pos % GRID_W).astype(F32)[:, None] * inv[None, :]
    zeros = jnp.zeros((s, half), F32)
    lead = jnp.zeros((s, MLA_NOPE), F32)
    tail = jnp.zeros((s, HEAD_PAD - MLA_QK_DIM), F32)
    cos = jnp.concatenate([lead + 1.0, jnp.cos(ang_r), jnp.cos(ang_r), jnp.cos(ang_c), jnp.cos(ang_c), tail + 1.0], 1)
    s1 = jnp.concatenate([lead, -jnp.sin(ang_r), zeros, -jnp.sin(ang_c), zeros, tail], 1)
    s2 = jnp.concatenate([lead, zeros, jnp.sin(ang_r), zeros, jnp.sin(ang_c), tail], 1)
    return cos, s1, s2


def _na_bias_tables(rpb):
    d = jnp.arange(NA_WIN_ROWS)[:, None]
    i = jnp.arange(NA_WIN_ROWS)[None, :]
    qc = jnp.arange(GRID_W)[:, None]
    kc = jnp.arange(GRID_W)[None, :]
    cs = jnp.clip(qc - NA_WIN_COLS // 2, 0, GRID_W - NA_WIN_COLS)
    cvalid = (kc >= cs) & (kc < cs + NA_WIN_COLS)
    ohr = jax.nn.one_hot(i - d + NA_WIN_ROWS - 1, 2 * NA_WIN_ROWS - 1, dtype=F32)
    ohc = jax.nn.one_hot(jnp.clip(kc - qc + NA_WIN_COLS - 1, 0, 2 * NA_WIN_COLS - 2), 2 * NA_WIN_COLS - 1, dtype=F32)
    t = jnp.einsum("dia,hab,qkb->dhqik", ohr, rpb.astype(F32), ohc, precision=lax.Precision.HIGHEST)
    t = jnp.where(cvalid[None, None, :, None, :], t * LOG2E, NEG_INF)
    return t.reshape(NA_WIN_ROWS, rpb.shape[0], GRID_W, NA_WIN_ROWS * GRID_W)


def _slot_tables(eid, rank, counts):
    bm = BM_EXP
    counts = counts[:, 0].astype(jnp.int32)
    padded = ((counts + bm - 1) // bm) * bm
    pad_end = jnp.cumsum(padded)
    pad_start = pad_end - padded
    experts = jnp.arange(N_EXPERTS, dtype=jnp.int32)
    dest = rank + jnp.sum(jnp.where(eid[..., None] == experts, pad_start, 0), axis=-1)
    m = eid.shape[1] * TOP_K
    n_blocks = -(-m // bm) + N_EXPERTS
    blk = jnp.arange(n_blocks, dtype=jnp.int32) * bm
    block_expert = jnp.minimum(jnp.sum(pad_end[None, :] <= blk[:, None], axis=1), N_EXPERTS - 1).astype(jnp.int32)
    nused = (pad_end[-1:] // bm).astype(jnp.int32)
    return dest.astype(jnp.int32), block_expert, nused, n_blocks * bm


def kernel(x, c, ctx, c_ctx, w_ada, b_ada, norm_mix_g, norm_ffn_g, w_in, na_q_g, na_k_g, na_rpb,
           mla_cq_g, w_uq, mla_ckv_g, w_ukv, mla_q_g, mla_k_g, w_oa, w_ob, w_out,
           w_router, router_bias, w1, w3, w2):
    bsz, s, d = x.shape
    n_ctx = ctx.shape[1]
    ctx_row = bsz
    pad_rows = -(bsz + 1) % 8
    cvec = jnp.concatenate([c, c_ctx[None, :], jnp.zeros((pad_rows, d), F32)], axis=0)
    b_ada3 = b_ada[:, None, :]
    mix_g3 = norm_mix_g[:, None, :]
    ffn_g3 = norm_ffn_g[:, None, :]
    tabs_x = _rope_tables(s)
    tabs_c = (jnp.ones((n_ctx, LANES), F32), jnp.zeros((n_ctx, LANES), F32), jnp.zeros((n_ctx, LANES), F32))
    tri = (jnp.arange(TR_ROUTE)[:, None] < jnp.arange(TR_ROUTE)[None, :]).astype(BF16)
    bias_col = router_bias.astype(F32)[:, None]
    n_x = bsz * s

    xc = ctx
    for l in range(DEPTH):
        last = l == DEPTH - 1
        lw = _layer_weights(l, w_in, na_q_g, na_k_g, mla_cq_g, w_uq, mla_ckv_g, w_ukv, mla_q_g, mla_k_g,
                            w_oa, w_ob, w_out, w_router)
        mod3 = _ada(cvec, w_ada, b_ada3, l)[:, None, :]
        qa, ka, va, qb, kb, vb, ga, gb = _proj(x, mod3, None, mix_g3, l, lw, tabs_x)
        qa_c, ka_c, va_c, qb_c, kb_c, vb_c, ga_c, gb_c = _proj(xc, mod3, ctx_row, mix_g3, l, lw, tabs_c)
        oa = _na_attention(qa, ka, va, ka_c, va_c, _na_bias_tables(na_rpb[l]))
        ob = _mla_attention(qb, kb, vb, kb_c, vb_c)
        x, h_x, lg = _merge(x, oa, ob, ga, gb, mod3, None, ffn_g3, l, lw)
        if not last:
            oa_c, ob_c = _ctx_attention(qa_c, ka_c, va_c, qb_c, kb_c, vb_c)
            xc, h_c, lg_c = _merge(xc, oa_c, ob_c, ga_c, gb_c, mod3, ctx_row, ffn_g3, l, lw)
            lg = jnp.concatenate([lg, lg_c], axis=1)

        eid, gw, rank, counts = _route(lg, bias_col, tri)
        dest, block_expert, nused, n_slots = _slot_tables(eid, rank, counts)
        gw_t = gw.T
        slots = jnp.zeros((n_slots, d), F32)
        slots = _dispatch(h_x, dest[:, :n_x].reshape(-1), slots)
        if not last:
            slots = _dispatch(h_c, dest[:, n_x:].reshape(-1), slots)
        y_slots = _experts(slots, block_expert, nused, w1, w3, w2, l)
        x = _combine(x, y_slots, dest[:, :n_x].reshape(-1), gw_t[:n_x], mod3, None)
        if not last:
            xc = _combine(xc, y_slots, dest[:, n_x:].reshape(-1), gw_t[n_x:], mod3, ctx_row)
    return x
```

```python
import functools

import jax
import jax.numpy as jnp
from jax import lax
from jax.experimental import pallas as pl
from jax.experimental.pallas import tpu as pltpu

F32 = jnp.float32
BF16 = jnp.bfloat16

D_MODEL = 1024
DEPTH = 2
GRID_W = 64
N_MOD = 6

NA_HEADS = 8
NA_HEAD_DIM = 64
NA_WIN_ROWS = 8
NA_WIN_COLS = 16
NA_WIDTH = NA_HEADS * NA_HEAD_DIM

MLA_HEADS = 8
MLA_NOPE = 64
MLA_ROPE = 32
MLA_QK_DIM = MLA_NOPE + MLA_ROPE
MLA_V_DIM = 64
MLA_Q_LORA = 384
MLA_KV_LORA = 256
MLA_WIDTH = MLA_HEADS * MLA_V_DIM
ROPE_BASE = 10000.0

N_EXPERTS = 64
EXPERTS_PER_GROUP = 8
N_GROUPS = N_EXPERTS // EXPERTS_PER_GROUP
TOP_K = 2

RMS_EPS = 1e-6
NEG_INF = -1e30
LOG2E = 1.4426950408889634

LANES = 128
HEAD_PAD = LANES
MLA_QK_PAD = MLA_HEADS * HEAD_PAD

C_NA = 3 * NA_WIDTH
C_CQ = C_NA + MLA_Q_LORA
C_CKV = C_CQ + MLA_KV_LORA
C_KR = C_CKV + HEAD_PAD
C_ALL = C_KR + 2 * D_MODEL

TM_PROJ = 256
TM_MERGE = 512
TQ_MLA = 256
NA_QROWS = 4
TR_ROUTE = 512
BM_EXP = 256
PACK_W = D_MODEL // 2
PACK_SUB = PACK_W // LANES
TM_MOVE = 256
MOVE_UNROLL = 8
VMEM_LIMIT = 56 * 1024 * 1024


def _sigmoid(v):
    return 1.0 / (1.0 + jnp.exp(-v))


def _rms(v, n):
    return v * lax.rsqrt(jnp.sum(v * v, axis=-1, keepdims=True) * (1.0 / n) + RMS_EPS)


def _pack_rows(v):
    lo = pltpu.bitcast(v[:, :PACK_W].astype(BF16).astype(F32), jnp.uint32)
    hi = pltpu.bitcast(v[:, PACK_W:].astype(BF16).astype(F32), jnp.uint32)
    return (lo >> 16) | (hi & jnp.uint32(0xFFFF0000))


def _unpack_rows(w):
    return pltpu.bitcast(w << 16, F32), pltpu.bitcast(w & jnp.uint32(0xFFFF0000), F32)


def _store_slabs(ref, w):
    for c in range(PACK_SUB):
        ref[pl.ds(c, w.shape[0], stride=PACK_SUB), :] = w[:, c * LANES:(c + 1) * LANES]


def _load_slabs(ref, m):
    return jnp.concatenate([ref[pl.ds(c, m, stride=PACK_SUB), :] for c in range(PACK_SUB)], axis=1)


def _split_bf16(a):
    hi = a.astype(BF16)
    lo = (a - hi.astype(F32)).astype(BF16)
    return hi, lo


def _dot(a, b):
    return jnp.dot(a, b, preferred_element_type=F32)


def _dot_nt(a, b):
    return lax.dot_general(a, b, (((1,), (1,)), ((), ())), preferred_element_type=F32)


def _dot_split_lhs(a, b):
    hi, lo = _split_bf16(a)
    return _dot(hi, b) + _dot(lo, b)


def _params(sem, vmem=None):
    return pltpu.CompilerParams(dimension_semantics=sem, vmem_limit_bytes=vmem)


def _ada_kernel(c_ref, w_ref, b_ref, o_ref):
    cv = c_ref[...]
    s = cv * _sigmoid(cv)
    s_hi, s_lo = _split_bf16(s)
    w_hi, w_lo = _split_bf16(w_ref[...])
    o_ref[...] = _dot(s_hi, w_hi) + _dot(s_lo, w_hi) + _dot(s_hi, w_lo) + b_ref[...]


def _ada(cvec, w_ada, b_ada3, l):
    rows = cvec.shape[0]
    n = N_MOD * D_MODEL
    tn = 512
    return pl.pallas_call(
        _ada_kernel,
        grid=(n // tn,),
        in_specs=[
            pl.BlockSpec((rows, D_MODEL), lambda j: (0, 0)),
            pl.BlockSpec((None, D_MODEL, tn), lambda j: (l, 0, j)),
            pl.BlockSpec((None, 1, tn), lambda j: (l, 0, j)),
        ],
        out_specs=pl.BlockSpec((rows, tn), lambda j: (0, j)),
        out_shape=jax.ShapeDtypeStruct((rows, n), F32),
        compiler_params=_params(("arbitrary",)),
        name="ada_mod",
    )(cvec, w_ada, b_ada3)


def _rope(v, cos, s1, s2):
    return v * cos + pltpu.roll(v, LANES - 8, axis=1) * s1 + pltpu.roll(v, 8, axis=1) * s2


def _proj_kernel(x_ref, sh_ref, sc_ref, g_ref, win_ref, naqg_ref, nakg_ref, cqg_ref, ckvg_ref,
                 wuq_ref, wuk_ref, wuv_ref, qg_ref, kg_ref, cos_ref, s1_ref, s2_ref, e_ref, et_ref,
                 qa_ref, ka_ref, va_ref, qb_ref, kb_ref, vb_ref, ga_ref, gb_ref):
    x = x_ref[...]
    h = _rms(x, D_MODEL) * g_ref[...]
    h = h * (1.0 + sc_ref[...]) + sh_ref[...]
    hb = h.astype(BF16)

    e = e_ref[...]
    et = et_ref[...]

    def headnorm(z, g):
        ss = _dot_split_lhs(z * z, e)
        r = lax.rsqrt(ss * (1.0 / NA_HEAD_DIM) + RMS_EPS)
        return z * _dot_split_lhs(r, et) * g

    zq = _dot(hb, win_ref[:, 0:NA_WIDTH])
    qa_ref[...] = (headnorm(zq, naqg_ref[...]) * (NA_HEAD_DIM ** -0.5 * LOG2E)).astype(BF16)
    zk = _dot(hb, win_ref[:, NA_WIDTH:2 * NA_WIDTH])
    ka_ref[...] = headnorm(zk, nakg_ref[...]).astype(BF16)
    va_ref[...] = _dot(hb, win_ref[:, 2 * NA_WIDTH:C_NA]).astype(BF16)

    cos = cos_ref[...]
    s1 = s1_ref[...]
    s2 = s2_ref[...]

    cq = _rms(_dot(hb, win_ref[:, C_NA:C_CQ]), MLA_Q_LORA) * cqg_ref[...]
    q = _dot(cq.astype(BF16), wuq_ref[...])
    qg = qg_ref[...]
    for hh in range(MLA_HEADS):
        sl = slice(hh * HEAD_PAD, (hh + 1) * HEAD_PAD)
        qn = _rms(q[:, sl], MLA_QK_DIM) * qg
        qb_ref[:, sl] = (_rope(qn, cos, s1, s2) * (MLA_QK_DIM ** -0.5 * LOG2E)).astype(BF16)

    ckv = (_rms(_dot(hb, win_ref[:, C_CQ:C_CKV]), MLA_KV_LORA) * ckvg_ref[...]).astype(BF16)
    kn = _dot(ckv, wuk_ref[...])
    kr = _dot(hb, win_ref[:, C_CKV:C_KR])
    kg = kg_ref[...]
    for hh in range(MLA_HEADS):
        sl = slice(hh * HEAD_PAD, (hh + 1) * HEAD_PAD)
        kh = _rms(kn[:, sl] + kr, MLA_QK_DIM) * kg
        kb_ref[:, sl] = _rope(kh, cos, s1, s2).astype(BF16)
    vb_ref[...] = _dot(ckv, wuv_ref[...]).astype(BF16)

    ga_ref[...] = _sigmoid(_dot(hb, win_ref[:, C_KR:C_KR + D_MODEL])).astype(BF16)
    gb_ref[...] = _sigmoid(_dot(hb, win_ref[:, C_KR + D_MODEL:C_ALL])).astype(BF16)


def _proj(x, mod3, mod_row, gain3, l, lw, tabs):
    bsz, sx, _ = x.shape
    tm = min(TM_PROJ, sx)
    nt = sx // tm
    row = (lambda b: b) if mod_row is None else (lambda b: mod_row)

    def full(shape):
        return pl.BlockSpec(shape, lambda b, i: (0,) * len(shape))

    def tok(width):
        return pl.BlockSpec((None, tm, width), lambda b, i: (b, i, 0))

    def tab():
        return pl.BlockSpec((tm, LANES), lambda b, i: (i, 0))

    in_specs = [
        tok(D_MODEL),
        pl.BlockSpec((None, 1, D_MODEL), lambda b, i: (row(b), 0, 0)),
        pl.BlockSpec((None, 1, D_MODEL), lambda b, i: (row(b), 0, 1)),
        pl.BlockSpec((None, 1, D_MODEL), lambda b, i: (l, 0, 0)),
        full((D_MODEL, C_ALL)),
        full((1, NA_WIDTH)), full((1, NA_WIDTH)),
        full((1, MLA_Q_LORA)), full((1, MLA_KV_LORA)),
        full((MLA_Q_LORA, MLA_QK_PAD)), full((MLA_KV_LORA, MLA_QK_PAD)), full((MLA_KV_LORA, MLA_WIDTH)),
        full((1, HEAD_PAD)), full((1, HEAD_PAD)),
        tab(), tab(), tab(),
        full((NA_WIDTH, LANES)), full((LANES, NA_WIDTH)),
    ]
    widths = (NA_WIDTH, NA_WIDTH, NA_WIDTH, MLA_QK_PAD, MLA_QK_PAD, MLA_WIDTH, D_MODEL, D_MODEL)
    return pl.pallas_call(
        _proj_kernel,
        grid=(bsz, nt),
        in_specs=in_specs,
        out_specs=[tok(w) for w in widths],
        out_shape=[jax.ShapeDtypeStruct((bsz, sx, w), BF16) for w in widths],
        compiler_params=_params(("arbitrary", "arbitrary"), VMEM_LIMIT),
        name="mixer_inputs",
    )(x, mod3, mod3, gain3, lw["w_in"], lw["na_q_g"], lw["na_k_g"], lw["cq_g"], lw["ckv_g"],
      lw["w_uq"], lw["w_uk"], lw["w_uv"], lw["q_g"], lw["k_g"], tabs[0], tabs[1], tabs[2],
      lw["seg"], lw["seg_t"])


def _pair_masks(rows):
    lane = lax.broadcasted_iota(jnp.int32, (rows, LANES), 1)
    return lane < NA_HEAD_DIM


def _ones_pad(v, low):
    one = jnp.ones_like(v)
    return jnp.where(low, v, one), jnp.where(low, one, v)


def _pair_finish(o_a, o_b, low):
    num = jnp.where(low, o_a, o_b)
    den = pltpu.roll(jnp.where(low, o_b, o_a), NA_HEAD_DIM, axis=1)
    return (num / den).astype(BF16)


def _na_kernel(q_ref, k_ref, v_ref, kc_ref, vc_ref, bias_ref, o_ref, va_ref, vb_ref, vca_ref, vcb_ref):
    qb = pl.program_id(2)
    n_rows = k_ref.shape[0] // GRID_W
    nwin = NA_WIN_ROWS * GRID_W

    @pl.when(qb == 0)
    def _():
        va_ref[...], vb_ref[...] = _ones_pad(v_ref[...], _pair_masks(v_ref.shape[0]))
        vca_ref[...], vcb_ref[...] = _ones_pad(vc_ref[...], _pair_masks(vc_ref.shape[0]))

    low = _pair_masks(GRID_W)
    kc = kc_ref[...]
    rows = range(NA_QROWS)
    starts, scores = [], []
    for j in rows:
        r = qb * NA_QROWS + j
        rs = jnp.clip(r - NA_WIN_ROWS // 2, 0, n_rows - NA_WIN_ROWS)
        start = pl.multiple_of(rs * GRID_W, GRID_W)
        q = q_ref[j * GRID_W:(j + 1) * GRID_W, :]
        zero = jnp.zeros_like(q)
        qs = jnp.concatenate([jnp.where(low, q, zero), jnp.where(low, zero, q)], axis=0)
        s = _dot_nt(qs, k_ref[pl.ds(start, nwin), :]) + bias_ref[r - rs].reshape(2 * GRID_W, nwin)
        starts.append(start)
        scores.append((s, _dot_nt(qs, kc)))
    probs = []
    for s, sc in scores:
        m = jnp.maximum(jnp.max(s, axis=-1, keepdims=True), jnp.max(sc, axis=-1, keepdims=True))
        probs.append((jnp.exp2(s - m).astype(BF16), jnp.exp2(sc - m).astype(BF16)))
    outs = []
    for start, (p, pc) in zip(starts, probs):
        o_a = _dot(p[:GRID_W], va_ref[pl.ds(start, nwin), :]) + _dot(pc[:GRID_W], vca_ref[...])
        o_b = _dot(p[GRID_W:], vb_ref[pl.ds(start, nwin), :]) + _dot(pc[GRID_W:], vcb_ref[...])
        outs.append(_pair_finish(o_a, o_b, low))
    o_ref[...] = jnp.concatenate(outs, axis=0)


def _na_attention(qa, ka, va, kc, vc, bias):
    bsz, s, _ = qa.shape
    nc = kc.shape[1]
    nq = NA_QROWS * GRID_W
    npairs = NA_WIDTH // LANES
    nwin = NA_WIN_ROWS * GRID_W
    whole = pl.BlockSpec((None, s, LANES), lambda p, b, qb: (b, 0, p))
    cspec = pl.BlockSpec((None, nc, LANES), lambda p, b, qb: (b, 0, p))
    return pl.pallas_call(
        _na_kernel,
        grid=(npairs, bsz, s // nq),
        in_specs=[pl.BlockSpec((None, nq, LANES), lambda p, b, qb: (b, qb, p)),
                  whole, whole, cspec, cspec,
                  pl.BlockSpec((NA_WIN_ROWS, 2, GRID_W, nwin), lambda p, b, qb: (0, p, 0, 0))],
        out_specs=pl.BlockSpec((None, nq, LANES), lambda p, b, qb: (b, qb, p)),
        out_shape=jax.ShapeDtypeStruct((bsz, s, NA_WIDTH), BF16),
        scratch_shapes=[pltpu.VMEM((s, LANES), BF16), pltpu.VMEM((s, LANES), BF16),
                        pltpu.VMEM((nc, LANES), BF16), pltpu.VMEM((nc, LANES), BF16)],
        compiler_params=_params(("arbitrary", "arbitrary", "arbitrary"), VMEM_LIMIT),
        name="na_attention",
    )(qa, ka, va, kc, vc, bias)


def _mla_kernel(q_ref, k_ref, v_ref, kc_ref, vc_ref, o_ref, va_ref, vb_ref, vca_ref, vcb_ref):
    @pl.when(pl.program_id(2) == 0)
    def _():
        va_ref[...], vb_ref[...] = _ones_pad(v_ref[...], _pair_masks(v_ref.shape[0]))
        vca_ref[...], vcb_ref[...] = _ones_pad(vc_ref[...], _pair_masks(vc_ref.shape[0]))

    scores = []
    for hh in range(2):
        sl = slice(hh * HEAD_PAD, (hh + 1) * HEAD_PAD)
        q = q_ref[:, sl]
        scores.append((_dot_nt(q, k_ref[:, sl]), _dot_nt(q, kc_ref[:, sl])))
    probs = []
    for s, sc in scores:
        m = jnp.maximum(jnp.max(s, axis=-1, keepdims=True), jnp.max(sc, axis=-1, keepdims=True))
        probs.append((jnp.exp2(s - m).astype(BF16), jnp.exp2(sc - m).astype(BF16)))
    outs = [_dot(p, vx_ref[...]) + _dot(pc, vcx_ref[...])
            for (p, pc), (vx_ref, vcx_ref) in zip(probs, ((va_ref, vca_ref), (vb_ref, vcb_ref)))]
    o_ref[...] = _pair_finish(outs[0], outs[1], _pair_masks(q_ref.shape[0]))


def _mla_attention(qb, kb, vb, kcb, vcb):
    bsz, s, _ = qb.shape
    nc = kcb.shape[1]
    tq = TQ_MLA
    npairs = MLA_WIDTH // LANES
    return pl.pallas_call(
        _mla_kernel,
        grid=(bsz, npairs, s // tq),
        in_specs=[pl.BlockSpec((None, tq, 2 * HEAD_PAD), lambda b, p, i: (b, i, p)),
                  pl.BlockSpec((None, s, 2 * HEAD_PAD), lambda b, p, i: (b, 0, p)),
                  pl.BlockSpec((None, s, LANES), lambda b, p, i: (b, 0, p)),
                  pl.BlockSpec((None, nc, 2 * HEAD_PAD), lambda b, p, i: (b, 0, p)),
                  pl.BlockSpec((None, nc, LANES), lambda b, p, i: (b, 0, p))],
        out_specs=pl.BlockSpec((None, tq, LANES), lambda b, p, i: (b, i, p)),
        out_shape=jax.ShapeDtypeStruct((bsz, s, MLA_WIDTH), BF16),
        scratch_shapes=[pltpu.VMEM((s, LANES), BF16), pltpu.VMEM((s, LANES), BF16),
                        pltpu.VMEM((nc, LANES), BF16), pltpu.VMEM((nc, LANES), BF16)],
        compiler_params=_params(("arbitrary", "arbitrary", "arbitrary"), VMEM_LIMIT),
        name="mla_attention",
    )(qb, kb, vb, kcb, vcb)


def _softmax2_pv(s, v):
    p = jnp.exp2(s - jnp.max(s, axis=-1, keepdims=True))
    return _dot(p.astype(BF16), v) / jnp.sum(p, axis=-1, keepdims=True)


def _ctx_kernel(qa_ref, ka_ref, va_ref, qb_ref, kb_ref, vb_ref, oa_ref, ob_ref):
    n = qa_ref.shape[0]
    low = _pair_masks(n)
    qa = qa_ref[...]
    zero = jnp.zeros_like(qa)
    ka = ka_ref[...]
    va = va_ref[...]
    o_lo = _softmax2_pv(_dot_nt(jnp.where(low, qa, zero), ka), va)
    o_hi = _softmax2_pv(_dot_nt(jnp.where(low, zero, qa), ka), va)
    oa_ref[...] = jnp.where(low, o_lo, o_hi).astype(BF16)
    vb = vb_ref[...]
    outs = []
    for hh in range(2):
        sl = slice(hh * HEAD_PAD, (hh + 1) * HEAD_PAD)
        outs.append(_softmax2_pv(_dot_nt(qb_ref[:, sl], kb_ref[:, sl]), vb))
    ob_ref[...] = jnp.where(low, outs[0], outs[1]).astype(BF16)


def _ctx_attention(qa, ka, va, qb, kb, vb):
    bsz, n, _ = qa.shape
    npairs = NA_WIDTH // LANES
    narrow = pl.BlockSpec((None, n, LANES), lambda b, p: (b, 0, p))
    wide = pl.BlockSpec((None, n, 2 * HEAD_PAD), lambda b, p: (b, 0, p))
    return pl.pallas_call(
        _ctx_kernel,
        grid=(bsz, npairs),
        in_specs=[narrow, narrow, narrow, wide, wide, narrow],
        out_specs=[narrow, narrow],
        out_shape=[jax.ShapeDtypeStruct((bsz, n, NA_WIDTH), BF16),
                   jax.ShapeDtypeStruct((bsz, n, MLA_WIDTH), BF16)],
        compiler_params=_params(("arbitrary", "arbitrary")),
        name="ctx_attention",
    )(qa, ka, va, qb, kb, vb)


def _merge_kernel(x_ref, oa_ref, ob_ref, ga_ref, gb_ref, woa_ref, wob_ref, wout_ref, gm_ref,
                  g_ref, sh_ref, sc_ref, wrh_ref, wrl_ref, xo_ref, h_ref, lg_ref):
    y = (ga_ref[...].astype(F32) * _dot(oa_ref[...], woa_ref[...])
         + gb_ref[...].astype(F32) * _dot(ob_ref[...], wob_ref[...]))
    xn = x_ref[...] + gm_ref[...] * _dot(y.astype(BF16), wout_ref[...])
    xo_ref[...] = xn
    h = _rms(xn, D_MODEL) * g_ref[...]
    h = h * (1.0 + sc_ref[...]) + sh_ref[...]
    _store_slabs(h_ref, _pack_rows(h))
    h_hi, h_lo = _split_bf16(h)
    wrh = wrh_ref[...]
    lg_ref[...] = _dot_nt(wrh, h_hi) + _dot_nt(wrh, h_lo) + _dot_nt(wrl_ref[...], h_hi)


def _merge(x, oa, ob, ga, gb, mod3, mod_row, gain3, l, lw):
    bsz, sx, _ = x.shape
    tm = min(TM_MERGE, sx)
    nt = sx // tm
    row = (lambda b: b) if mod_row is None else (lambda b: mod_row)

    def full(shape):
        return pl.BlockSpec(shape, lambda b, i: (0,) * len(shape))

    def tok(width):
        return pl.BlockSpec((None, tm, width), lambda b, i: (b, i, 0))

    def mod(chunk):
        return pl.BlockSpec((None, 1, D_MODEL), lambda b, i: (row(b), 0, chunk))

    return pl.pallas_call(
        _merge_kernel,
        grid=(bsz, nt),
        in_specs=[tok(D_MODEL), tok(NA_WIDTH), tok(MLA_WIDTH), tok(D_MODEL), tok(D_MODEL),
                  full((NA_WIDTH, D_MODEL)), full((MLA_WIDTH, D_MODEL)), full((D_MODEL, D_MODEL)),
                  mod(2),
                  pl.BlockSpec((None, 1, D_MODEL), lambda b, i: (l, 0, 0)),
                  mod(3), mod(4),
                  full((N_EXPERTS, D_MODEL)), full((N_EXPERTS, D_MODEL))],
        out_specs=[tok(D_MODEL),
                   pl.BlockSpec((tm * PACK_SUB, LANES), lambda b, i: (b * nt + i, 0)),
                   pl.BlockSpec((N_EXPERTS, tm), lambda b, i: (0, b * nt + i))],
        out_shape=[jax.ShapeDtypeStruct((bsz, sx, D_MODEL), F32),
                   jax.ShapeDtypeStruct((bsz * sx * PACK_SUB, LANES), jnp.uint32),
                   jax.ShapeDtypeStruct((N_EXPERTS, bsz * sx), F32)],
        compiler_params=_params(("arbitrary", "arbitrary"), VMEM_LIMIT),
        name="merge_branches",
    )(x, oa, ob, ga, gb, lw["w_oa"], lw["w_ob"], lw["w_out"], mod3, gain3, mod3, mod3,
      lw["wr_hi"], lw["wr_lo"])


def _route_kernel(lg_ref, bias_ref, tri_ref, eid_ref, gw_ref, rank_ref, cnt_ref, run_ref):
    @pl.when(pl.program_id(0) == 0)
    def _():
        run_ref[...] = jnp.zeros_like(run_ref)

    tr = lg_ref.shape[1]
    s = _sigmoid(lg_ref[...])
    biased = s + bias_ref[...]
    sub = lax.broadcasted_iota(jnp.int32, (EXPERTS_PER_GROUP, tr), 0).astype(F32)
    none = float(EXPERTS_PER_GROUP)
    best = jnp.full((1, tr), -jnp.inf, F32)
    e1 = jnp.zeros((1, tr), F32)
    e2 = jnp.zeros((1, tr), F32)
    for g in range(N_GROUPS):
        bg = biased[g * EXPERTS_PER_GROUP:(g + 1) * EXPERTS_PER_GROUP]
        m1 = jnp.max(bg, axis=0, keepdims=True)
        i1 = jnp.min(jnp.where(bg == m1, sub, none), axis=0, keepdims=True)
        rest = jnp.where(sub == i1, -jnp.inf, bg)
        m2 = jnp.max(rest, axis=0, keepdims=True)
        i2 = jnp.min(jnp.where(rest == m2, sub, none), axis=0, keepdims=True)
        score = m1 + m2
        better = score > best
        best = jnp.where(better, score, best)
        e1 = jnp.where(better, g * EXPERTS_PER_GROUP + i1, e1)
        e2 = jnp.where(better, g * EXPERTS_PER_GROUP + i2, e2)

    rowid = lax.broadcasted_iota(jnp.int32, (N_EXPERTS, tr), 0).astype(F32)
    is1 = rowid == e1
    is2 = rowid == e2
    w1 = jnp.sum(jnp.where(is1, s, 0.0), axis=0, keepdims=True)
    w2 = jnp.sum(jnp.where(is2, s, 0.0), axis=0, keepdims=True)
    tot = w1 + w2
    gw_ref[0:1, :] = w1 / tot
    gw_ref[1:2, :] = w2 / tot
    eid_ref[0:1, :] = e1.astype(jnp.int32)
    eid_ref[1:2, :] = e2.astype(jnp.int32)

    onehot = jnp.where(is1 | is2, 1.0, 0.0)
    before = _dot(onehot.astype(BF16), tri_ref[...]) + run_ref[...]
    rank_ref[0:1, :] = jnp.sum(jnp.where(is1, before, 0.0), axis=0, keepdims=True).astype(jnp.int32)
    rank_ref[1:2, :] = jnp.sum(jnp.where(is2, before, 0.0), axis=0, keepdims=True).astype(jnp.int32)
    run_ref[...] = run_ref[...] + jnp.sum(onehot, axis=1, keepdims=True)
    cnt_ref[...] = run_ref[...]


def _route(logits, bias_col, tri):
    t = logits.shape[1]
    tr = TR_ROUTE
    pair = pl.BlockSpec((TOP_K, tr), lambda i: (0, i))
    return pl.pallas_call(
        _route_kernel,
        grid=(t // tr,),
        in_specs=[pl.BlockSpec((N_EXPERTS, tr), lambda i: (0, i)),
                  pl.BlockSpec((N_EXPERTS, 1), lambda i: (0, 0)),
                  pl.BlockSpec((tr, tr), lambda i: (0, 0))],
        out_specs=[pair, pair, pair, pl.BlockSpec((N_EXPERTS, 1), lambda i: (0, 0))],
        out_shape=[jax.ShapeDtypeStruct((TOP_K, t), jnp.int32),
                   jax.ShapeDtypeStruct((TOP_K, t), F32),
                   jax.ShapeDtypeStruct((TOP_K, t), jnp.int32),
                   jax.ShapeDtypeStruct((N_EXPERTS, 1), F32)],
        scratch_shapes=[pltpu.VMEM((N_EXPERTS, 1), F32)],
        compiler_params=_params(("arbitrary",)),
        name="route",
    )(logits, bias_col, tri)


def _dispatch_kernel(dest_ref, h_ref, slots_in_ref, slots_ref, sem, *, n_tok):
    del slots_in_ref
    tm = h_ref.shape[0] // PACK_SUB
    base = pl.program_id(0) * tm

    def copy(r, k):
        d = dest_ref[k * n_tok + base + r]
        return pltpu.make_async_copy(h_ref.at[pl.ds(pl.multiple_of(r * PACK_SUB, PACK_SUB), PACK_SUB)],
                                     slots_ref.at[pl.ds(pl.multiple_of(d * PACK_SUB, PACK_SUB), PACK_SUB)], sem)

    def issue(r, carry):
        for k in range(TOP_K):
            copy(r, k).start(priority=k)
        return carry

    def drain(r, carry):
        for k in range(TOP_K):
            copy(r, k).wait()
        return carry

    lax.fori_loop(0, tm, issue, 0, unroll=MOVE_UNROLL)
    lax.fori_loop(0, tm, drain, 0, unroll=MOVE_UNROLL)


def _dispatch(h, dest_flat, slots):
    n_tok = h.shape[0] // PACK_SUB
    tm = TM_MOVE
    return pl.pallas_call(
        functools.partial(_dispatch_kernel, n_tok=n_tok),
        grid_spec=pltpu.PrefetchScalarGridSpec(
            num_scalar_prefetch=1,
            grid=(n_tok // tm,),
            in_specs=[pl.BlockSpec((tm * PACK_SUB, LANES), lambda i, dest: (i, 0)),
                      pl.BlockSpec(memory_space=pl.ANY)],
            out_specs=pl.BlockSpec(memory_space=pl.ANY),
            scratch_shapes=[pltpu.SemaphoreType.DMA],
        ),
        out_shape=jax.ShapeDtypeStruct(slots.shape, slots.dtype),
        input_output_aliases={2: 0},
        compiler_params=_params(("arbitrary",)),
        name="moe_dispatch",
    )(dest_flat, h, slots)


def _expert_kernel(be_ref, nused_ref, x_ref, w1_ref, w3_ref, w2_ref, y_ref, w1b, w3b, w2b):
    i = pl.program_id(0)
    used = i < nused_ref[0]

    @pl.when(used)
    def _():
        prev = be_ref[jnp.maximum(i - 1, 0)]

        @pl.when((i == 0) | (be_ref[i] != prev))
        def _():
            w1b[...] = w1_ref[...].astype(BF16)
            w3b[...] = w3_ref[...].astype(BF16)
            w2b[...] = w2_ref[...].astype(BF16)

        bm = x_ref.shape[0] // PACK_SUB
        x_lo, x_hi = _unpack_rows(_load_slabs(x_ref, bm))
        xb = jnp.concatenate([x_lo.astype(BF16), x_hi.astype(BF16)], axis=1)
        a = _dot(xb, w1b[...])
        b = _dot(xb, w3b[...])
        _store_slabs(y_ref, _pack_rows(_dot((a * _sigmoid(a) * b).astype(BF16), w2b[...])))

    @pl.when(jnp.logical_not(used))
    def _():
        y_ref[...] = jnp.zeros_like(y_ref)


def _experts(slots, block_expert, nused, w1, w3, w2, l):
    n_slots = slots.shape[0] // PACK_SUB
    bm = BM_EXP
    de = w1.shape[-1]
    rows = pl.BlockSpec((bm * PACK_SUB, LANES), lambda i, be, nu: (i, 0))
    wspec_in = pl.BlockSpec((None, None, D_MODEL, de), lambda i, be, nu: (l, be[i], 0, 0))
    wspec_out = pl.BlockSpec((None, None, de, D_MODEL), lambda i, be, nu: (l, be[i], 0, 0))
    return pl.pallas_call(
        _expert_kernel,
        grid_spec=pltpu.PrefetchScalarGridSpec(
            num_scalar_prefetch=2,
            grid=(n_slots // bm,),
            in_specs=[rows, wspec_in, wspec_in, wspec_out],
            out_specs=rows,
            scratch_shapes=[pltpu.VMEM((D_MODEL, de), BF16), pltpu.VMEM((D_MODEL, de), BF16),
                            pltpu.VMEM((de, D_MODEL), BF16)],
        ),
        out_shape=jax.ShapeDtypeStruct(slots.shape, slots.dtype),
        compiler_params=_params(("arbitrary",), VMEM_LIMIT),
        name="moe_experts",
    )(block_expert, nused, slots, w1, w3, w2)


def _combine_kernel(dest_ref, x_ref, gw_ref, gf_ref, y_hbm, o_ref, ybuf, sem, *, n_tok):
    tm = x_ref.shape[0]
    i = pl.program_id(0)
    nsteps = pl.num_programs(0)

    def copy(step, r, k):
        slot = step % 2
        d = dest_ref[k * n_tok + step * tm + r]
        return pltpu.make_async_copy(y_hbm.at[pl.ds(pl.multiple_of(d * PACK_SUB, PACK_SUB), PACK_SUB)],
                                     ybuf.at[slot, k, pl.ds(pl.multiple_of(r * PACK_SUB, PACK_SUB), PACK_SUB)],
                                     sem.at[slot])

    def issue(step):
        def body(r, carry):
            for k in range(TOP_K):
                copy(step, r, k).start(priority=k)
            return carry
        lax.fori_loop(0, tm, body, 0, unroll=MOVE_UNROLL)

    @pl.when(i == 0)
    def _():
        issue(i)

    @pl.when(i + 1 < nsteps)
    def _():
        issue(i + 1)

    def drain(r, carry):
        for k in range(TOP_K):
            copy(i, r, k).wait()
        return carry
    lax.fori_loop(0, tm, drain, 0, unroll=MOVE_UNROLL)

    slot = i % 2
    gw = gw_ref[...]
    halves = [_unpack_rows(_load_slabs(ybuf.at[slot, k], tm)) for k in range(TOP_K)]
    gf = gf_ref[...]
    for part in range(2):
        cols = slice(part * PACK_W, (part + 1) * PACK_W)
        y = gw[:, 0:1] * halves[0][part] + gw[:, 1:2] * halves[1][part]
        o_ref[:, cols] = x_ref[:, cols] + gf[:, cols] * y


def _combine(x, y_slots, dest_flat, gw_t, mod3, mod_row):
    bsz, sx, _ = x.shape
    n_tok = bsz * sx
    tm = TM_MOVE
    nt = sx // tm
    row = (lambda i: i // nt) if mod_row is None else (lambda i: mod_row)
    out = pl.pallas_call(
        functools.partial(_combine_kernel, n_tok=n_tok),
        grid_spec=pltpu.PrefetchScalarGridSpec(
            num_scalar_prefetch=1,
            grid=(n_tok // tm,),
            in_specs=[pl.BlockSpec((tm, D_MODEL), lambda i, dest: (i, 0)),
                      pl.BlockSpec((tm, TOP_K), lambda i, dest: (i, 0)),
                      pl.BlockSpec((None, 1, D_MODEL), lambda i, dest: (row(i), 0, 5)),
                      pl.BlockSpec(memory_space=pl.ANY)],
            out_specs=pl.BlockSpec((tm, D_MODEL), lambda i, dest: (i, 0)),
            scratch_shapes=[pltpu.VMEM((2, TOP_K, tm * PACK_SUB, LANES), jnp.uint32),
                            pltpu.SemaphoreType.DMA((2,))],
        ),
        out_shape=jax.ShapeDtypeStruct((n_tok, D_MODEL), F32),
        compiler_params=_params(("arbitrary",)),
        name="moe_combine",
    )(dest_flat, x.reshape(n_tok, D_MODEL), gw_t, mod3, y_slots)
    return out.reshape(bsz, sx, D_MODEL)


def _layer_weights(l, w_in, na_q_g, na_k_g, mla_cq_g, w_uq, mla_ckv_g, w_ukv, mla_q_g, mla_k_g,
                   w_oa, w_ob, w_out, w_router):
    wi = w_in[l]
    c0 = C_NA + MLA_Q_LORA + MLA_KV_LORA
    kr_cols = jnp.zeros((D_MODEL, HEAD_PAD), F32).at[:, MLA_NOPE:MLA_QK_DIM].set(wi[:, c0:c0 + MLA_ROPE])
    w_in_arr = jnp.concatenate([wi[:, :c0], kr_cols, wi[:, c0 + MLA_ROPE:]], axis=1).astype(BF16)

    def pad_heads(w, width):
        w = w.reshape(w.shape[0], MLA_HEADS, width)
        return jnp.pad(w, ((0, 0), (0, 0), (0, HEAD_PAD - width))).reshape(w.shape[0], MLA_QK_PAD)

    ukv = w_ukv[l].reshape(MLA_KV_LORA, MLA_HEADS, MLA_NOPE + MLA_V_DIM)
    seg = (jnp.arange(NA_WIDTH)[:, None] // NA_HEAD_DIM == jnp.arange(LANES)[None, :]).astype(BF16)
    wr_t = w_router.T
    wr_hi = wr_t.astype(BF16)
    return {
        "w_in": w_in_arr,
        "na_q_g": jnp.tile(na_q_g[l], NA_HEADS)[None, :],
        "na_k_g": jnp.tile(na_k_g[l], NA_HEADS)[None, :],
        "cq_g": mla_cq_g[l][None, :],
        "ckv_g": mla_ckv_g[l][None, :],
        "w_uq": pad_heads(w_uq[l], MLA_QK_DIM).astype(BF16),
        "w_uk": pad_heads(ukv[:, :, :MLA_NOPE].reshape(MLA_KV_LORA, -1), MLA_NOPE).astype(BF16),
        "w_uv": ukv[:, :, MLA_NOPE:].reshape(MLA_KV_LORA, MLA_WIDTH).astype(BF16),
        "q_g": jnp.pad(mla_q_g[l], (0, HEAD_PAD - MLA_QK_DIM))[None, :],
        "k_g": jnp.pad(mla_k_g[l], (0, HEAD_PAD - MLA_QK_DIM))[None, :],
        "seg": seg,
        "seg_t": seg.T,
        "w_oa": w_oa[l].astype(BF16),
        "w_ob": w_ob[l].astype(BF16),
        "w_out": w_out[l].astype(BF16),
        "wr_hi": wr_hi,
        "wr_lo": (wr_t - wr_hi.astype(F32)).astype(BF16),
    }


def _rope_tables(s):
    half = MLA_ROPE // 4
    pos = jnp.arange(s, dtype=jnp.int32)
    inv = ROPE_BASE ** (-jnp.arange(half, dtype=F32) / half)
    ang_r = (pos // GRID_W).astype(F32)[:, None] * inv[None, :]
    ang_c = (pos % GRID_W).astype(F32)[:, None] * inv[None, :]
    zeros = jnp.zeros((s, half), F32)
    lead = jnp.zeros((s, MLA_NOPE), F32)
    tail = jnp.zeros((s, HEAD_PAD - MLA_QK_DIM), F32)
    cos = jnp.concatenate([lead + 1.0, jnp.cos(ang_r), jnp.cos(ang_r), jnp.cos(ang_c), jnp.cos(ang_c), tail + 1.0], 1)
    s1 = jnp.concatenate([lead, -jnp.sin(ang_r), zeros, -jnp.sin(ang_c), zeros, tail], 1)
    s2 = jnp.concatenate([lead, zeros, jnp.sin(ang_r), zeros, jnp.sin(ang_c), tail], 1)
    return cos, s1, s2


def _na_bias_tables(rpb):
    d = jnp.arange(NA_WIN_ROWS)[:, None]
    i = jnp.arange(NA_WIN_ROWS)[None, :]
    qc = jnp.arange(GRID_W)[:, None]
    kc = jnp.arange(GRID_W)[None, :]
    cs = jnp.clip(qc - NA_WIN_COLS // 2, 0, GRID_W - NA_WIN_COLS)
    cvalid = (kc >= cs) & (kc < cs + NA_WIN_COLS)
    ohr = jax.nn.one_hot(i - d + NA_WIN_ROWS - 1, 2 * NA_WIN_ROWS - 1, dtype=F32)
    ohc = jax.nn.one_hot(jnp.clip(kc - qc + NA_WIN_COLS - 1, 0, 2 * NA_WIN_COLS - 2), 2 * NA_WIN_COLS - 1, dtype=F32)
    t = jnp.einsum("dia,hab,qkb->dhqik", ohr, rpb.astype(F32), ohc, precision=lax.Precision.HIGHEST)
    t = jnp.where(cvalid[None, None, :, None, :], t * LOG2E, NEG_INF)
    return t.reshape(NA_WIN_ROWS, rpb.shape[0], GRID_W, NA_WIN_ROWS * GRID_W)


def _slot_tables(eid, rank, counts):
    bm = BM_EXP
    counts = counts[:, 0].astype(jnp.int32)
    padded = ((counts + bm - 1) // bm) * bm
    pad_end = jnp.cumsum(padded)
    pad_start = pad_end - padded
    experts = jnp.arange(N_EXPERTS, dtype=jnp.int32)
    dest = rank + jnp.sum(jnp.where(eid[..., None] == experts, pad_start, 0), axis=-1)
    m = eid.shape[1] * TOP_K
    n_blocks = -(-m // bm) + N_EXPERTS
    blk = jnp.arange(n_blocks, dtype=jnp.int32) * bm
    block_expert = jnp.minimum(jnp.sum(pad_end[None, :] <= blk[:, None], axis=1), N_EXPERTS - 1).astype(jnp.int32)
    nused = (pad_end[-1:] // bm).astype(jnp.int32)
    return dest.astype(jnp.int32), block_expert, nused, n_blocks * bm


def kernel(x, c, ctx, c_ctx, w_ada, b_ada, norm_mix_g, norm_ffn_g, w_in, na_q_g, na_k_g, na_rpb,
           mla_cq_g, w_uq, mla_ckv_g, w_ukv, mla_q_g, mla_k_g, w_oa, w_ob, w_out,
           w_router, router_bias, w1, w3, w2):
    bsz, s, d = x.shape
    n_ctx = ctx.shape[1]
    ctx_row = bsz
    pad_rows = -(bsz + 1) % 8
    cvec = jnp.concatenate([c, c_ctx[None, :], jnp.zeros((pad_rows, d), F32)], axis=0)
    b_ada3 = b_ada[:, None, :]
    mix_g3 = norm_mix_g[:, None, :]
    ffn_g3 = norm_ffn_g[:, None, :]
    tabs_x = _rope_tables(s)
    tabs_c = (jnp.ones((n_ctx, LANES), F32), jnp.zeros((n_ctx, LANES), F32), jnp.zeros((n_ctx, LANES), F32))
    tri = (jnp.arange(TR_ROUTE)[:, None] < jnp.arange(TR_ROUTE)[None, :]).astype(BF16)
    bias_col = router_bias.astype(F32)[:, None]
    n_x = bsz * s

    xc = ctx
    for l in range(DEPTH):
        last = l == DEPTH - 1
        lw = _layer_weights(l, w_in, na_q_g, na_k_g, mla_cq_g, w_uq, mla_ckv_g, w_ukv, mla_q_g, mla_k_g,
                            w_oa, w_ob, w_out, w_router)
        mod3 = _ada(cvec, w_ada, b_ada3, l)[:, None, :]
        qa, ka, va, qb, kb, vb, ga, gb = _proj(x, mod3, None, mix_g3, l, lw, tabs_x)
        qa_c, ka_c, va_c, qb_c, kb_c, vb_c, ga_c, gb_c = _proj(xc, mod3, ctx_row, mix_g3, l, lw, tabs_c)
        oa = _na_attention(qa, ka, va, ka_c, va_c, _na_bias_tables(na_rpb[l]))
        ob = _mla_attention(qb, kb, vb, kb_c, vb_c)
        x, h_x, lg = _merge(x, oa, ob, ga, gb, mod3, None, ffn_g3, l, lw)
        if not last:
            oa_c, ob_c = _ctx_attention(qa_c, ka_c, va_c, qb_c, kb_c, vb_c)
            xc, h_c, lg_c = _merge(xc, oa_c, ob_c, ga_c, gb_c, mod3, ctx_row, ffn_g3, l, lw)
            lg = jnp.concatenate([lg, lg_c], axis=1)

        eid, gw, rank, counts = _route(lg, bias_col, tri)
        dest, block_expert, nused, n_slots = _slot_tables(eid, rank, counts)
        gw_t = gw.T
        slots = jnp.zeros((n_slots * PACK_SUB, LANES), jnp.uint32)
        slots = _dispatch(h_x, dest[:, :n_x].reshape(-1), slots)
        if not last:
            slots = _dispatch(h_c, dest[:, n_x:].reshape(-1), slots)
        y_slots = _experts(slots, block_expert, nused, w1, w3, w2, l)
        x = _combine(x, y_slots, dest[:, :n_x].reshape(-1), gw_t[:n_x], mod3, None)
        if not last:
            xc = _combine(xc, y_slots, dest[:, n_x:].reshape(-1), gw_t[n_x:], mod3, ctx_row)
    return x
```

```python
import functools

import jax
import jax.numpy as jnp
from jax import lax
from jax.experimental import pallas as pl
from jax.experimental.pallas import tpu as pltpu

F32 = jnp.float32
BF16 = jnp.bfloat16

D_MODEL = 1024
DEPTH = 2
GRID_W = 64
N_MOD = 6

NA_HEADS = 8
NA_HEAD_DIM = 64
NA_WIN_ROWS = 8
NA_WIN_COLS = 16
NA_WIDTH = NA_HEADS * NA_HEAD_DIM

MLA_HEADS = 8
MLA_NOPE = 64
MLA_ROPE = 32
MLA_QK_DIM = MLA_NOPE + MLA_ROPE
MLA_V_DIM = 64
MLA_Q_LORA = 384
MLA_KV_LORA = 256
MLA_WIDTH = MLA_HEADS * MLA_V_DIM
ROPE_BASE = 10000.0

N_EXPERTS = 64
EXPERTS_PER_GROUP = 8
N_GROUPS = N_EXPERTS // EXPERTS_PER_GROUP
TOP_K = 2

RMS_EPS = 1e-6
NEG_INF = -1e30
LOG2E = 1.4426950408889634

LANES = 128
HEAD_PAD = LANES
MLA_QK_PAD = MLA_HEADS * HEAD_PAD

C_NA = 3 * NA_WIDTH
C_CQ = C_NA + MLA_Q_LORA
C_CKV = C_CQ + MLA_KV_LORA
C_KR = C_CKV + HEAD_PAD
C_ALL = C_KR + 2 * D_MODEL

TM_PROJ = 256
TM_MERGE = 512
TQ_MLA = 512
NA_QROWS = 8
TR_ROUTE = 512
BM_EXP = 256
PACK_W = D_MODEL // 2
PACK_SUB = PACK_W // LANES
TM_MOVE = 256
MOVE_UNROLL = 8
VMEM_LIMIT = 56 * 1024 * 1024


def _sigmoid(v):
    return 1.0 / (1.0 + jnp.exp(-v))


def _rms(v, n):
    return v * lax.rsqrt(jnp.sum(v * v, axis=-1, keepdims=True) * (1.0 / n) + RMS_EPS)


def _pack_rows(v):
    lo = pltpu.bitcast(v[:, :PACK_W].astype(BF16).astype(F32), jnp.uint32)
    hi = pltpu.bitcast(v[:, PACK_W:].astype(BF16).astype(F32), jnp.uint32)
    return (lo >> 16) | (hi & jnp.uint32(0xFFFF0000))


def _unpack_rows(w):
    return pltpu.bitcast(w << 16, F32), pltpu.bitcast(w & jnp.uint32(0xFFFF0000), F32)


def _store_slabs(ref, w):
    for c in range(PACK_SUB):
        ref[pl.ds(c, w.shape[0], stride=PACK_SUB), :] = w[:, c * LANES:(c + 1) * LANES]


def _load_slabs(ref, m):
    return jnp.concatenate([ref[pl.ds(c, m, stride=PACK_SUB), :] for c in range(PACK_SUB)], axis=1)


def _split_bf16(a):
    hi = a.astype(BF16)
    lo = (a - hi.astype(F32)).astype(BF16)
    return hi, lo


def _dot(a, b):
    return jnp.dot(a, b, preferred_element_type=F32)


def _dot_nt(a, b):
    return lax.dot_general(a, b, (((1,), (1,)), ((), ())), preferred_element_type=F32)


def _params(sem, vmem=None):
    return pltpu.CompilerParams(dimension_semantics=sem, vmem_limit_bytes=vmem)


def _ada_kernel(c_ref, w_ref, b_ref, o_ref):
    cv = c_ref[...]
    s = cv * _sigmoid(cv)
    s_hi, s_lo = _split_bf16(s)
    w_hi, w_lo = _split_bf16(w_ref[...])
    o_ref[...] = _dot(s_hi, w_hi) + _dot(s_lo, w_hi) + _dot(s_hi, w_lo) + b_ref[...]


def _ada(cvec, w_ada, b_ada3, l):
    rows = cvec.shape[0]
    n = N_MOD * D_MODEL
    tn = 512
    return pl.pallas_call(
        _ada_kernel,
        grid=(n // tn,),
        in_specs=[
            pl.BlockSpec((rows, D_MODEL), lambda j: (0, 0)),
            pl.BlockSpec((None, D_MODEL, tn), lambda j: (l, 0, j)),
            pl.BlockSpec((None, 1, tn), lambda j: (l, 0, j)),
        ],
        out_specs=pl.BlockSpec((rows, tn), lambda j: (0, j)),
        out_shape=jax.ShapeDtypeStruct((rows, n), F32),
        compiler_params=_params(("arbitrary",)),
        name="ada_mod",
    )(cvec, w_ada, b_ada3)


def _rope(v, cos, s1, s2):
    return v * cos + pltpu.roll(v, LANES - 8, axis=1) * s1 + pltpu.roll(v, 8, axis=1) * s2


def _proj_kernel(x_ref, sh_ref, sc_ref, g_ref, win_ref, naqg_ref, nakg_ref, cqg_ref, ckvg_ref,
                 wuq_ref, wuk_ref, wuv_ref, qg_ref, kg_ref, cos_ref, s1_ref, s2_ref,
                 qa_ref, ka_ref, va_ref, qb_ref, kb_ref, vb_ref, ga_ref, gb_ref):
    x = x_ref[...]
    h = _rms(x, D_MODEL) * g_ref[...]
    h = h * (1.0 + sc_ref[...]) + sh_ref[...]
    hb = h.astype(BF16)

    low = _pair_masks(x.shape[0])

    def headnorm(z, g):
        tiles = []
        for c in range(NA_WIDTH // LANES):
            zc = z[:, c * LANES:(c + 1) * LANES]
            sq = zc * zc
            lo = jnp.sum(jnp.where(low, sq, 0.0), axis=-1, keepdims=True)
            hi = jnp.sum(jnp.where(low, 0.0, sq), axis=-1, keepdims=True)
            tiles.append(zc * lax.rsqrt(jnp.where(low, lo, hi) * (1.0 / NA_HEAD_DIM) + RMS_EPS))
        return jnp.concatenate(tiles, axis=1) * g

    zq = _dot(hb, win_ref[:, 0:NA_WIDTH])
    qa_ref[...] = (headnorm(zq, naqg_ref[...]) * (NA_HEAD_DIM ** -0.5 * LOG2E)).astype(BF16)
    zk = _dot(hb, win_ref[:, NA_WIDTH:2 * NA_WIDTH])
    ka_ref[...] = headnorm(zk, nakg_ref[...]).astype(BF16)
    va_ref[...] = _dot(hb, win_ref[:, 2 * NA_WIDTH:C_NA]).astype(BF16)

    cos = cos_ref[...]
    s1 = s1_ref[...]
    s2 = s2_ref[...]

    cq = _rms(_dot(hb, win_ref[:, C_NA:C_CQ]), MLA_Q_LORA) * cqg_ref[...]
    q = _dot(cq.astype(BF16), wuq_ref[...])
    qg = qg_ref[...]
    for hh in range(MLA_HEADS):
        sl = slice(hh * HEAD_PAD, (hh + 1) * HEAD_PAD)
        qn = _rms(q[:, sl], MLA_QK_DIM) * qg
        qb_ref[:, sl] = (_rope(qn, cos, s1, s2) * (MLA_QK_DIM ** -0.5 * LOG2E)).astype(BF16)

    ckv = (_rms(_dot(hb, win_ref[:, C_CQ:C_CKV]), MLA_KV_LORA) * ckvg_ref[...]).astype(BF16)
    kn = _dot(ckv, wuk_ref[...])
    kr = _dot(hb, win_ref[:, C_CKV:C_KR])
    kg = kg_ref[...]
    for hh in range(MLA_HEADS):
        sl = slice(hh * HEAD_PAD, (hh + 1) * HEAD_PAD)
        kh = _rms(kn[:, sl] + kr, MLA_QK_DIM) * kg
        kb_ref[:, sl] = _rope(kh, cos, s1, s2).astype(BF16)
    vb_ref[...] = _dot(ckv, wuv_ref[...]).astype(BF16)

    ga_ref[...] = _sigmoid(_dot(hb, win_ref[:, C_KR:C_KR + D_MODEL])).astype(BF16)
    gb_ref[...] = _sigmoid(_dot(hb, win_ref[:, C_KR + D_MODEL:C_ALL])).astype(BF16)


def _proj(x, mod3, mod_row, gain3, l, lw, tabs):
    bsz, sx, _ = x.shape
    tm = min(TM_PROJ, sx)
    nt = sx // tm
    row = (lambda b: b) if mod_row is None else (lambda b: mod_row)

    def full(shape):
        return pl.BlockSpec(shape, lambda b, i: (0,) * len(shape))

    def tok(width):
        return pl.BlockSpec((None, tm, width), lambda b, i: (b, i, 0))

    def tab():
        return pl.BlockSpec((tm, LANES), lambda b, i: (i, 0))

    in_specs = [
        tok(D_MODEL),
        pl.BlockSpec((None, 1, D_MODEL), lambda b, i: (row(b), 0, 0)),
        pl.BlockSpec((None, 1, D_MODEL), lambda b, i: (row(b), 0, 1)),
        pl.BlockSpec((None, 1, D_MODEL), lambda b, i: (l, 0, 0)),
        full((D_MODEL, C_ALL)),
        full((1, NA_WIDTH)), full((1, NA_WIDTH)),
        full((1, MLA_Q_LORA)), full((1, MLA_KV_LORA)),
        full((MLA_Q_LORA, MLA_QK_PAD)), full((MLA_KV_LORA, MLA_QK_PAD)), full((MLA_KV_LORA, MLA_WIDTH)),
        full((1, HEAD_PAD)), full((1, HEAD_PAD)),
        tab(), tab(), tab(),
    ]
    widths = (NA_WIDTH, NA_WIDTH, NA_WIDTH, MLA_QK_PAD, MLA_QK_PAD, MLA_WIDTH, D_MODEL, D_MODEL)
    return pl.pallas_call(
        _proj_kernel,
        grid=(bsz, nt),
        in_specs=in_specs,
        out_specs=[tok(w) for w in widths],
        out_shape=[jax.ShapeDtypeStruct((bsz, sx, w), BF16) for w in widths],
        compiler_params=_params(("arbitrary", "arbitrary"), VMEM_LIMIT),
        name="mixer_inputs",
    )(x, mod3, mod3, gain3, lw["w_in"], lw["na_q_g"], lw["na_k_g"], lw["cq_g"], lw["ckv_g"],
      lw["w_uq"], lw["w_uk"], lw["w_uv"], lw["q_g"], lw["k_g"], tabs[0], tabs[1], tabs[2])


def _pair_masks(rows):
    lane = lax.broadcasted_iota(jnp.int32, (rows, LANES), 1)
    return lane < NA_HEAD_DIM


def _ones_pad(v, low):
    one = jnp.ones_like(v)
    return jnp.where(low, v, one), jnp.where(low, one, v)


def _pair_finish(o_a, o_b, low):
    num = jnp.where(low, o_a, o_b)
    den = pltpu.roll(jnp.where(low, o_b, o_a), NA_HEAD_DIM, axis=1)
    return (num / den).astype(BF16)


def _na_kernel(q_ref, k_ref, v_ref, kc_ref, vc_ref, bias_ref, o_ref, va_ref, vb_ref, vca_ref, vcb_ref):
    qb = pl.program_id(2)
    n_rows = k_ref.shape[0] // GRID_W
    nwin = NA_WIN_ROWS * GRID_W

    @pl.when(qb == 0)
    def _():
        va_ref[...], vb_ref[...] = _ones_pad(v_ref[...], _pair_masks(v_ref.shape[0]))
        vca_ref[...], vcb_ref[...] = _ones_pad(vc_ref[...], _pair_masks(vc_ref.shape[0]))

    low = _pair_masks(GRID_W)
    kc = kc_ref[...]
    rows = range(NA_QROWS)
    starts, scores = [], []
    for j in rows:
        r = qb * NA_QROWS + j
        rs = jnp.clip(r - NA_WIN_ROWS // 2, 0, n_rows - NA_WIN_ROWS)
        start = pl.multiple_of(rs * GRID_W, GRID_W)
        q = q_ref[j * GRID_W:(j + 1) * GRID_W, :]
        zero = jnp.zeros_like(q)
        qs = jnp.concatenate([jnp.where(low, q, zero), jnp.where(low, zero, q)], axis=0)
        s = _dot_nt(qs, k_ref[pl.ds(start, nwin), :]) + bias_ref[r - rs].reshape(2 * GRID_W, nwin)
        starts.append(start)
        scores.append((s, _dot_nt(qs, kc)))
    probs = []
    for s, sc in scores:
        m = jnp.maximum(jnp.max(s, axis=-1, keepdims=True), jnp.max(sc, axis=-1, keepdims=True))
        probs.append((jnp.exp2(s - m).astype(BF16), jnp.exp2(sc - m).astype(BF16)))
    outs = []
    for start, (p, pc) in zip(starts, probs):
        o_a = _dot(p[:GRID_W], va_ref[pl.ds(start, nwin), :]) + _dot(pc[:GRID_W], vca_ref[...])
        o_b = _dot(p[GRID_W:], vb_ref[pl.ds(start, nwin), :]) + _dot(pc[GRID_W:], vcb_ref[...])
        outs.append(_pair_finish(o_a, o_b, low))
    o_ref[...] = jnp.concatenate(outs, axis=0)


def _na_attention(qa, ka, va, kc, vc, bias):
    bsz, s, _ = qa.shape
    nc = kc.shape[1]
    nq = NA_QROWS * GRID_W
    npairs = NA_WIDTH // LANES
    nwin = NA_WIN_ROWS * GRID_W
    whole = pl.BlockSpec((None, s, LANES), lambda p, b, qb: (b, 0, p))
    cspec = pl.BlockSpec((None, nc, LANES), lambda p, b, qb: (b, 0, p))
    return pl.pallas_call(
        _na_kernel,
        grid=(npairs, bsz, s // nq),
        in_specs=[pl.BlockSpec((None, nq, LANES), lambda p, b, qb: (b, qb, p)),
                  whole, whole, cspec, cspec,
                  pl.BlockSpec((NA_WIN_ROWS, 2, GRID_W, nwin), lambda p, b, qb: (0, p, 0, 0))],
        out_specs=pl.BlockSpec((None, nq, LANES), lambda p, b, qb: (b, qb, p)),
        out_shape=jax.ShapeDtypeStruct((bsz, s, NA_WIDTH), BF16),
        scratch_shapes=[pltpu.VMEM((s, LANES), BF16), pltpu.VMEM((s, LANES), BF16),
                        pltpu.VMEM((nc, LANES), BF16), pltpu.VMEM((nc, LANES), BF16)],
        compiler_params=_params(("arbitrary", "arbitrary", "arbitrary"), VMEM_LIMIT),
        name="na_attention",
    )(qa, ka, va, kc, vc, bias)


def _mla_kernel(q_ref, k_ref, v_ref, kc_ref, vc_ref, o_ref, va_ref, vb_ref, vca_ref, vcb_ref):
    @pl.when(pl.program_id(2) == 0)
    def _():
        va_ref[...], vb_ref[...] = _ones_pad(v_ref[...], _pair_masks(v_ref.shape[0]))
        vca_ref[...], vcb_ref[...] = _ones_pad(vc_ref[...], _pair_masks(vc_ref.shape[0]))

    scores = []
    for hh in range(2):
        sl = slice(hh * HEAD_PAD, (hh + 1) * HEAD_PAD)
        q = q_ref[:, sl]
        scores.append((_dot_nt(q, k_ref[:, sl]), _dot_nt(q, kc_ref[:, sl])))
    probs = []
    for s, sc in scores:
        m = jnp.maximum(jnp.max(s, axis=-1, keepdims=True), jnp.max(sc, axis=-1, keepdims=True))
        probs.append((jnp.exp2(s - m).astype(BF16), jnp.exp2(sc - m).astype(BF16)))
    outs = [_dot(p, vx_ref[...]) + _dot(pc, vcx_ref[...])
            for (p, pc), (vx_ref, vcx_ref) in zip(probs, ((va_ref, vca_ref), (vb_ref, vcb_ref)))]
    o_ref[...] = _pair_finish(outs[0], outs[1], _pair_masks(q_ref.shape[0]))


def _mla_attention(qb, kb, vb, kcb, vcb):
    bsz, s, _ = qb.shape
    nc = kcb.shape[1]
    tq = TQ_MLA
    npairs = MLA_WIDTH // LANES
    return pl.pallas_call(
        _mla_kernel,
        grid=(bsz, npairs, s // tq),
        in_specs=[pl.BlockSpec((None, tq, 2 * HEAD_PAD), lambda b, p, i: (b, i, p)),
                  pl.BlockSpec((None, s, 2 * HEAD_PAD), lambda b, p, i: (b, 0, p)),
                  pl.BlockSpec((None, s, LANES), lambda b, p, i: (b, 0, p)),
                  pl.BlockSpec((None, nc, 2 * HEAD_PAD), lambda b, p, i: (b, 0, p)),
                  pl.BlockSpec((None, nc, LANES), lambda b, p, i: (b, 0, p))],
        out_specs=pl.BlockSpec((None, tq, LANES), lambda b, p, i: (b, i, p)),
        out_shape=jax.ShapeDtypeStruct((bsz, s, MLA_WIDTH), BF16),
        scratch_shapes=[pltpu.VMEM((s, LANES), BF16), pltpu.VMEM((s, LANES), BF16),
                        pltpu.VMEM((nc, LANES), BF16), pltpu.VMEM((nc, LANES), BF16)],
        compiler_params=_params(("arbitrary", "arbitrary", "arbitrary"), VMEM_LIMIT),
        name="mla_attention",
    )(qb, kb, vb, kcb, vcb)


def _softmax2_pv(s, v):
    p = jnp.exp2(s - jnp.max(s, axis=-1, keepdims=True))
    return _dot(p.astype(BF16), v) / jnp.sum(p, axis=-1, keepdims=True)


def _ctx_kernel(qa_ref, ka_ref, va_ref, qb_ref, kb_ref, vb_ref, oa_ref, ob_ref):
    n = qa_ref.shape[0]
    low = _pair_masks(n)
    qa = qa_ref[...]
    zero = jnp.zeros_like(qa)
    ka = ka_ref[...]
    va = va_ref[...]
    o_lo = _softmax2_pv(_dot_nt(jnp.where(low, qa, zero), ka), va)
    o_hi = _softmax2_pv(_dot_nt(jnp.where(low, zero, qa), ka), va)
    oa_ref[...] = jnp.where(low, o_lo, o_hi).astype(BF16)
    vb = vb_ref[...]
    outs = []
    for hh in range(2):
        sl = slice(hh * HEAD_PAD, (hh + 1) * HEAD_PAD)
        outs.append(_softmax2_pv(_dot_nt(qb_ref[:, sl], kb_ref[:, sl]), vb))
    ob_ref[...] = jnp.where(low, outs[0], outs[1]).astype(BF16)


def _ctx_attention(qa, ka, va, qb, kb, vb):
    bsz, n, _ = qa.shape
    npairs = NA_WIDTH // LANES
    narrow = pl.BlockSpec((None, n, LANES), lambda b, p: (b, 0, p))
    wide = pl.BlockSpec((None, n, 2 * HEAD_PAD), lambda b, p: (b, 0, p))
    return pl.pallas_call(
        _ctx_kernel,
        grid=(bsz, npairs),
        in_specs=[narrow, narrow, narrow, wide, wide, narrow],
        out_specs=[narrow, narrow],
        out_shape=[jax.ShapeDtypeStruct((bsz, n, NA_WIDTH), BF16),
                   jax.ShapeDtypeStruct((bsz, n, MLA_WIDTH), BF16)],
        compiler_params=_params(("arbitrary", "arbitrary")),
        name="ctx_attention",
    )(qa, ka, va, qb, kb, vb)


def _merge_kernel(x_ref, oa_ref, ob_ref, ga_ref, gb_ref, woa_ref, wob_ref, wout_ref, gm_ref,
                  g_ref, sh_ref, sc_ref, wrh_ref, wrl_ref, xo_ref, h_ref, lg_ref):
    y = (ga_ref[...].astype(F32) * _dot(oa_ref[...], woa_ref[...])
         + gb_ref[...].astype(F32) * _dot(ob_ref[...], wob_ref[...]))
    xn = x_ref[...] + gm_ref[...] * _dot(y.astype(BF16), wout_ref[...])
    xo_ref[...] = xn
    h = _rms(xn, D_MODEL) * g_ref[...]
    h = h * (1.0 + sc_ref[...]) + sh_ref[...]
    _store_slabs(h_ref, _pack_rows(h))
    h_hi, h_lo = _split_bf16(h)
    wrh = wrh_ref[...]
    lg_ref[...] = _dot_nt(wrh, h_hi) + _dot_nt(wrh, h_lo) + _dot_nt(wrl_ref[...], h_hi)


def _merge(x, oa, ob, ga, gb, mod3, mod_row, gain3, l, lw):
    bsz, sx, _ = x.shape
    tm = min(TM_MERGE, sx)
    nt = sx // tm
    row = (lambda b: b) if mod_row is None else (lambda b: mod_row)

    def full(shape):
        return pl.BlockSpec(shape, lambda b, i: (0,) * len(shape))

    def tok(width):
        return pl.BlockSpec((None, tm, width), lambda b, i: (b, i, 0))

    def mod(chunk):
        return pl.BlockSpec((None, 1, D_MODEL), lambda b, i: (row(b), 0, chunk))

    return pl.pallas_call(
        _merge_kernel,
        grid=(bsz, nt),
        in_specs=[tok(D_MODEL), tok(NA_WIDTH), tok(MLA_WIDTH), tok(D_MODEL), tok(D_MODEL),
                  full((NA_WIDTH, D_MODEL)), full((MLA_WIDTH, D_MODEL)), full((D_MODEL, D_MODEL)),
                  mod(2),
                  pl.BlockSpec((None, 1, D_MODEL), lambda b, i: (l, 0, 0)),
                  mod(3), mod(4),
                  full((N_EXPERTS, D_MODEL)), full((N_EXPERTS, D_MODEL))],
        out_specs=[tok(D_MODEL),
                   pl.BlockSpec((tm * PACK_SUB, LANES), lambda b, i: (b * nt + i, 0)),
                   pl.BlockSpec((N_EXPERTS, tm), lambda b, i: (0, b * nt + i))],
        out_shape=[jax.ShapeDtypeStruct((bsz, sx, D_MODEL), F32),
                   jax.ShapeDtypeStruct((bsz * sx * PACK_SUB, LANES), jnp.uint32),
                   jax.ShapeDtypeStruct((N_EXPERTS, bsz * sx), F32)],
        compiler_params=_params(("arbitrary", "arbitrary"), VMEM_LIMIT),
        name="merge_branches",
    )(x, oa, ob, ga, gb, lw["w_oa"], lw["w_ob"], lw["w_out"], mod3, gain3, mod3, mod3,
      lw["wr_hi"], lw["wr_lo"])


def _route_kernel(lg_ref, bias_ref, tri_ref, eid_ref, gw_ref, rank_ref, cnt_ref, run_ref):
    @pl.when(pl.program_id(0) == 0)
    def _():
        run_ref[...] = jnp.zeros_like(run_ref)

    tr = lg_ref.shape[1]
    s = _sigmoid(lg_ref[...])
    biased = s + bias_ref[...]
    sub = lax.broadcasted_iota(jnp.int32, (EXPERTS_PER_GROUP, tr), 0).astype(F32)
    none = float(EXPERTS_PER_GROUP)
    best = jnp.full((1, tr), -jnp.inf, F32)
    e1 = jnp.zeros((1, tr), F32)
    e2 = jnp.zeros((1, tr), F32)
    for g in range(N_GROUPS):
        bg = biased[g * EXPERTS_PER_GROUP:(g + 1) * EXPERTS_PER_GROUP]
        m1 = jnp.max(bg, axis=0, keepdims=True)
        i1 = jnp.min(jnp.where(bg == m1, sub, none), axis=0, keepdims=True)
        rest = jnp.where(sub == i1, -jnp.inf, bg)
        m2 = jnp.max(rest, axis=0, keepdims=True)
        i2 = jnp.min(jnp.where(rest == m2, sub, none), axis=0, keepdims=True)
        score = m1 + m2
        better = score > best
        best = jnp.where(better, score, best)
        e1 = jnp.where(better, g * EXPERTS_PER_GROUP + i1, e1)
        e2 = jnp.where(better, g * EXPERTS_PER_GROUP + i2, e2)

    rowid = lax.broadcasted_iota(jnp.int32, (N_EXPERTS, tr), 0).astype(F32)
    is1 = rowid == e1
    is2 = rowid == e2
    w1 = jnp.sum(jnp.where(is1, s, 0.0), axis=0, keepdims=True)
    w2 = jnp.sum(jnp.where(is2, s, 0.0), axis=0, keepdims=True)
    tot = w1 + w2
    gw_ref[0:1, :] = w1 / tot
    gw_ref[1:2, :] = w2 / tot
    eid_ref[0:1, :] = e1.astype(jnp.int32)
    eid_ref[1:2, :] = e2.astype(jnp.int32)

    onehot = jnp.where(is1 | is2, 1.0, 0.0)
    before = _dot(onehot.astype(BF16), tri_ref[...]) + run_ref[...]
    rank_ref[0:1, :] = jnp.sum(jnp.where(is1, before, 0.0), axis=0, keepdims=True).astype(jnp.int32)
    rank_ref[1:2, :] = jnp.sum(jnp.where(is2, before, 0.0), axis=0, keepdims=True).astype(jnp.int32)
    run_ref[...] = run_ref[...] + jnp.sum(onehot, axis=1, keepdims=True)
    cnt_ref[...] = run_ref[...]


def _route(logits, bias_col, tri):
    t = logits.shape[1]
    tr = TR_ROUTE
    pair = pl.BlockSpec((TOP_K, tr), lambda i: (0, i))
    return pl.pallas_call(
        _route_kernel,
        grid=(t // tr,),
        in_specs=[pl.BlockSpec((N_EXPERTS, tr), lambda i: (0, i)),
                  pl.BlockSpec((N_EXPERTS, 1), lambda i: (0, 0)),
                  pl.BlockSpec((tr, tr), lambda i: (0, 0))],
        out_specs=[pair, pair, pair, pl.BlockSpec((N_EXPERTS, 1), lambda i: (0, 0))],
        out_shape=[jax.ShapeDtypeStruct((TOP_K, t), jnp.int32),
                   jax.ShapeDtypeStruct((TOP_K, t), F32),
                   jax.ShapeDtypeStruct((TOP_K, t), jnp.int32),
                   jax.ShapeDtypeStruct((N_EXPERTS, 1), F32)],
        scratch_shapes=[pltpu.VMEM((N_EXPERTS, 1), F32)],
        compiler_params=_params(("arbitrary",)),
        name="route",
    )(logits, bias_col, tri)


def _dispatch_kernel(dest_ref, h_ref, slots_in_ref, slots_ref, sem, *, n_tok):
    del slots_in_ref
    tm = h_ref.shape[0] // PACK_SUB
    base = pl.program_id(0) * tm

    def copy(r, k):
        d = dest_ref[k * n_tok + base + r]
        return pltpu.make_async_copy(h_ref.at[pl.ds(pl.multiple_of(r * PACK_SUB, PACK_SUB), PACK_SUB)],
                                     slots_ref.at[pl.ds(pl.multiple_of(d * PACK_SUB, PACK_SUB), PACK_SUB)], sem)

    def issue(r, carry):
        for k in range(TOP_K):
            copy(r, k).start(priority=k)
        return carry

    def drain(r, carry):
        for k in range(TOP_K):
            copy(r, k).wait()
        return carry

    lax.fori_loop(0, tm, issue, 0, unroll=MOVE_UNROLL)
    lax.fori_loop(0, tm, drain, 0, unroll=MOVE_UNROLL)


def _dispatch(h, dest_flat, slots):
    n_tok = h.shape[0] // PACK_SUB
    tm = TM_MOVE
    return pl.pallas_call(
        functools.partial(_dispatch_kernel, n_tok=n_tok),
        grid_spec=pltpu.PrefetchScalarGridSpec(
            num_scalar_prefetch=1,
            grid=(n_tok // tm,),
            in_specs=[pl.BlockSpec((tm * PACK_SUB, LANES), lambda i, dest: (i, 0)),
                      pl.BlockSpec(memory_space=pl.ANY)],
            out_specs=pl.BlockSpec(memory_space=pl.ANY),
            scratch_shapes=[pltpu.SemaphoreType.DMA],
        ),
        out_shape=jax.ShapeDtypeStruct(slots.shape, slots.dtype),
        input_output_aliases={2: 0},
        compiler_params=_params(("arbitrary",)),
        name="moe_dispatch",
    )(dest_flat, h, slots)


def _expert_kernel(be_ref, nused_ref, nxt_ref, x_ref, w1_hbm, w3_hbm, w2_hbm, y_ref,
                   st1, st3, st2, w1b, w3b, w2b, sem, *, layer):
    i = pl.program_id(0)
    used = i < nused_ref[0]
    stages = ((w1_hbm, st1, w1b), (w3_hbm, st3, w3b), (w2_hbm, st2, w2b))

    def fetch(e):
        return [pltpu.make_async_copy(w_hbm.at[layer, e], st, sem.at[j]) for j, (w_hbm, st, _) in enumerate(stages)]

    @pl.when(used)
    def _():
        e = be_ref[i]
        prev = be_ref[jnp.maximum(i - 1, 0)]

        @pl.when(i == 0)
        def _():
            for c in fetch(e):
                c.start()

        @pl.when((i == 0) | (e != prev))
        def _():
            for c, (_, st, wb) in zip(fetch(e), stages):
                c.wait()
                wb[...] = st[...].astype(BF16)
            nxt = nxt_ref[e]

            @pl.when(nxt >= 0)
            def _():
                for c in fetch(nxt):
                    c.start()

        bm = x_ref.shape[0] // PACK_SUB
        x_lo, x_hi = _unpack_rows(_load_slabs(x_ref, bm))
        xb = jnp.concatenate([x_lo.astype(BF16), x_hi.astype(BF16)], axis=1)
        a = _dot(xb, w1b[...])
        b = _dot(xb, w3b[...])
        _store_slabs(y_ref, _pack_rows(_dot((a * _sigmoid(a) * b).astype(BF16), w2b[...])))

    @pl.when(jnp.logical_not(used))
    def _():
        y_ref[...] = jnp.zeros_like(y_ref)


def _experts(slots, block_expert, nused, next_expert, w1, w3, w2, l):
    n_slots = slots.shape[0] // PACK_SUB
    bm = BM_EXP
    de = w1.shape[-1]
    rows = pl.BlockSpec((bm * PACK_SUB, LANES), lambda i, be, nu, nx: (i, 0))
    hbm = pl.BlockSpec(memory_space=pl.ANY)
    return pl.pallas_call(
        functools.partial(_expert_kernel, layer=l),
        grid_spec=pltpu.PrefetchScalarGridSpec(
            num_scalar_prefetch=3,
            grid=(n_slots // bm,),
            in_specs=[rows, hbm, hbm, hbm],
            out_specs=rows,
            scratch_shapes=[pltpu.VMEM((D_MODEL, de), F32), pltpu.VMEM((D_MODEL, de), F32),
                            pltpu.VMEM((de, D_MODEL), F32),
                            pltpu.VMEM((D_MODEL, de), BF16), pltpu.VMEM((D_MODEL, de), BF16),
                            pltpu.VMEM((de, D_MODEL), BF16),
                            pltpu.SemaphoreType.DMA((3,))],
        ),
        out_shape=jax.ShapeDtypeStruct(slots.shape, slots.dtype),
        compiler_params=_params(("arbitrary",), VMEM_LIMIT),
        name="moe_experts",
    )(block_expert, nused, next_expert, slots, w1, w3, w2)


def _combine_kernel(dest_ref, x_ref, gw_ref, gf_ref, y_hbm, o_ref, ybuf, sem, *, n_tok):
    tm = x_ref.shape[0]
    i = pl.program_id(0)
    nsteps = pl.num_programs(0)

    def copy(step, r, k):
        slot = step % 2
        d = dest_ref[k * n_tok + step * tm + r]
        return pltpu.make_async_copy(y_hbm.at[pl.ds(pl.multiple_of(d * PACK_SUB, PACK_SUB), PACK_SUB)],
                                     ybuf.at[slot, k, pl.ds(pl.multiple_of(r * PACK_SUB, PACK_SUB), PACK_SUB)],
                                     sem.at[slot])

    def issue(step):
        def body(r, carry):
            for k in range(TOP_K):
                copy(step, r, k).start(priority=k)
            return carry
        lax.fori_loop(0, tm, body, 0, unroll=MOVE_UNROLL)

    @pl.when(i == 0)
    def _():
        issue(i)

    @pl.when(i + 1 < nsteps)
    def _():
        issue(i + 1)

    def drain(r, carry):
        for k in range(TOP_K):
            copy(i, r, k).wait()
        return carry
    lax.fori_loop(0, tm, drain, 0, unroll=MOVE_UNROLL)

    slot = i % 2
    gw = gw_ref[...]
    halves = [_unpack_rows(_load_slabs(ybuf.at[slot, k], tm)) for k in range(TOP_K)]
    gf = gf_ref[...]
    for part in range(2):
        cols = slice(part * PACK_W, (part + 1) * PACK_W)
        y = gw[:, 0:1] * halves[0][part] + gw[:, 1:2] * halves[1][part]
        o_ref[:, cols] = x_ref[:, cols] + gf[:, cols] * y


def _combine(x, y_slots, dest_flat, gw_t, mod3, mod_row):
    bsz, sx, _ = x.shape
    n_tok = bsz * sx
    tm = TM_MOVE
    nt = sx // tm
    row = (lambda i: i // nt) if mod_row is None else (lambda i: mod_row)
    out = pl.pallas_call(
        functools.partial(_combine_kernel, n_tok=n_tok),
        grid_spec=pltpu.PrefetchScalarGridSpec(
            num_scalar_prefetch=1,
            grid=(n_tok // tm,),
            in_specs=[pl.BlockSpec((tm, D_MODEL), lambda i, dest: (i, 0)),
                      pl.BlockSpec((tm, TOP_K), lambda i, dest: (i, 0)),
                      pl.BlockSpec((None, 1, D_MODEL), lambda i, dest: (row(i), 0, 5)),
                      pl.BlockSpec(memory_space=pl.ANY)],
            out_specs=pl.BlockSpec((tm, D_MODEL), lambda i, dest: (i, 0)),
            scratch_shapes=[pltpu.VMEM((2, TOP_K, tm * PACK_SUB, LANES), jnp.uint32),
                            pltpu.SemaphoreType.DMA((2,))],
        ),
        out_shape=jax.ShapeDtypeStruct((n_tok, D_MODEL), F32),
        compiler_params=_params(("arbitrary",)),
        name="moe_combine",
    )(dest_flat, x.reshape(n_tok, D_MODEL), gw_t, mod3, y_slots)
    return out.reshape(bsz, sx, D_MODEL)


def _layer_weights(l, w_in, na_q_g, na_k_g, mla_cq_g, w_uq, mla_ckv_g, w_ukv, mla_q_g, mla_k_g,
                   w_oa, w_ob, w_out, w_router):
    wi = w_in[l]
    c0 = C_NA + MLA_Q_LORA + MLA_KV_LORA
    kr_cols = jnp.zeros((D_MODEL, HEAD_PAD), F32).at[:, MLA_NOPE:MLA_QK_DIM].set(wi[:, c0:c0 + MLA_ROPE])
    w_in_arr = jnp.concatenate([wi[:, :c0], kr_cols, wi[:, c0 + MLA_ROPE:]], axis=1).astype(BF16)

    def pad_heads(w, width):
        w = w.reshape(w.shape[0], MLA_HEADS, width)
        return jnp.pad(w, ((0, 0), (0, 0), (0, HEAD_PAD - width))).reshape(w.shape[0], MLA_QK_PAD)

    ukv = w_ukv[l].reshape(MLA_KV_LORA, MLA_HEADS, MLA_NOPE + MLA_V_DIM)
    wr_t = w_router.T
    wr_hi = wr_t.astype(BF16)
    return {
        "w_in": w_in_arr,
        "na_q_g": jnp.tile(na_q_g[l], NA_HEADS)[None, :],
        "na_k_g": jnp.tile(na_k_g[l], NA_HEADS)[None, :],
        "cq_g": mla_cq_g[l][None, :],
        "ckv_g": mla_ckv_g[l][None, :],
        "w_uq": pad_heads(w_uq[l], MLA_QK_DIM).astype(BF16),
        "w_uk": pad_heads(ukv[:, :, :MLA_NOPE].reshape(MLA_KV_LORA, -1), MLA_NOPE).astype(BF16),
        "w_uv": ukv[:, :, MLA_NOPE:].reshape(MLA_KV_LORA, MLA_WIDTH).astype(BF16),
        "q_g": jnp.pad(mla_q_g[l], (0, HEAD_PAD - MLA_QK_DIM))[None, :],
        "k_g": jnp.pad(mla_k_g[l], (0, HEAD_PAD - MLA_QK_DIM))[None, :],
        "w_oa": w_oa[l].astype(BF16),
        "w_ob": w_ob[l].astype(BF16),
        "w_out": w_out[l].astype(BF16),
        "wr_hi": wr_hi,
        "wr_lo": (wr_t - wr_hi.astype(F32)).astype(BF16),
    }


def _rope_tables(s):
    half = MLA_ROPE // 4
    pos = jnp.arange(s, dtype=jnp.int32)
    inv = ROPE_BASE ** (-jnp.arange(half, dtype=F32) / half)
    ang_r = (pos // GRID_W).astype(F32)[:, None] * inv[None, :]
    ang_c = (pos % GRID_W).astype(F32)[:, None] * inv[None, :]
    zeros = jnp.zeros((s, half), F32)
    lead = jnp.zeros((s, MLA_NOPE), F32)
    tail = jnp.zeros((s, HEAD_PAD - MLA_QK_DIM), F32)
    cos = jnp.concatenate([lead + 1.0, jnp.cos(ang_r), jnp.cos(ang_r), jnp.cos(ang_c), jnp.cos(ang_c), tail + 1.0], 1)
    s1 = jnp.concatenate([lead, -jnp.sin(ang_r), zeros, -jnp.sin(ang_c), zeros, tail], 1)
    s2 = jnp.concatenate([lead, zeros, jnp.sin(ang_r), zeros, jnp.sin(ang_c), tail], 1)
    return cos, s1, s2


def _na_bias_tables(rpb):
    d = jnp.arange(NA_WIN_ROWS)[:, None]
    i = jnp.arange(NA_WIN_ROWS)[None, :]
    qc = jnp.arange(GRID_W)[:, None]
    kc = jnp.arange(GRID_W)[None, :]
    cs = jnp.clip(qc - NA_WIN_COLS // 2, 0, GRID_W - NA_WIN_COLS)
    cvalid = (kc >= cs) & (kc < cs + NA_WIN_COLS)
    ohr = jax.nn.one_hot(i - d + NA_WIN_ROWS - 1, 2 * NA_WIN_ROWS - 1, dtype=F32)
    ohc = jax.nn.one_hot(jnp.clip(kc - qc + NA_WIN_COLS - 1, 0, 2 * NA_WIN_COLS - 2), 2 * NA_WIN_COLS - 1, dtype=F32)
    t = jnp.einsum("dia,hab,qkb->dhqik", ohr, rpb.astype(F32), ohc, precision=lax.Precision.HIGHEST)
    t = jnp.where(cvalid[None, None, :, None, :], t * LOG2E, NEG_INF)
    return t.reshape(NA_WIN_ROWS, rpb.shape[0], GRID_W, NA_WIN_ROWS * GRID_W)


def _slot_tables(eid, rank, counts):
    bm = BM_EXP
    counts = counts[:, 0].astype(jnp.int32)
    padded = ((counts + bm - 1) // bm) * bm
    pad_end = jnp.cumsum(padded)
    pad_start = pad_end - padded
    experts = jnp.arange(N_EXPERTS, dtype=jnp.int32)
    dest = rank + jnp.sum(jnp.where(eid[..., None] == experts, pad_start, 0), axis=-1)
    m = eid.shape[1] * TOP_K
    n_blocks = -(-m // bm) + N_EXPERTS
    blk = jnp.arange(n_blocks, dtype=jnp.int32) * bm
    block_expert = jnp.minimum(jnp.sum(pad_end[None, :] <= blk[:, None], axis=1), N_EXPERTS - 1).astype(jnp.int32)
    nused = (pad_end[-1:] // bm).astype(jnp.int32)
    later = lax.cummin(jnp.where(counts > 0, experts, N_EXPERTS), axis=0, reverse=True)
    next_expert = jnp.concatenate([later[1:], jnp.full((1,), N_EXPERTS, jnp.int32)])
    next_expert = jnp.where(next_expert < N_EXPERTS, next_expert, -1).astype(jnp.int32)
    return dest.astype(jnp.int32), block_expert, nused, next_expert, n_blocks * bm


def kernel(x, c, ctx, c_ctx, w_ada, b_ada, norm_mix_g, norm_ffn_g, w_in, na_q_g, na_k_g, na_rpb,
           mla_cq_g, w_uq, mla_ckv_g, w_ukv, mla_q_g, mla_k_g, w_oa, w_ob, w_out,
           w_router, router_bias, w1, w3, w2):
    bsz, s, d = x.shape
    n_ctx = ctx.shape[1]
    ctx_row = bsz
    pad_rows = -(bsz + 1) % 8
    cvec = jnp.concatenate([c, c_ctx[None, :], jnp.zeros((pad_rows, d), F32)], axis=0)
    b_ada3 = b_ada[:, None, :]
    mix_g3 = norm_mix_g[:, None, :]
    ffn_g3 = norm_ffn_g[:, None, :]
    tabs_x = _rope_tables(s)
    tabs_c = (jnp.ones((n_ctx, LANES), F32), jnp.zeros((n_ctx, LANES), F32), jnp.zeros((n_ctx, LANES), F32))
    tri = (jnp.arange(TR_ROUTE)[:, None] < jnp.arange(TR_ROUTE)[None, :]).astype(BF16)
    bias_col = router_bias.astype(F32)[:, None]
    n_x = bsz * s

    xc = ctx
    for l in range(DEPTH):
        last = l == DEPTH - 1
        lw = _layer_weights(l, w_in, na_q_g, na_k_g, mla_cq_g, w_uq, mla_ckv_g, w_ukv, mla_q_g, mla_k_g,
                            w_oa, w_ob, w_out, w_router)
        mod3 = _ada(cvec, w_ada, b_ada3, l)[:, None, :]
        qa, ka, va, qb, kb, vb, ga, gb = _proj(x, mod3, None, mix_g3, l, lw, tabs_x)
        qa_c, ka_c, va_c, qb_c, kb_c, vb_c, ga_c, gb_c = _proj(xc, mod3, ctx_row, mix_g3, l, lw, tabs_c)
        oa = _na_attention(qa, ka, va, ka_c, va_c, _na_bias_tables(na_rpb[l]))
        ob = _mla_attention(qb, kb, vb, kb_c, vb_c)
        x, h_x, lg = _merge(x, oa, ob, ga, gb, mod3, None, ffn_g3, l, lw)
        if not last:
            oa_c, ob_c = _ctx_attention(qa_c, ka_c, va_c, qb_c, kb_c, vb_c)
            xc, h_c, lg_c = _merge(xc, oa_c, ob_c, ga_c, gb_c, mod3, ctx_row, ffn_g3, l, lw)
            lg = jnp.concatenate([lg, lg_c], axis=1)

        eid, gw, rank, counts = _route(lg, bias_col, tri)
        dest, block_expert, nused, next_expert, n_slots = _slot_tables(eid, rank, counts)
        gw_t = gw.T
        slots = jnp.zeros((n_slots * PACK_SUB, LANES), jnp.uint32)
        slots = _dispatch(h_x, dest[:, :n_x].reshape(-1), slots)
        if not last:
            slots = _dispatch(h_c, dest[:, n_x:].reshape(-1), slots)
        y_slots = _experts(slots, block_expert, nused, next_expert, w1, w3, w2, l)
        x = _combine(x, y_slots, dest[:, :n_x].reshape(-1), gw_t[:n_x], mod3, None)
        if not last:
            xc = _combine(xc, y_slots, dest[:, n_x:].reshape(-1), gw_t[n_x:], mod3, ctx_row)
    return x
```

```python
import functools

import jax
import jax.numpy as jnp
from jax import lax
from jax.experimental import pallas as pl
from jax.experimental.pallas import tpu as pltpu

F32 = jnp.float32
BF16 = jnp.bfloat16

D_MODEL = 1024
DEPTH = 2
GRID_W = 64
N_MOD = 6

NA_HEADS = 8
NA_HEAD_DIM = 64
NA_WIN_ROWS = 8
NA_WIN_COLS = 16
NA_WIDTH = NA_HEADS * NA_HEAD_DIM

MLA_HEADS = 8
MLA_NOPE = 64
MLA_ROPE = 32
MLA_QK_DIM = MLA_NOPE + MLA_ROPE
MLA_V_DIM = 64
MLA_Q_LORA = 384
MLA_KV_LORA = 256
MLA_WIDTH = MLA_HEADS * MLA_V_DIM
ROPE_BASE = 10000.0

N_EXPERTS = 64
EXPERTS_PER_GROUP = 8
N_GROUPS = N_EXPERTS // EXPERTS_PER_GROUP
TOP_K = 2

RMS_EPS = 1e-6
NEG_INF = -1e30
LOG2E = 1.4426950408889634

LANES = 128
HEAD_PAD = LANES
MLA_QK_PAD = MLA_HEADS * HEAD_PAD

C_NA = 3 * NA_WIDTH
C_CQ = C_NA + MLA_Q_LORA
C_CKV = C_CQ + MLA_KV_LORA
C_KR = C_CKV + HEAD_PAD
C_ALL = C_KR + 2 * D_MODEL

TM_PROJ = 256
TM_MERGE = 512
TQ_MLA = 512
NA_QROWS = 8
TR_ROUTE = 512
BM_EXP = 256
PACK_W = D_MODEL // 2
PACK_SUB = PACK_W // LANES
TM_MOVE = 256
MOVE_UNROLL = 8
EXP_CHUNKS = 4
VMEM_LIMIT = 56 * 1024 * 1024


def _sigmoid(v):
    return 1.0 / (1.0 + jnp.exp(-v))


def _rms(v, n):
    return v * lax.rsqrt(jnp.sum(v * v, axis=-1, keepdims=True) * (1.0 / n) + RMS_EPS)


def _pack_rows(v):
    lo = pltpu.bitcast(v[:, :PACK_W].astype(BF16).astype(F32), jnp.uint32)
    hi = pltpu.bitcast(v[:, PACK_W:].astype(BF16).astype(F32), jnp.uint32)
    return (lo >> 16) | (hi & jnp.uint32(0xFFFF0000))


def _unpack_rows(w):
    return pltpu.bitcast(w << 16, F32), pltpu.bitcast(w & jnp.uint32(0xFFFF0000), F32)


def _store_slabs(ref, w):
    for c in range(PACK_SUB):
        ref[pl.ds(c, w.shape[0], stride=PACK_SUB), :] = w[:, c * LANES:(c + 1) * LANES]


def _load_slabs(ref, m):
    return jnp.concatenate([ref[pl.ds(c, m, stride=PACK_SUB), :] for c in range(PACK_SUB)], axis=1)


def _split_bf16(a):
    hi = a.astype(BF16)
    lo = (a - hi.astype(F32)).astype(BF16)
    return hi, lo


def _dot(a, b):
    return jnp.dot(a, b, preferred_element_type=F32)


def _dot_nt(a, b):
    return lax.dot_general(a, b, (((1,), (1,)), ((), ())), preferred_element_type=F32)


def _params(sem, vmem=None):
    return pltpu.CompilerParams(dimension_semantics=sem, vmem_limit_bytes=vmem)


def _ada_kernel(c_ref, w_ref, b_ref, o_ref):
    cv = c_ref[...]
    s = cv * _sigmoid(cv)
    s_hi, s_lo = _split_bf16(s)
    w_hi, w_lo = _split_bf16(w_ref[...])
    o_ref[...] = _dot(s_hi, w_hi) + _dot(s_lo, w_hi) + _dot(s_hi, w_lo) + b_ref[...]


def _ada(cvec, w_ada, b_ada3, l):
    rows = cvec.shape[0]
    n = N_MOD * D_MODEL
    tn = 512
    return pl.pallas_call(
        _ada_kernel,
        grid=(n // tn,),
        in_specs=[
            pl.BlockSpec((rows, D_MODEL), lambda j: (0, 0)),
            pl.BlockSpec((None, D_MODEL, tn), lambda j: (l, 0, j)),
            pl.BlockSpec((None, 1, tn), lambda j: (l, 0, j)),
        ],
        out_specs=pl.BlockSpec((rows, tn), lambda j: (0, j)),
        out_shape=jax.ShapeDtypeStruct((rows, n), F32),
        compiler_params=_params(("arbitrary",)),
        name="ada_mod",
    )(cvec, w_ada, b_ada3)


def _rope(v, cos, s1, s2):
    return v * cos + pltpu.roll(v, LANES - 8, axis=1) * s1 + pltpu.roll(v, 8, axis=1) * s2


def _proj_kernel(x_ref, sh_ref, sc_ref, g_ref, win_ref, naqg_ref, nakg_ref, cqg_ref, ckvg_ref,
                 wuq_ref, wuk_ref, wuv_ref, qg_ref, kg_ref, cos_ref, s1_ref, s2_ref,
                 qa_ref, ka_ref, va_ref, qb_ref, kb_ref, vb_ref, ga_ref, gb_ref):
    x = x_ref[...]
    h = _rms(x, D_MODEL) * g_ref[...]
    h = h * (1.0 + sc_ref[...]) + sh_ref[...]
    hb = h.astype(BF16)

    low = _pair_masks(x.shape[0])

    def headnorm(z, g):
        tiles = []
        for c in range(NA_WIDTH // LANES):
            zc = z[:, c * LANES:(c + 1) * LANES]
            sq = zc * zc
            lo = jnp.sum(jnp.where(low, sq, 0.0), axis=-1, keepdims=True)
            hi = jnp.sum(jnp.where(low, 0.0, sq), axis=-1, keepdims=True)
            tiles.append(zc * lax.rsqrt(jnp.where(low, lo, hi) * (1.0 / NA_HEAD_DIM) + RMS_EPS))
        return jnp.concatenate(tiles, axis=1) * g

    zq = _dot(hb, win_ref[:, 0:NA_WIDTH])
    qa_ref[...] = (headnorm(zq, naqg_ref[...]) * (NA_HEAD_DIM ** -0.5 * LOG2E)).astype(BF16)
    zk = _dot(hb, win_ref[:, NA_WIDTH:2 * NA_WIDTH])
    ka_ref[...] = headnorm(zk, nakg_ref[...]).astype(BF16)
    va_ref[...] = _dot(hb, win_ref[:, 2 * NA_WIDTH:C_NA]).astype(BF16)

    cos = cos_ref[...]
    s1 = s1_ref[...]
    s2 = s2_ref[...]

    cq = _rms(_dot(hb, win_ref[:, C_NA:C_CQ]), MLA_Q_LORA) * cqg_ref[...]
    q = _dot(cq.astype(BF16), wuq_ref[...])
    qg = qg_ref[...]
    for hh in range(MLA_HEADS):
        sl = slice(hh * HEAD_PAD, (hh + 1) * HEAD_PAD)
        qn = _rms(q[:, sl], MLA_QK_DIM) * qg
        qb_ref[:, sl] = (_rope(qn, cos, s1, s2) * (MLA_QK_DIM ** -0.5 * LOG2E)).astype(BF16)

    ckv = (_rms(_dot(hb, win_ref[:, C_CQ:C_CKV]), MLA_KV_LORA) * ckvg_ref[...]).astype(BF16)
    kn = _dot(ckv, wuk_ref[...])
    kr = _dot(hb, win_ref[:, C_CKV:C_KR])
    kg = kg_ref[...]
    for hh in range(MLA_HEADS):
        sl = slice(hh * HEAD_PAD, (hh + 1) * HEAD_PAD)
        kh = _rms(kn[:, sl] + kr, MLA_QK_DIM) * kg
        kb_ref[:, sl] = _rope(kh, cos, s1, s2).astype(BF16)
    vb_ref[...] = _dot(ckv, wuv_ref[...]).astype(BF16)

    ga_ref[...] = _sigmoid(_dot(hb, win_ref[:, C_KR:C_KR + D_MODEL])).astype(BF16)
    gb_ref[...] = _sigmoid(_dot(hb, win_ref[:, C_KR + D_MODEL:C_ALL])).astype(BF16)


def _proj(x, mod3, mod_row, gain3, l, lw, tabs):
    bsz, sx, _ = x.shape
    tm = min(TM_PROJ, sx)
    nt = sx // tm
    row = (lambda b: b) if mod_row is None else (lambda b: mod_row)

    def full(shape):
        return pl.BlockSpec(shape, lambda b, i: (0,) * len(shape))

    def tok(width):
        return pl.BlockSpec((None, tm, width), lambda b, i: (b, i, 0))

    def tab():
        return pl.BlockSpec((tm, LANES), lambda b, i: (i, 0))

    in_specs = [
        tok(D_MODEL),
        pl.BlockSpec((None, 1, D_MODEL), lambda b, i: (row(b), 0, 0)),
        pl.BlockSpec((None, 1, D_MODEL), lambda b, i: (row(b), 0, 1)),
        pl.BlockSpec((None, 1, D_MODEL), lambda b, i: (l, 0, 0)),
        full((D_MODEL, C_ALL)),
        full((1, NA_WIDTH)), full((1, NA_WIDTH)),
        full((1, MLA_Q_LORA)), full((1, MLA_KV_LORA)),
        full((MLA_Q_LORA, MLA_QK_PAD)), full((MLA_KV_LORA, MLA_QK_PAD)), full((MLA_KV_LORA, MLA_WIDTH)),
        full((1, HEAD_PAD)), full((1, HEAD_PAD)),
        tab(), tab(), tab(),
    ]
    widths = (NA_WIDTH, NA_WIDTH, NA_WIDTH, MLA_QK_PAD, MLA_QK_PAD, MLA_WIDTH, D_MODEL, D_MODEL)
    return pl.pallas_call(
        _proj_kernel,
        grid=(bsz, nt),
        in_specs=in_specs,
        out_specs=[tok(w) for w in widths],
        out_shape=[jax.ShapeDtypeStruct((bsz, sx, w), BF16) for w in widths],
        compiler_params=_params(("arbitrary", "arbitrary"), VMEM_LIMIT),
        name="mixer_inputs",
    )(x, mod3, mod3, gain3, lw["w_in"], lw["na_q_g"], lw["na_k_g"], lw["cq_g"], lw["ckv_g"],
      lw["w_uq"], lw["w_uk"], lw["w_uv"], lw["q_g"], lw["k_g"], tabs[0], tabs[1], tabs[2])


def _pair_masks(rows):
    lane = lax.broadcasted_iota(jnp.int32, (rows, LANES), 1)
    return lane < NA_HEAD_DIM


def _ones_pad(v, low):
    one = jnp.ones_like(v)
    return jnp.where(low, v, one), jnp.where(low, one, v)


def _pair_finish(o_a, o_b, low):
    num = jnp.where(low, o_a, o_b)
    den = pltpu.roll(jnp.where(low, o_b, o_a), NA_HEAD_DIM, axis=1)
    return (num / den).astype(BF16)


def _na_kernel(q_ref, k_ref, v_ref, kc_ref, vc_ref, bias_ref, o_ref, va_ref, vb_ref, vca_ref, vcb_ref):
    qb = pl.program_id(2)
    n_rows = k_ref.shape[0] // GRID_W
    nwin = NA_WIN_ROWS * GRID_W

    @pl.when(qb == 0)
    def _():
        va_ref[...], vb_ref[...] = _ones_pad(v_ref[...], _pair_masks(v_ref.shape[0]))
        vca_ref[...], vcb_ref[...] = _ones_pad(vc_ref[...], _pair_masks(vc_ref.shape[0]))

    low = _pair_masks(GRID_W)
    kc = kc_ref[...]
    rows = range(NA_QROWS)
    starts, scores = [], []
    for j in rows:
        r = qb * NA_QROWS + j
        rs = jnp.clip(r - NA_WIN_ROWS // 2, 0, n_rows - NA_WIN_ROWS)
        start = pl.multiple_of(rs * GRID_W, GRID_W)
        q = q_ref[j * GRID_W:(j + 1) * GRID_W, :]
        zero = jnp.zeros_like(q)
        qs = jnp.concatenate([jnp.where(low, q, zero), jnp.where(low, zero, q)], axis=0)
        s = _dot_nt(qs, k_ref[pl.ds(start, nwin), :]) + bias_ref[r - rs].reshape(2 * GRID_W, nwin)
        starts.append(start)
        scores.append((s, _dot_nt(qs, kc)))
    probs = []
    for s, sc in scores:
        m = jnp.maximum(jnp.max(s, axis=-1, keepdims=True), jnp.max(sc, axis=-1, keepdims=True))
        probs.append((jnp.exp2(s - m).astype(BF16), jnp.exp2(sc - m).astype(BF16)))
    outs = []
    for start, (p, pc) in zip(starts, probs):
        o_a = _dot(p[:GRID_W], va_ref[pl.ds(start, nwin), :]) + _dot(pc[:GRID_W], vca_ref[...])
        o_b = _dot(p[GRID_W:], vb_ref[pl.ds(start, nwin), :]) + _dot(pc[GRID_W:], vcb_ref[...])
        outs.append(_pair_finish(o_a, o_b, low))
    o_ref[...] = jnp.concatenate(outs, axis=0)


def _na_attention(qa, ka, va, kc, vc, bias):
    bsz, s, _ = qa.shape
    nc = kc.shape[1]
    nq = NA_QROWS * GRID_W
    npairs = NA_WIDTH // LANES
    nwin = NA_WIN_ROWS * GRID_W
    whole = pl.BlockSpec((None, s, LANES), lambda p, b, qb: (b, 0, p))
    cspec = pl.BlockSpec((None, nc, LANES), lambda p, b, qb: (b, 0, p))
    return pl.pallas_call(
        _na_kernel,
        grid=(npairs, bsz, s // nq),
        in_specs=[pl.BlockSpec((None, nq, LANES), lambda p, b, qb: (b, qb, p)),
                  whole, whole, cspec, cspec,
                  pl.BlockSpec((NA_WIN_ROWS, 2, GRID_W, nwin), lambda p, b, qb: (0, p, 0, 0))],
        out_specs=pl.BlockSpec((None, nq, LANES), lambda p, b, qb: (b, qb, p)),
        out_shape=jax.ShapeDtypeStruct((bsz, s, NA_WIDTH), BF16),
        scratch_shapes=[pltpu.VMEM((s, LANES), BF16), pltpu.VMEM((s, LANES), BF16),
                        pltpu.VMEM((nc, LANES), BF16), pltpu.VMEM((nc, LANES), BF16)],
        compiler_params=_params(("arbitrary", "arbitrary", "arbitrary"), VMEM_LIMIT),
        name="na_attention",
    )(qa, ka, va, kc, vc, bias)


def _mla_kernel(q_ref, k_ref, v_ref, kc_ref, vc_ref, o_ref, va_ref, vb_ref, vca_ref, vcb_ref):
    @pl.when(pl.program_id(2) == 0)
    def _():
        va_ref[...], vb_ref[...] = _ones_pad(v_ref[...], _pair_masks(v_ref.shape[0]))
        vca_ref[...], vcb_ref[...] = _ones_pad(vc_ref[...], _pair_masks(vc_ref.shape[0]))

    scores = []
    for hh in range(2):
        sl = slice(hh * HEAD_PAD, (hh + 1) * HEAD_PAD)
        q = q_ref[:, sl]
        scores.append((_dot_nt(q, k_ref[:, sl]), _dot_nt(q, kc_ref[:, sl])))
    probs = []
    for s, sc in scores:
        m = jnp.maximum(jnp.max(s, axis=-1, keepdims=True), jnp.max(sc, axis=-1, keepdims=True))
        probs.append((jnp.exp2(s - m).astype(BF16), jnp.exp2(sc - m).astype(BF16)))
    outs = [_dot(p, vx_ref[...]) + _dot(pc, vcx_ref[...])
            for (p, pc), (vx_ref, vcx_ref) in zip(probs, ((va_ref, vca_ref), (vb_ref, vcb_ref)))]
    o_ref[...] = _pair_finish(outs[0], outs[1], _pair_masks(q_ref.shape[0]))


def _mla_attention(qb, kb, vb, kcb, vcb):
    bsz, s, _ = qb.shape
    nc = kcb.shape[1]
    tq = TQ_MLA
    npairs = MLA_WIDTH // LANES
    return pl.pallas_call(
        _mla_kernel,
        grid=(bsz, npairs, s // tq),
        in_specs=[pl.BlockSpec((None, tq, 2 * HEAD_PAD), lambda b, p, i: (b, i, p)),
                  pl.BlockSpec((None, s, 2 * HEAD_PAD), lambda b, p, i: (b, 0, p)),
                  pl.BlockSpec((None, s, LANES), lambda b, p, i: (b, 0, p)),
                  pl.BlockSpec((None, nc, 2 * HEAD_PAD), lambda b, p, i: (b, 0, p)),
                  pl.BlockSpec((None, nc, LANES), lambda b, p, i: (b, 0, p))],
        out_specs=pl.BlockSpec((None, tq, LANES), lambda b, p, i: (b, i, p)),
        out_shape=jax.ShapeDtypeStruct((bsz, s, MLA_WIDTH), BF16),
        scratch_shapes=[pltpu.VMEM((s, LANES), BF16), pltpu.VMEM((s, LANES), BF16),
                        pltpu.VMEM((nc, LANES), BF16), pltpu.VMEM((nc, LANES), BF16)],
        compiler_params=_params(("arbitrary", "arbitrary", "arbitrary"), VMEM_LIMIT),
        name="mla_attention",
    )(qb, kb, vb, kcb, vcb)


def _softmax2_pv(s, v):
    p = jnp.exp2(s - jnp.max(s, axis=-1, keepdims=True))
    return _dot(p.astype(BF16), v) / jnp.sum(p, axis=-1, keepdims=True)


def _ctx_kernel(qa_ref, ka_ref, va_ref, qb_ref, kb_ref, vb_ref, oa_ref, ob_ref):
    n = qa_ref.shape[0]
    low = _pair_masks(n)
    qa = qa_ref[...]
    zero = jnp.zeros_like(qa)
    ka = ka_ref[...]
    va = va_ref[...]
    o_lo = _softmax2_pv(_dot_nt(jnp.where(low, qa, zero), ka), va)
    o_hi = _softmax2_pv(_dot_nt(jnp.where(low, zero, qa), ka), va)
    oa_ref[...] = jnp.where(low, o_lo, o_hi).astype(BF16)
    vb = vb_ref[...]
    outs = []
    for hh in range(2):
        sl = slice(hh * HEAD_PAD, (hh + 1) * HEAD_PAD)
        outs.append(_softmax2_pv(_dot_nt(qb_ref[:, sl], kb_ref[:, sl]), vb))
    ob_ref[...] = jnp.where(low, outs[0], outs[1]).astype(BF16)


def _ctx_attention(qa, ka, va, qb, kb, vb):
    bsz, n, _ = qa.shape
    npairs = NA_WIDTH // LANES
    narrow = pl.BlockSpec((None, n, LANES), lambda b, p: (b, 0, p))
    wide = pl.BlockSpec((None, n, 2 * HEAD_PAD), lambda b, p: (b, 0, p))
    return pl.pallas_call(
        _ctx_kernel,
        grid=(bsz, npairs),
        in_specs=[narrow, narrow, narrow, wide, wide, narrow],
        out_specs=[narrow, narrow],
        out_shape=[jax.ShapeDtypeStruct((bsz, n, NA_WIDTH), BF16),
                   jax.ShapeDtypeStruct((bsz, n, MLA_WIDTH), BF16)],
        compiler_params=_params(("arbitrary", "arbitrary")),
        name="ctx_attention",
    )(qa, ka, va, qb, kb, vb)


def _merge_kernel(x_ref, oa_ref, ob_ref, ga_ref, gb_ref, woa_ref, wob_ref, wout_ref, gm_ref,
                  g_ref, sh_ref, sc_ref, wrh_ref, wrl_ref, xo_ref, h_ref, lg_ref):
    y = (ga_ref[...].astype(F32) * _dot(oa_ref[...], woa_ref[...])
         + gb_ref[...].astype(F32) * _dot(ob_ref[...], wob_ref[...]))
    xn = x_ref[...] + gm_ref[...] * _dot(y.astype(BF16), wout_ref[...])
    xo_ref[...] = xn
    h = _rms(xn, D_MODEL) * g_ref[...]
    h = h * (1.0 + sc_ref[...]) + sh_ref[...]
    _store_slabs(h_ref, _pack_rows(h))
    h_hi, h_lo = _split_bf16(h)
    wrh = wrh_ref[...]
    lg_ref[...] = _dot_nt(wrh, h_hi) + _dot_nt(wrh, h_lo) + _dot_nt(wrl_ref[...], h_hi)


def _merge(x, oa, ob, ga, gb, mod3, mod_row, gain3, l, lw):
    bsz, sx, _ = x.shape
    tm = min(TM_MERGE, sx)
    nt = sx // tm
    row = (lambda b: b) if mod_row is None else (lambda b: mod_row)

    def full(shape):
        return pl.BlockSpec(shape, lambda b, i: (0,) * len(shape))

    def tok(width):
        return pl.BlockSpec((None, tm, width), lambda b, i: (b, i, 0))

    def mod(chunk):
        return pl.BlockSpec((None, 1, D_MODEL), lambda b, i: (row(b), 0, chunk))

    return pl.pallas_call(
        _merge_kernel,
        grid=(bsz, nt),
        in_specs=[tok(D_MODEL), tok(NA_WIDTH), tok(MLA_WIDTH), tok(D_MODEL), tok(D_MODEL),
                  full((NA_WIDTH, D_MODEL)), full((MLA_WIDTH, D_MODEL)), full((D_MODEL, D_MODEL)),
                  mod(2),
                  pl.BlockSpec((None, 1, D_MODEL), lambda b, i: (l, 0, 0)),
                  mod(3), mod(4),
                  full((N_EXPERTS, D_MODEL)), full((N_EXPERTS, D_MODEL))],
        out_specs=[tok(D_MODEL),
                   pl.BlockSpec((tm * PACK_SUB, LANES), lambda b, i: (b * nt + i, 0)),
                   pl.BlockSpec((N_EXPERTS, tm), lambda b, i: (0, b * nt + i))],
        out_shape=[jax.ShapeDtypeStruct((bsz, sx, D_MODEL), F32),
                   jax.ShapeDtypeStruct((bsz * sx * PACK_SUB, LANES), jnp.uint32),
                   jax.ShapeDtypeStruct((N_EXPERTS, bsz * sx), F32)],
        compiler_params=_params(("arbitrary", "arbitrary"), VMEM_LIMIT),
        name="merge_branches",
    )(x, oa, ob, ga, gb, lw["w_oa"], lw["w_ob"], lw["w_out"], mod3, gain3, mod3, mod3,
      lw["wr_hi"], lw["wr_lo"])


def _route_kernel(lg_ref, bias_ref, tri_ref, eid_ref, gw_ref, rank_ref, cnt_ref, run_ref):
    @pl.when(pl.program_id(0) == 0)
    def _():
        run_ref[...] = jnp.zeros_like(run_ref)

    tr = lg_ref.shape[1]
    s = _sigmoid(lg_ref[...])
    biased = s + bias_ref[...]
    sub = lax.broadcasted_iota(jnp.int32, (EXPERTS_PER_GROUP, tr), 0).astype(F32)
    none = float(EXPERTS_PER_GROUP)
    best = jnp.full((1, tr), -jnp.inf, F32)
    e1 = jnp.zeros((1, tr), F32)
    e2 = jnp.zeros((1, tr), F32)
    for g in range(N_GROUPS):
        bg = biased[g * EXPERTS_PER_GROUP:(g + 1) * EXPERTS_PER_GROUP]
        m1 = jnp.max(bg, axis=0, keepdims=True)
        i1 = jnp.min(jnp.where(bg == m1, sub, none), axis=0, keepdims=True)
        rest = jnp.where(sub == i1, -jnp.inf, bg)
        m2 = jnp.max(rest, axis=0, keepdims=True)
        i2 = jnp.min(jnp.where(rest == m2, sub, none), axis=0, keepdims=True)
        score = m1 + m2
        better = score > best
        best = jnp.where(better, score, best)
        e1 = jnp.where(better, g * EXPERTS_PER_GROUP + i1, e1)
        e2 = jnp.where(better, g * EXPERTS_PER_GROUP + i2, e2)

    rowid = lax.broadcasted_iota(jnp.int32, (N_EXPERTS, tr), 0).astype(F32)
    is1 = rowid == e1
    is2 = rowid == e2
    w1 = jnp.sum(jnp.where(is1, s, 0.0), axis=0, keepdims=True)
    w2 = jnp.sum(jnp.where(is2, s, 0.0), axis=0, keepdims=True)
    tot = w1 + w2
    gw_ref[0:1, :] = w1 / tot
    gw_ref[1:2, :] = w2 / tot
    eid_ref[0:1, :] = e1.astype(jnp.int32)
    eid_ref[1:2, :] = e2.astype(jnp.int32)

    onehot = jnp.where(is1 | is2, 1.0, 0.0)
    before = _dot(onehot.astype(BF16), tri_ref[...]) + run_ref[...]
    rank_ref[0:1, :] = jnp.sum(jnp.where(is1, before, 0.0), axis=0, keepdims=True).astype(jnp.int32)
    rank_ref[1:2, :] = jnp.sum(jnp.where(is2, before, 0.0), axis=0, keepdims=True).astype(jnp.int32)
    run_ref[...] = run_ref[...] + jnp.sum(onehot, axis=1, keepdims=True)
    cnt_ref[...] = run_ref[...]


def _route(logits, bias_col, tri):
    t = logits.shape[1]
    tr = TR_ROUTE
    pair = pl.BlockSpec((TOP_K, tr), lambda i: (0, i))
    return pl.pallas_call(
        _route_kernel,
        grid=(t // tr,),
        in_specs=[pl.BlockSpec((N_EXPERTS, tr), lambda i: (0, i)),
                  pl.BlockSpec((N_EXPERTS, 1), lambda i: (0, 0)),
                  pl.BlockSpec((tr, tr), lambda i: (0, 0))],
        out_specs=[pair, pair, pair, pl.BlockSpec((N_EXPERTS, 1), lambda i: (0, 0))],
        out_shape=[jax.ShapeDtypeStruct((TOP_K, t), jnp.int32),
                   jax.ShapeDtypeStruct((TOP_K, t), F32),
                   jax.ShapeDtypeStruct((TOP_K, t), jnp.int32),
                   jax.ShapeDtypeStruct((N_EXPERTS, 1), F32)],
        scratch_shapes=[pltpu.VMEM((N_EXPERTS, 1), F32)],
        compiler_params=_params(("arbitrary",)),
        name="route",
    )(logits, bias_col, tri)


def _dispatch_kernel(dest_ref, h_ref, slots_in_ref, slots_ref, sem, *, n_tok):
    del slots_in_ref
    tm = h_ref.shape[0] // PACK_SUB
    base = pl.program_id(0) * tm

    def copy(r, k):
        d = dest_ref[k * n_tok + base + r]
        return pltpu.make_async_copy(h_ref.at[pl.ds(pl.multiple_of(r * PACK_SUB, PACK_SUB), PACK_SUB)],
                                     slots_ref.at[pl.ds(pl.multiple_of(d * PACK_SUB, PACK_SUB), PACK_SUB)], sem)

    def issue(r, carry):
        for k in range(TOP_K):
            copy(r, k).start(priority=k)
        return carry

    def drain(r, carry):
        for k in range(TOP_K):
            copy(r, k).wait()
        return carry

    lax.fori_loop(0, tm, issue, 0, unroll=MOVE_UNROLL)
    lax.fori_loop(0, tm, drain, 0, unroll=MOVE_UNROLL)


def _dispatch(h, dest_flat, slots):
    n_tok = h.shape[0] // PACK_SUB
    tm = TM_MOVE
    return pl.pallas_call(
        functools.partial(_dispatch_kernel, n_tok=n_tok),
        grid_spec=pltpu.PrefetchScalarGridSpec(
            num_scalar_prefetch=1,
            grid=(n_tok // tm,),
            in_specs=[pl.BlockSpec((tm * PACK_SUB, LANES), lambda i, dest: (i, 0)),
                      pl.BlockSpec(memory_space=pl.ANY)],
            out_specs=pl.BlockSpec(memory_space=pl.ANY),
            scratch_shapes=[pltpu.SemaphoreType.DMA],
        ),
        out_shape=jax.ShapeDtypeStruct(slots.shape, slots.dtype),
        input_output_aliases={2: 0},
        compiler_params=_params(("arbitrary",)),
        name="moe_dispatch",
    )(dest_flat, h, slots)


def _expert_kernel(be_ref, nused_ref, nxt_ref, tok_ref, h_hbm, w1_hbm, w3_hbm, w2_hbm, y_ref,
                   xbuf, st1, st3, st2, w1b, w3b, w2b, sem, gsem, *, layer):
    i = pl.program_id(0)
    last = pl.num_programs(0) - 1
    bm = xbuf.shape[1] // PACK_SUB
    used = i < nused_ref[0]
    stages = ((w1_hbm, st1, w1b), (w3_hbm, st3, w3b), (w2_hbm, st2, w2b))
    cur = i % 2
    nxt_slot = (i + 1) % 2
    nxt_block = jnp.minimum(i + 1, last)

    def fetch(e):
        return [pltpu.make_async_copy(w_hbm.at[layer, e], st, sem.at[j]) for j, (w_hbm, st, _) in enumerate(stages)]

    def row_copy(block, slot, r):
        t = tok_ref[block * bm + r]
        return pltpu.make_async_copy(h_hbm.at[pl.ds(pl.multiple_of(t * PACK_SUB, PACK_SUB), PACK_SUB)],
                                     xbuf.at[slot, pl.ds(r * PACK_SUB, PACK_SUB)], gsem.at[slot])

    def gather_next(rows):
        for r in rows:
            row_copy(nxt_block, nxt_slot, r).start(priority=r % 2)

    @pl.when(i == 0)
    def _():
        for r in range(bm):
            row_copy(i, cur, r).start(priority=r % 2)

    for r in range(bm):
        row_copy(i, cur, r).wait()

    @pl.when(used)
    def _():
        e = be_ref[i]
        prev = be_ref[jnp.maximum(i - 1, 0)]

        @pl.when(i == 0)
        def _():
            for c in fetch(e):
                c.start()

        @pl.when((i == 0) | (e != prev))
        def _():
            for c, (_, st, wb) in zip(fetch(e), stages):
                c.wait()
                wb[...] = st[...].astype(BF16)
            nxt = nxt_ref[e]

            @pl.when(nxt >= 0)
            def _():
                for c in fetch(nxt):
                    c.start()

        x_lo, x_hi = _unpack_rows(_load_slabs(xbuf.at[cur], bm))
        xb = jnp.concatenate([x_lo.astype(BF16), x_hi.astype(BF16)], axis=1)
        de = w1b.shape[1]
        cw = de // EXP_CHUNKS
        acts = []
        for c in range(EXP_CHUNKS):
            cols = slice(c * cw, (c + 1) * cw)
            a = _dot(xb, w1b[:, cols])
            b = _dot(xb, w3b[:, cols])
            acts.append((a * _sigmoid(a) * b).astype(BF16))
            gather_next(range(c * bm // EXP_CHUNKS, (c + 1) * bm // EXP_CHUNKS))
        _store_slabs(y_ref, _pack_rows(_dot(jnp.concatenate(acts, axis=1), w2b[...])))

    @pl.when(jnp.logical_not(used))
    def _():
        y_ref[...] = jnp.zeros_like(y_ref)
        gather_next(range(bm))

    @pl.when(i == last)
    def _():
        for r in range(bm):
            row_copy(nxt_block, nxt_slot, r).wait()


def _experts(h_rows, slot_tok, block_expert, nused, next_expert, w1, w3, w2, l):
    n_slots = slot_tok.shape[0]
    bm = BM_EXP
    de = w1.shape[-1]
    hbm = pl.BlockSpec(memory_space=pl.ANY)
    return pl.pallas_call(
        functools.partial(_expert_kernel, layer=l),
        grid_spec=pltpu.PrefetchScalarGridSpec(
            num_scalar_prefetch=4,
            grid=(n_slots // bm,),
            in_specs=[hbm, hbm, hbm, hbm],
            out_specs=pl.BlockSpec((bm * PACK_SUB, LANES), lambda i, be, nu, nx, tk: (i, 0)),
            scratch_shapes=[pltpu.VMEM((2, bm * PACK_SUB, LANES), jnp.uint32),
                            pltpu.VMEM((D_MODEL, de), F32), pltpu.VMEM((D_MODEL, de), F32),
                            pltpu.VMEM((de, D_MODEL), F32),
                            pltpu.VMEM((D_MODEL, de), BF16), pltpu.VMEM((D_MODEL, de), BF16),
                            pltpu.VMEM((de, D_MODEL), BF16),
                            pltpu.SemaphoreType.DMA((3,)), pltpu.SemaphoreType.DMA((2,))],
        ),
        out_shape=jax.ShapeDtypeStruct((n_slots * PACK_SUB, LANES), jnp.uint32),
        compiler_params=_params(("arbitrary",), VMEM_LIMIT),
        name="moe_experts",
    )(block_expert, nused, next_expert, slot_tok, h_rows, w1, w3, w2)


def _combine_kernel(dest_ref, x_ref, gw_ref, gf_ref, y_hbm, o_ref, ybuf, sem, *, n_tok):
    tm = x_ref.shape[0]
    i = pl.program_id(0)
    nsteps = pl.num_programs(0)

    def copy(step, r, k):
        slot = step % 2
        d = dest_ref[k * n_tok + step * tm + r]
        return pltpu.make_async_copy(y_hbm.at[pl.ds(pl.multiple_of(d * PACK_SUB, PACK_SUB), PACK_SUB)],
                                     ybuf.at[slot, k, pl.ds(pl.multiple_of(r * PACK_SUB, PACK_SUB), PACK_SUB)],
                                     sem.at[slot])

    def issue(step):
        def body(r, carry):
            for k in range(TOP_K):
                copy(step, r, k).start(priority=k)
            return carry
        lax.fori_loop(0, tm, body, 0, unroll=MOVE_UNROLL)

    @pl.when(i == 0)
    def _():
        issue(i)

    @pl.when(i + 1 < nsteps)
    def _():
        issue(i + 1)

    def drain(r, carry):
        for k in range(TOP_K):
            copy(i, r, k).wait()
        return carry
    lax.fori_loop(0, tm, drain, 0, unroll=MOVE_UNROLL)

    slot = i % 2
    gw = gw_ref[...]
    halves = [_unpack_rows(_load_slabs(ybuf.at[slot, k], tm)) for k in range(TOP_K)]
    gf = gf_ref[...]
    for part in range(2):
        cols = slice(part * PACK_W, (part + 1) * PACK_W)
        y = gw[:, 0:1] * halves[0][part] + gw[:, 1:2] * halves[1][part]
        o_ref[:, cols] = x_ref[:, cols] + gf[:, cols] * y


def _combine(x, y_slots, dest_flat, gw_t, mod3, mod_row):
    bsz, sx, _ = x.shape
    n_tok = bsz * sx
    tm = TM_MOVE
    nt = sx // tm
    row = (lambda i: i // nt) if mod_row is None else (lambda i: mod_row)
    out = pl.pallas_call(
        functools.partial(_combine_kernel, n_tok=n_tok),
        grid_spec=pltpu.PrefetchScalarGridSpec(
            num_scalar_prefetch=1,
            grid=(n_tok // tm,),
            in_specs=[pl.BlockSpec((tm, D_MODEL), lambda i, dest: (i, 0)),
                      pl.BlockSpec((tm, TOP_K), lambda i, dest: (i, 0)),
                      pl.BlockSpec((None, 1, D_MODEL), lambda i, dest: (row(i), 0, 5)),
                      pl.BlockSpec(memory_space=pl.ANY)],
            out_specs=pl.BlockSpec((tm, D_MODEL), lambda i, dest: (i, 0)),
            scratch_shapes=[pltpu.VMEM((2, TOP_K, tm * PACK_SUB, LANES), jnp.uint32),
                            pltpu.SemaphoreType.DMA((2,))],
        ),
        out_shape=jax.ShapeDtypeStruct((n_tok, D_MODEL), F32),
        compiler_params=_params(("arbitrary",)),
        name="moe_combine",
    )(dest_flat, x.reshape(n_tok, D_MODEL), gw_t, mod3, y_slots)
    return out.reshape(bsz, sx, D_MODEL)


def _layer_weights(l, w_in, na_q_g, na_k_g, mla_cq_g, w_uq, mla_ckv_g, w_ukv, mla_q_g, mla_k_g,
                   w_oa, w_ob, w_out, w_router):
    wi = w_in[l]
    c0 = C_NA + MLA_Q_LORA + MLA_KV_LORA
    kr_cols = jnp.zeros((D_MODEL, HEAD_PAD), F32).at[:, MLA_NOPE:MLA_QK_DIM].set(wi[:, c0:c0 + MLA_ROPE])
    w_in_arr = jnp.concatenate([wi[:, :c0], kr_cols, wi[:, c0 + MLA_ROPE:]], axis=1).astype(BF16)

    def pad_heads(w, width):
        w = w.reshape(w.shape[0], MLA_HEADS, width)
        return jnp.pad(w, ((0, 0), (0, 0), (0, HEAD_PAD - width))).reshape(w.shape[0], MLA_QK_PAD)

    ukv = w_ukv[l].reshape(MLA_KV_LORA, MLA_HEADS, MLA_NOPE + MLA_V_DIM)
    wr_t = w_router.T
    wr_hi = wr_t.astype(BF16)
    return {
        "w_in": w_in_arr,
        "na_q_g": jnp.tile(na_q_g[l], NA_HEADS)[None, :],
        "na_k_g": jnp.tile(na_k_g[l], NA_HEADS)[None, :],
        "cq_g": mla_cq_g[l][None, :],
        "ckv_g": mla_ckv_g[l][None, :],
        "w_uq": pad_heads(w_uq[l], MLA_QK_DIM).astype(BF16),
        "w_uk": pad_heads(ukv[:, :, :MLA_NOPE].reshape(MLA_KV_LORA, -1), MLA_NOPE).astype(BF16),
        "w_uv": ukv[:, :, MLA_NOPE:].reshape(MLA_KV_LORA, MLA_WIDTH).astype(BF16),
        "q_g": jnp.pad(mla_q_g[l], (0, HEAD_PAD - MLA_QK_DIM))[None, :],
        "k_g": jnp.pad(mla_k_g[l], (0, HEAD_PAD - MLA_QK_DIM))[None, :],
        "w_oa": w_oa[l].astype(BF16),
        "w_ob": w_ob[l].astype(BF16),
        "w_out": w_out[l].astype(BF16),
        "wr_hi": wr_hi,
        "wr_lo": (wr_t - wr_hi.astype(F32)).astype(BF16),
    }


def _rope_tables(s):
    half = MLA_ROPE // 4
    pos = jnp.arange(s, dtype=jnp.int32)
    inv = ROPE_BASE ** (-jnp.arange(half, dtype=F32) / half)
    ang_r = (pos // GRID_W).astype(F32)[:, None] * inv[None, :]
    ang_c = (pos % GRID_W).astype(F32)[:, None] * inv[None, :]
    zeros = jnp.zeros((s, half), F32)
    lead = jnp.zeros((s, MLA_NOPE), F32)
    tail = jnp.zeros((s, HEAD_PAD - MLA_QK_DIM), F32)
    cos = jnp.concatenate([lead + 1.0, jnp.cos(ang_r), jnp.cos(ang_r), jnp.cos(ang_c), jnp.cos(ang_c), tail + 1.0], 1)
    s1 = jnp.concatenate([lead, -jnp.sin(ang_r), zeros, -jnp.sin(ang_c), zeros, tail], 1)
    s2 = jnp.concatenate([lead, zeros, jnp.sin(ang_r), zeros, jnp.sin(ang_c), tail], 1)
    return cos, s1, s2


def _na_bias_tables(rpb):
    d = jnp.arange(NA_WIN_ROWS)[:, None]
    i = jnp.arange(NA_WIN_ROWS)[None, :]
    qc = jnp.arange(GRID_W)[:, None]
    kc = jnp.arange(GRID_W)[None, :]
    cs = jnp.clip(qc - NA_WIN_COLS // 2, 0, GRID_W - NA_WIN_COLS)
    cvalid = (kc >= cs) & (kc < cs + NA_WIN_COLS)
    ohr = jax.nn.one_hot(i - d + NA_WIN_ROWS - 1, 2 * NA_WIN_ROWS - 1, dtype=F32)
    ohc = jax.nn.one_hot(jnp.clip(kc - qc + NA_WIN_COLS - 1, 0, 2 * NA_WIN_COLS - 2), 2 * NA_WIN_COLS - 1, dtype=F32)
    t = jnp.einsum("dia,hab,qkb->dhqik", ohr, rpb.astype(F32), ohc, precision=lax.Precision.HIGHEST)
    t = jnp.where(cvalid[None, None, :, None, :], t * LOG2E, NEG_INF)
    return t.reshape(NA_WIN_ROWS, rpb.shape[0], GRID_W, NA_WIN_ROWS * GRID_W)


def _slot_tables(eid, rank, counts):
    bm = BM_EXP
    counts = counts[:, 0].astype(jnp.int32)
    padded = ((counts + bm - 1) // bm) * bm
    pad_end = jnp.cumsum(padded)
    pad_start = pad_end - padded
    experts = jnp.arange(N_EXPERTS, dtype=jnp.int32)
    dest = rank + jnp.sum(jnp.where(eid[..., None] == experts, pad_start, 0), axis=-1)
    m = eid.shape[1] * TOP_K
    n_blocks = -(-m // bm) + N_EXPERTS
    blk = jnp.arange(n_blocks, dtype=jnp.int32) * bm
    block_expert = jnp.minimum(jnp.sum(pad_end[None, :] <= blk[:, None], axis=1), N_EXPERTS - 1).astype(jnp.int32)
    nused = (pad_end[-1:] // bm).astype(jnp.int32)
    later = lax.cummin(jnp.where(counts > 0, experts, N_EXPERTS), axis=0, reverse=True)
    next_expert = jnp.concatenate([later[1:], jnp.full((1,), N_EXPERTS, jnp.int32)])
    next_expert = jnp.where(next_expert < N_EXPERTS, next_expert, -1).astype(jnp.int32)
    dest = dest.astype(jnp.int32)
    tok = jnp.broadcast_to(jnp.arange(eid.shape[1], dtype=jnp.int32), dest.shape)
    slot_tok = jnp.zeros((n_blocks * bm,), jnp.int32).at[dest.reshape(-1)].set(tok.reshape(-1), unique_indices=True)
    return dest, slot_tok, block_expert, nused, next_expert


def kernel(x, c, ctx, c_ctx, w_ada, b_ada, norm_mix_g, norm_ffn_g, w_in, na_q_g, na_k_g, na_rpb,
           mla_cq_g, w_uq, mla_ckv_g, w_ukv, mla_q_g, mla_k_g, w_oa, w_ob, w_out,
           w_router, router_bias, w1, w3, w2):
    bsz, s, d = x.shape
    n_ctx = ctx.shape[1]
    ctx_row = bsz
    pad_rows = -(bsz + 1) % 8
    cvec = jnp.concatenate([c, c_ctx[None, :], jnp.zeros((pad_rows, d), F32)], axis=0)
    b_ada3 = b_ada[:, None, :]
    mix_g3 = norm_mix_g[:, None, :]
    ffn_g3 = norm_ffn_g[:, None, :]
    tabs_x = _rope_tables(s)
    tabs_c = (jnp.ones((n_ctx, LANES), F32), jnp.zeros((n_ctx, LANES), F32), jnp.zeros((n_ctx, LANES), F32))
    tri = (jnp.arange(TR_ROUTE)[:, None] < jnp.arange(TR_ROUTE)[None, :]).astype(BF16)
    bias_col = router_bias.astype(F32)[:, None]
    n_x = bsz * s

    xc = ctx
    for l in range(DEPTH):
        last = l == DEPTH - 1
        lw = _layer_weights(l, w_in, na_q_g, na_k_g, mla_cq_g, w_uq, mla_ckv_g, w_ukv, mla_q_g, mla_k_g,
                            w_oa, w_ob, w_out, w_router)
        mod3 = _ada(cvec, w_ada, b_ada3, l)[:, None, :]
        qa, ka, va, qb, kb, vb, ga, gb = _proj(x, mod3, None, mix_g3, l, lw, tabs_x)
        qa_c, ka_c, va_c, qb_c, kb_c, vb_c, ga_c, gb_c = _proj(xc, mod3, ctx_row, mix_g3, l, lw, tabs_c)
        oa = _na_attention(qa, ka, va, ka_c, va_c, _na_bias_tables(na_rpb[l]))
        ob = _mla_attention(qb, kb, vb, kb_c, vb_c)
        x, h_x, lg = _merge(x, oa, ob, ga, gb, mod3, None, ffn_g3, l, lw)
        if not last:
            oa_c, ob_c = _ctx_attention(qa_c, ka_c, va_c, qb_c, kb_c, vb_c)
            xc, h_c, lg_c = _merge(xc, oa_c, ob_c, ga_c, gb_c, mod3, ctx_row, ffn_g3, l, lw)
            lg = jnp.concatenate([lg, lg_c], axis=1)
            h_rows = jnp.concatenate([h_x, h_c], axis=0)
        else:
            h_rows = h_x

        eid, gw, rank, counts = _route(lg, bias_col, tri)
        dest, slot_tok, block_expert, nused, next_expert = _slot_tables(eid, rank, counts)
        gw_t = gw.T
        y_slots = _experts(h_rows, slot_tok, block_expert, nused, next_expert, w1, w3, w2, l)
        x = _combine(x, y_slots, dest[:, :n_x].reshape(-1), gw_t[:n_x], mod3, None)
        if not last:
            xc = _combine(xc, y_slots, dest[:, n_x:].reshape(-1), gw_t[n_x:], mod3, ctx_row)
    return x
```

```python
import functools

import jax
import jax.numpy as jnp
from jax import lax
from jax.experimental import pallas as pl
from jax.experimental.pallas import tpu as pltpu

F32 = jnp.float32
BF16 = jnp.bfloat16

D_MODEL = 1024
DEPTH = 2
GRID_W = 64
N_MOD = 6

NA_HEADS = 8
NA_HEAD_DIM = 64
NA_WIN_ROWS = 8
NA_WIN_COLS = 16
NA_WIDTH = NA_HEADS * NA_HEAD_DIM

MLA_HEADS = 8
MLA_NOPE = 64
MLA_ROPE = 32
MLA_QK_DIM = MLA_NOPE + MLA_ROPE
MLA_V_DIM = 64
MLA_Q_LORA = 384
MLA_KV_LORA = 256
MLA_WIDTH = MLA_HEADS * MLA_V_DIM
ROPE_BASE = 10000.0

N_EXPERTS = 64
EXPERTS_PER_GROUP = 8
N_GROUPS = N_EXPERTS // EXPERTS_PER_GROUP
TOP_K = 2

RMS_EPS = 1e-6
NEG_INF = -1e30
LOG2E = 1.4426950408889634

LANES = 128
HEAD_PAD = LANES
MLA_QK_PAD = MLA_HEADS * HEAD_PAD

C_NA = 3 * NA_WIDTH
C_CQ = C_NA + MLA_Q_LORA
C_CKV = C_CQ + MLA_KV_LORA
C_KR = C_CKV + HEAD_PAD
C_ALL = C_KR + 2 * D_MODEL

TM_PROJ = 256
TM_MERGE = 512
TQ_MLA = 512
NA_QROWS = 16
TR_ROUTE = 512
BM_EXP = 256
PACK_W = D_MODEL // 2
PACK_SUB = PACK_W // LANES
TM_MOVE = 512
MOVE_UNROLL = 8
VMEM_LIMIT = 56 * 1024 * 1024


def _sigmoid(v):
    return 1.0 / (1.0 + jnp.exp(-v))


def _rms(v, n):
    return v * lax.rsqrt(jnp.sum(v * v, axis=-1, keepdims=True) * (1.0 / n) + RMS_EPS)


def _pack_rows(v):
    lo = pltpu.bitcast(v[:, :PACK_W].astype(BF16).astype(F32), jnp.uint32)
    hi = pltpu.bitcast(v[:, PACK_W:].astype(BF16).astype(F32), jnp.uint32)
    return (lo >> 16) | (hi & jnp.uint32(0xFFFF0000))


def _unpack_rows(w):
    return pltpu.bitcast(w << 16, F32), pltpu.bitcast(w & jnp.uint32(0xFFFF0000), F32)


def _store_slabs(ref, w):
    for c in range(PACK_SUB):
        ref[pl.ds(c, w.shape[0], stride=PACK_SUB), :] = w[:, c * LANES:(c + 1) * LANES]


def _load_slabs(ref, m):
    return jnp.concatenate([ref[pl.ds(c, m, stride=PACK_SUB), :] for c in range(PACK_SUB)], axis=1)


def _split_bf16(a):
    hi = a.astype(BF16)
    lo = (a - hi.astype(F32)).astype(BF16)
    return hi, lo


def _dot(a, b):
    return jnp.dot(a, b, preferred_element_type=F32)


def _dot_nt(a, b):
    return lax.dot_general(a, b, (((1,), (1,)), ((), ())), preferred_element_type=F32)


def _params(sem, vmem=None):
    return pltpu.CompilerParams(dimension_semantics=sem, vmem_limit_bytes=vmem)


def _ada_kernel(c_ref, w_ref, b_ref, o_ref):
    cv = c_ref[...]
    s = cv * _sigmoid(cv)
    s_hi, s_lo = _split_bf16(s)
    w_hi, w_lo = _split_bf16(w_ref[...])
    o_ref[...] = _dot(s_hi, w_hi) + _dot(s_lo, w_hi) + _dot(s_hi, w_lo) + b_ref[...]


def _ada(cvec, w_ada, b_ada3, l):
    rows = cvec.shape[0]
    n = N_MOD * D_MODEL
    tn = 512
    return pl.pallas_call(
        _ada_kernel,
        grid=(n // tn,),
        in_specs=[
            pl.BlockSpec((rows, D_MODEL), lambda j: (0, 0)),
            pl.BlockSpec((None, D_MODEL, tn), lambda j: (l, 0, j)),
            pl.BlockSpec((None, 1, tn), lambda j: (l, 0, j)),
        ],
        out_specs=pl.BlockSpec((rows, tn), lambda j: (0, j)),
        out_shape=jax.ShapeDtypeStruct((rows, n), F32),
        compiler_params=_params(("arbitrary",)),
        name="ada_mod",
    )(cvec, w_ada, b_ada3)


def _rope(v, cos, s1, s2):
    return v * cos + pltpu.roll(v, LANES - 8, axis=1) * s1 + pltpu.roll(v, 8, axis=1) * s2


def _proj_kernel(x_ref, sh_ref, sc_ref, g_ref, win_ref, naqg_ref, nakg_ref, cqg_ref, ckvg_ref,
                 wuq_ref, wuk_ref, wuv_ref, qg_ref, kg_ref, cos_ref, s1_ref, s2_ref,
                 qa_ref, ka_ref, va_ref, qb_ref, kb_ref, vb_ref, ga_ref, gb_ref):
    x = x_ref[...]
    h = _rms(x, D_MODEL) * g_ref[...]
    h = h * (1.0 + sc_ref[...]) + sh_ref[...]
    hb = h.astype(BF16)

    low = _pair_masks(x.shape[0])

    def headnorm(z, g):
        tiles = []
        for c in range(NA_WIDTH // LANES):
            zc = z[:, c * LANES:(c + 1) * LANES]
            sq = zc * zc
            lo = jnp.sum(jnp.where(low, sq, 0.0), axis=-1, keepdims=True)
            hi = jnp.sum(jnp.where(low, 0.0, sq), axis=-1, keepdims=True)
            tiles.append(zc * lax.rsqrt(jnp.where(low, lo, hi) * (1.0 / NA_HEAD_DIM) + RMS_EPS))
        return jnp.concatenate(tiles, axis=1) * g

    zq = _dot(hb, win_ref[:, 0:NA_WIDTH])
    qa_ref[...] = (headnorm(zq, naqg_ref[...]) * (NA_HEAD_DIM ** -0.5 * LOG2E)).astype(BF16)
    zk = _dot(hb, win_ref[:, NA_WIDTH:2 * NA_WIDTH])
    ka_ref[...] = headnorm(zk, nakg_ref[...]).astype(BF16)
    va_ref[...] = _dot(hb, win_ref[:, 2 * NA_WIDTH:C_NA]).astype(BF16)

    cos = cos_ref[...]
    s1 = s1_ref[...]
    s2 = s2_ref[...]

    cq = _rms(_dot(hb, win_ref[:, C_NA:C_CQ]), MLA_Q_LORA) * cqg_ref[...]
    q = _dot(cq.astype(BF16), wuq_ref[...])
    qg = qg_ref[...]
    for hh in range(MLA_HEADS):
        sl = slice(hh * HEAD_PAD, (hh + 1) * HEAD_PAD)
        qn = _rms(q[:, sl], MLA_QK_DIM) * qg
        qb_ref[:, sl] = (_rope(qn, cos, s1, s2) * (MLA_QK_DIM ** -0.5 * LOG2E)).astype(BF16)

    ckv = (_rms(_dot(hb, win_ref[:, C_CQ:C_CKV]), MLA_KV_LORA) * ckvg_ref[...]).astype(BF16)
    kn = _dot(ckv, wuk_ref[...])
    kr = _dot(hb, win_ref[:, C_CKV:C_KR])
    kg = kg_ref[...]
    for hh in range(MLA_HEADS):
        sl = slice(hh * HEAD_PAD, (hh + 1) * HEAD_PAD)
        kh = _rms(kn[:, sl] + kr, MLA_QK_DIM) * kg
        kb_ref[:, sl] = _rope(kh, cos, s1, s2).astype(BF16)
    vb_ref[...] = _dot(ckv, wuv_ref[...]).astype(BF16)

    ga_ref[...] = _sigmoid(_dot(hb, win_ref[:, C_KR:C_KR + D_MODEL])).astype(BF16)
    gb_ref[...] = _sigmoid(_dot(hb, win_ref[:, C_KR + D_MODEL:C_ALL])).astype(BF16)


def _proj(x, mod3, mod_row, gain3, l, lw, tabs):
    bsz, sx, _ = x.shape
    tm = min(TM_PROJ, sx)
    nt = sx // tm
    row = (lambda b: b) if mod_row is None else (lambda b: mod_row)

    def full(shape):
        return pl.BlockSpec(shape, lambda b, i: (0,) * len(shape))

    def tok(width):
        return pl.BlockSpec((None, tm, width), lambda b, i: (b, i, 0))

    def tab():
        return pl.BlockSpec((tm, LANES), lambda b, i: (i, 0))

    in_specs = [
        tok(D_MODEL),
        pl.BlockSpec((None, 1, D_MODEL), lambda b, i: (row(b), 0, 0)),
        pl.BlockSpec((None, 1, D_MODEL), lambda b, i: (row(b), 0, 1)),
        pl.BlockSpec((None, 1, D_MODEL), lambda b, i: (l, 0, 0)),
        full((D_MODEL, C_ALL)),
        full((1, NA_WIDTH)), full((1, NA_WIDTH)),
        full((1, MLA_Q_LORA)), full((1, MLA_KV_LORA)),
        full((MLA_Q_LORA, MLA_QK_PAD)), full((MLA_KV_LORA, MLA_QK_PAD)), full((MLA_KV_LORA, MLA_WIDTH)),
        full((1, HEAD_PAD)), full((1, HEAD_PAD)),
        tab(), tab(), tab(),
    ]
    widths = (NA_WIDTH, NA_WIDTH, NA_WIDTH, MLA_QK_PAD, MLA_QK_PAD, MLA_WIDTH, D_MODEL, D_MODEL)
    return pl.pallas_call(
        _proj_kernel,
        grid=(bsz, nt),
        in_specs=in_specs,
        out_specs=[tok(w) for w in widths],
        out_shape=[jax.ShapeDtypeStruct((bsz, sx, w), BF16) for w in widths],
        compiler_params=_params(("arbitrary", "arbitrary"), VMEM_LIMIT),
        name="mixer_inputs",
    )(x, mod3, mod3, gain3, lw["w_in"], lw["na_q_g"], lw["na_k_g"], lw["cq_g"], lw["ckv_g"],
      lw["w_uq"], lw["w_uk"], lw["w_uv"], lw["q_g"], lw["k_g"], tabs[0], tabs[1], tabs[2])


def _pair_masks(rows):
    lane = lax.broadcasted_iota(jnp.int32, (rows, LANES), 1)
    return lane < NA_HEAD_DIM


def _ones_pad(v, low):
    one = jnp.ones_like(v)
    return jnp.where(low, v, one), jnp.where(low, one, v)


def _pair_finish(o_a, o_b, low):
    num = jnp.where(low, o_a, o_b)
    den = pltpu.roll(jnp.where(low, o_b, o_a), NA_HEAD_DIM, axis=1)
    return (num / den).astype(BF16)


def _na_kernel(q_ref, k_ref, v_ref, kc_ref, vc_ref, bias_ref, o_ref, va_ref, vb_ref, vca_ref, vcb_ref):
    qb = pl.program_id(2)
    n_rows = k_ref.shape[0] // GRID_W
    nwin = NA_WIN_ROWS * GRID_W

    @pl.when(qb == 0)
    def _():
        va_ref[...], vb_ref[...] = _ones_pad(v_ref[...], _pair_masks(v_ref.shape[0]))
        vca_ref[...], vcb_ref[...] = _ones_pad(vc_ref[...], _pair_masks(vc_ref.shape[0]))

    low = _pair_masks(GRID_W)
    kc = kc_ref[...]
    rows = range(NA_QROWS)
    starts, scores = [], []
    for j in rows:
        r = qb * NA_QROWS + j
        rs = jnp.clip(r - NA_WIN_ROWS // 2, 0, n_rows - NA_WIN_ROWS)
        start = pl.multiple_of(rs * GRID_W, GRID_W)
        q = q_ref[j * GRID_W:(j + 1) * GRID_W, :]
        zero = jnp.zeros_like(q)
        qs = jnp.concatenate([jnp.where(low, q, zero), jnp.where(low, zero, q)], axis=0)
        s = _dot_nt(qs, k_ref[pl.ds(start, nwin), :]) + bias_ref[r - rs].reshape(2 * GRID_W, nwin)
        starts.append(start)
        scores.append((s, _dot_nt(qs, kc)))
    probs = []
    for s, sc in scores:
        m = jnp.maximum(jnp.max(s, axis=-1, keepdims=True), jnp.max(sc, axis=-1, keepdims=True))
        probs.append((jnp.exp2(s - m).astype(BF16), jnp.exp2(sc - m).astype(BF16)))
    outs = []
    for start, (p, pc) in zip(starts, probs):
        o_a = _dot(p[:GRID_W], va_ref[pl.ds(start, nwin), :]) + _dot(pc[:GRID_W], vca_ref[...])
        o_b = _dot(p[GRID_W:], vb_ref[pl.ds(start, nwin), :]) + _dot(pc[GRID_W:], vcb_ref[...])
        outs.append(_pair_finish(o_a, o_b, low))
    o_ref[...] = jnp.concatenate(outs, axis=0)


def _na_attention(qa, ka, va, kc, vc, bias):
    bsz, s, _ = qa.shape
    nc = kc.shape[1]
    nq = NA_QROWS * GRID_W
    npairs = NA_WIDTH // LANES
    nwin = NA_WIN_ROWS * GRID_W
    whole = pl.BlockSpec((None, s, LANES), lambda p, b, qb: (b, 0, p))
    cspec = pl.BlockSpec((None, nc, LANES), lambda p, b, qb: (b, 0, p))
    return pl.pallas_call(
        _na_kernel,
        grid=(npairs, bsz, s // nq),
        in_specs=[pl.BlockSpec((None, nq, LANES), lambda p, b, qb: (b, qb, p)),
                  whole, whole, cspec, cspec,
                  pl.BlockSpec((NA_WIN_ROWS, 2, GRID_W, nwin), lambda p, b, qb: (0, p, 0, 0))],
        out_specs=pl.BlockSpec((None, nq, LANES), lambda p, b, qb: (b, qb, p)),
        out_shape=jax.ShapeDtypeStruct((bsz, s, NA_WIDTH), BF16),
        scratch_shapes=[pltpu.VMEM((s, LANES), BF16), pltpu.VMEM((s, LANES), BF16),
                        pltpu.VMEM((nc, LANES), BF16), pltpu.VMEM((nc, LANES), BF16)],
        compiler_params=_params(("arbitrary", "arbitrary", "arbitrary"), VMEM_LIMIT),
        name="na_attention",
    )(qa, ka, va, kc, vc, bias)


def _mla_kernel(q_ref, k_ref, v_ref, kc_ref, vc_ref, o_ref, va_ref, vb_ref, vca_ref, vcb_ref):
    @pl.when(pl.program_id(2) == 0)
    def _():
        va_ref[...], vb_ref[...] = _ones_pad(v_ref[...], _pair_masks(v_ref.shape[0]))
        vca_ref[...], vcb_ref[...] = _ones_pad(vc_ref[...], _pair_masks(vc_ref.shape[0]))

    scores = []
    for hh in range(2):
        sl = slice(hh * HEAD_PAD, (hh + 1) * HEAD_PAD)
        q = q_ref[:, sl]
        scores.append((_dot_nt(q, k_ref[:, sl]), _dot_nt(q, kc_ref[:, sl])))
    probs = []
    for s, sc in scores:
        m = jnp.maximum(jnp.max(s, axis=-1, keepdims=True), jnp.max(sc, axis=-1, keepdims=True))
        probs.append((jnp.exp2(s - m).astype(BF16), jnp.exp2(sc - m).astype(BF16)))
    outs = [_dot(p, vx_ref[...]) + _dot(pc, vcx_ref[...])
            for (p, pc), (vx_ref, vcx_ref) in zip(probs, ((va_ref, vca_ref), (vb_ref, vcb_ref)))]
    o_ref[...] = _pair_finish(outs[0], outs[1], _pair_masks(q_ref.shape[0]))


def _mla_attention(qb, kb, vb, kcb, vcb):
    bsz, s, _ = qb.shape
    nc = kcb.shape[1]
    tq = TQ_MLA
    npairs = MLA_WIDTH // LANES
    return pl.pallas_call(
        _mla_kernel,
        grid=(bsz, npairs, s // tq),
        in_specs=[pl.BlockSpec((None, tq, 2 * HEAD_PAD), lambda b, p, i: (b, i, p)),
                  pl.BlockSpec((None, s, 2 * HEAD_PAD), lambda b, p, i: (b, 0, p)),
                  pl.BlockSpec((None, s, LANES), lambda b, p, i: (b, 0, p)),
                  pl.BlockSpec((None, nc, 2 * HEAD_PAD), lambda b, p, i: (b, 0, p)),
                  pl.BlockSpec((None, nc, LANES), lambda b, p, i: (b, 0, p))],
        out_specs=pl.BlockSpec((None, tq, LANES), lambda b, p, i: (b, i, p)),
        out_shape=jax.ShapeDtypeStruct((bsz, s, MLA_WIDTH), BF16),
        scratch_shapes=[pltpu.VMEM((s, LANES), BF16), pltpu.VMEM((s, LANES), BF16),
                        pltpu.VMEM((nc, LANES), BF16), pltpu.VMEM((nc, LANES), BF16)],
        compiler_params=_params(("arbitrary", "arbitrary", "arbitrary"), VMEM_LIMIT),
        name="mla_attention",
    )(qb, kb, vb, kcb, vcb)


def _softmax2_pv(s, v):
    p = jnp.exp2(s - jnp.max(s, axis=-1, keepdims=True))
    return _dot(p.astype(BF16), v) / jnp.sum(p, axis=-1, keepdims=True)


def _ctx_kernel(qa_ref, ka_ref, va_ref, qb_ref, kb_ref, vb_ref, oa_ref, ob_ref):
    n = qa_ref.shape[0]
    low = _pair_masks(n)
    qa = qa_ref[...]
    zero = jnp.zeros_like(qa)
    ka = ka_ref[...]
    va = va_ref[...]
    o_lo = _softmax2_pv(_dot_nt(jnp.where(low, qa, zero), ka), va)
    o_hi = _softmax2_pv(_dot_nt(jnp.where(low, zero, qa), ka), va)
    oa_ref[...] = jnp.where(low, o_lo, o_hi).astype(BF16)
    vb = vb_ref[...]
    outs = []
    for hh in range(2):
        sl = slice(hh * HEAD_PAD, (hh + 1) * HEAD_PAD)
        outs.append(_softmax2_pv(_dot_nt(qb_ref[:, sl], kb_ref[:, sl]), vb))
    ob_ref[...] = jnp.where(low, outs[0], outs[1]).astype(BF16)


def _ctx_attention(qa, ka, va, qb, kb, vb):
    bsz, n, _ = qa.shape
    npairs = NA_WIDTH // LANES
    narrow = pl.BlockSpec((None, n, LANES), lambda b, p: (b, 0, p))
    wide = pl.BlockSpec((None, n, 2 * HEAD_PAD), lambda b, p: (b, 0, p))
    return pl.pallas_call(
        _ctx_kernel,
        grid=(bsz, npairs),
        in_specs=[narrow, narrow, narrow, wide, wide, narrow],
        out_specs=[narrow, narrow],
        out_shape=[jax.ShapeDtypeStruct((bsz, n, NA_WIDTH), BF16),
                   jax.ShapeDtypeStruct((bsz, n, MLA_WIDTH), BF16)],
        compiler_params=_params(("arbitrary", "arbitrary")),
        name="ctx_attention",
    )(qa, ka, va, qb, kb, vb)


def _merge_kernel(x_ref, oa_ref, ob_ref, ga_ref, gb_ref, woa_ref, wob_ref, wout_ref, gm_ref,
                  g_ref, sh_ref, sc_ref, wr_ref, xo_ref, h_ref, lg_ref):
    y = (ga_ref[...].astype(F32) * _dot(oa_ref[...], woa_ref[...])
         + gb_ref[...].astype(F32) * _dot(ob_ref[...], wob_ref[...]))
    xn = x_ref[...] + gm_ref[...] * _dot(y.astype(BF16), wout_ref[...])
    xo_ref[...] = xn
    h = _rms(xn, D_MODEL) * g_ref[...]
    h = h * (1.0 + sc_ref[...]) + sh_ref[...]
    _store_slabs(h_ref, _pack_rows(h))
    lg_ref[...] = _dot_nt(wr_ref[...], h.astype(BF16))


def _merge(x, oa, ob, ga, gb, mod3, mod_row, gain3, l, lw):
    bsz, sx, _ = x.shape
    tm = min(TM_MERGE, sx)
    nt = sx // tm
    row = (lambda b: b) if mod_row is None else (lambda b: mod_row)

    def full(shape):
        return pl.BlockSpec(shape, lambda b, i: (0,) * len(shape))

    def tok(width):
        return pl.BlockSpec((None, tm, width), lambda b, i: (b, i, 0))

    def mod(chunk):
        return pl.BlockSpec((None, 1, D_MODEL), lambda b, i: (row(b), 0, chunk))

    return pl.pallas_call(
        _merge_kernel,
        grid=(bsz, nt),
        in_specs=[tok(D_MODEL), tok(NA_WIDTH), tok(MLA_WIDTH), tok(D_MODEL), tok(D_MODEL),
                  full((NA_WIDTH, D_MODEL)), full((MLA_WIDTH, D_MODEL)), full((D_MODEL, D_MODEL)),
                  mod(2),
                  pl.BlockSpec((None, 1, D_MODEL), lambda b, i: (l, 0, 0)),
                  mod(3), mod(4),
                  full((N_EXPERTS, D_MODEL))],
        out_specs=[tok(D_MODEL),
                   pl.BlockSpec((tm * PACK_SUB, LANES), lambda b, i: (b * nt + i, 0)),
                   pl.BlockSpec((N_EXPERTS, tm), lambda b, i: (0, b * nt + i))],
        out_shape=[jax.ShapeDtypeStruct((bsz, sx, D_MODEL), F32),
                   jax.ShapeDtypeStruct((bsz * sx * PACK_SUB, LANES), jnp.uint32),
                   jax.ShapeDtypeStruct((N_EXPERTS, bsz * sx), F32)],
        compiler_params=_params(("arbitrary", "arbitrary"), VMEM_LIMIT),
        name="merge_branches",
    )(x, oa, ob, ga, gb, lw["w_oa"], lw["w_ob"], lw["w_out"], mod3, gain3, mod3, mod3,
      lw["w_router_t"])


def _route_kernel(lg_ref, bias_ref, tri_ref, eid_ref, gw_ref, rank_ref, cnt_ref, run_ref):
    @pl.when(pl.program_id(0) == 0)
    def _():
        run_ref[...] = jnp.zeros_like(run_ref)

    tr = lg_ref.shape[1]
    s = _sigmoid(lg_ref[...])
    biased = s + bias_ref[...]
    sub = lax.broadcasted_iota(jnp.int32, (EXPERTS_PER_GROUP, tr), 0).astype(F32)
    none = float(EXPERTS_PER_GROUP)
    best = jnp.full((1, tr), -jnp.inf, F32)
    e1 = jnp.zeros((1, tr), F32)
    e2 = jnp.zeros((1, tr), F32)
    for g in range(N_GROUPS):
        bg = biased[g * EXPERTS_PER_GROUP:(g + 1) * EXPERTS_PER_GROUP]
        m1 = jnp.max(bg, axis=0, keepdims=True)
        i1 = jnp.min(jnp.where(bg == m1, sub, none), axis=0, keepdims=True)
        rest = jnp.where(sub == i1, -jnp.inf, bg)
        m2 = jnp.max(rest, axis=0, keepdims=True)
        i2 = jnp.min(jnp.where(rest == m2, sub, none), axis=0, keepdims=True)
        score = m1 + m2
        better = score > best
        best = jnp.where(better, score, best)
        e1 = jnp.where(better, g * EXPERTS_PER_GROUP + i1, e1)
        e2 = jnp.where(better, g * EXPERTS_PER_GROUP + i2, e2)

    rowid = lax.broadcasted_iota(jnp.int32, (N_EXPERTS, tr), 0).astype(F32)
    is1 = rowid == e1
    is2 = rowid == e2
    w1 = jnp.sum(jnp.where(is1, s, 0.0), axis=0, keepdims=True)
    w2 = jnp.sum(jnp.where(is2, s, 0.0), axis=0, keepdims=True)
    tot = w1 + w2
    gw_ref[0:1, :] = w1 / tot
    gw_ref[1:2, :] = w2 / tot
    eid_ref[0:1, :] = e1.astype(jnp.int32)
    eid_ref[1:2, :] = e2.astype(jnp.int32)

    onehot = jnp.where(is1 | is2, 1.0, 0.0)
    before = _dot(onehot.astype(BF16), tri_ref[...]) + run_ref[...]
    rank_ref[0:1, :] = jnp.sum(jnp.where(is1, before, 0.0), axis=0, keepdims=True).astype(jnp.int32)
    rank_ref[1:2, :] = jnp.sum(jnp.where(is2, before, 0.0), axis=0, keepdims=True).astype(jnp.int32)
    run_ref[...] = run_ref[...] + jnp.sum(onehot, axis=1, keepdims=True)
    cnt_ref[...] = run_ref[...]


def _route(logits, bias_col, tri):
    t = logits.shape[1]
    tr = TR_ROUTE
    pair = pl.BlockSpec((TOP_K, tr), lambda i: (0, i))
    return pl.pallas_call(
        _route_kernel,
        grid=(t // tr,),
        in_specs=[pl.BlockSpec((N_EXPERTS, tr), lambda i: (0, i)),
                  pl.BlockSpec((N_EXPERTS, 1), lambda i: (0, 0)),
                  pl.BlockSpec((tr, tr), lambda i: (0, 0))],
        out_specs=[pair, pair, pair, pl.BlockSpec((N_EXPERTS, 1), lambda i: (0, 0))],
        out_shape=[jax.ShapeDtypeStruct((TOP_K, t), jnp.int32),
                   jax.ShapeDtypeStruct((TOP_K, t), F32),
                   jax.ShapeDtypeStruct((TOP_K, t), jnp.int32),
                   jax.ShapeDtypeStruct((N_EXPERTS, 1), F32)],
        scratch_shapes=[pltpu.VMEM((N_EXPERTS, 1), F32)],
        compiler_params=_params(("arbitrary",)),
        name="route",
    )(logits, bias_col, tri)


def _dispatch_kernel(dest_ref, h_ref, slots_in_ref, slots_ref, sem, *, n_tok):
    del slots_in_ref
    tm = h_ref.shape[0] // PACK_SUB
    base = pl.program_id(0) * tm

    def copy(r, k):
        d = dest_ref[k * n_tok + base + r]
        return pltpu.make_async_copy(h_ref.at[pl.ds(pl.multiple_of(r * PACK_SUB, PACK_SUB), PACK_SUB)],
                                     slots_ref.at[pl.ds(pl.multiple_of(d * PACK_SUB, PACK_SUB), PACK_SUB)], sem)

    def issue(r, carry):
        for k in range(TOP_K):
            copy(r, k).start(priority=k)
        return carry

    def drain(r, carry):
        for k in range(TOP_K):
            copy(r, k).wait()
        return carry

    lax.fori_loop(0, tm, issue, 0, unroll=MOVE_UNROLL)
    lax.fori_loop(0, tm, drain, 0, unroll=MOVE_UNROLL)


def _dispatch(h, dest_flat, slots):
    n_tok = h.shape[0] // PACK_SUB
    tm = TM_MOVE
    return pl.pallas_call(
        functools.partial(_dispatch_kernel, n_tok=n_tok),
        grid_spec=pltpu.PrefetchScalarGridSpec(
            num_scalar_prefetch=1,
            grid=(n_tok // tm,),
            in_specs=[pl.BlockSpec((tm * PACK_SUB, LANES), lambda i, dest: (i, 0)),
                      pl.BlockSpec(memory_space=pl.ANY)],
            out_specs=pl.BlockSpec(memory_space=pl.ANY),
            scratch_shapes=[pltpu.SemaphoreType.DMA],
        ),
        out_shape=jax.ShapeDtypeStruct(slots.shape, slots.dtype),
        input_output_aliases={2: 0},
        compiler_params=_params(("arbitrary",)),
        name="moe_dispatch",
    )(dest_flat, h, slots)


def _expert_kernel(be_ref, nused_ref, nxt_ref, x_ref, w1_hbm, w3_hbm, w2_hbm, y_ref,
                   st1, st3, st2, w1b, w3b, w2b, sem, *, layer):
    i = pl.program_id(0)
    used = i < nused_ref[0]
    stages = ((w1_hbm, st1, w1b), (w3_hbm, st3, w3b), (w2_hbm, st2, w2b))

    def fetch(e):
        return [pltpu.make_async_copy(w_hbm.at[layer, e], st, sem.at[j]) for j, (w_hbm, st, _) in enumerate(stages)]

    @pl.when(used)
    def _():
        e = be_ref[i]
        prev = be_ref[jnp.maximum(i - 1, 0)]

        @pl.when(i == 0)
        def _():
            for c in fetch(e):
                c.start()

        @pl.when((i == 0) | (e != prev))
        def _():
            for c, (_, st, wb) in zip(fetch(e), stages):
                c.wait()
                wb[...] = st[...].astype(BF16)
            nxt = nxt_ref[e]

            @pl.when(nxt >= 0)
            def _():
                for c in fetch(nxt):
                    c.start()

        bm = x_ref.shape[0] // PACK_SUB
        x_lo, x_hi = _unpack_rows(_load_slabs(x_ref, bm))
        xb = jnp.concatenate([x_lo.astype(BF16), x_hi.astype(BF16)], axis=1)
        a = _dot(xb, w1b[...])
        b = _dot(xb, w3b[...])
        _store_slabs(y_ref, _pack_rows(_dot((a * _sigmoid(a) * b).astype(BF16), w2b[...])))

    @pl.when(jnp.logical_not(used))
    def _():
        y_ref[...] = jnp.zeros_like(y_ref)


def _experts(slots, block_expert, nused, next_expert, w1, w3, w2, l):
    n_slots = slots.shape[0] // PACK_SUB
    bm = BM_EXP
    de = w1.shape[-1]
    rows = pl.BlockSpec((bm * PACK_SUB, LANES), lambda i, be, nu, nx: (i, 0))
    hbm = pl.BlockSpec(memory_space=pl.ANY)
    return pl.pallas_call(
        functools.partial(_expert_kernel, layer=l),
        grid_spec=pltpu.PrefetchScalarGridSpec(
            num_scalar_prefetch=3,
            grid=(n_slots // bm,),
            in_specs=[rows, hbm, hbm, hbm],
            out_specs=rows,
            scratch_shapes=[pltpu.VMEM((D_MODEL, de), F32), pltpu.VMEM((D_MODEL, de), F32),
                            pltpu.VMEM((de, D_MODEL), F32),
                            pltpu.VMEM((D_MODEL, de), BF16), pltpu.VMEM((D_MODEL, de), BF16),
                            pltpu.VMEM((de, D_MODEL), BF16),
                            pltpu.SemaphoreType.DMA((3,))],
        ),
        out_shape=jax.ShapeDtypeStruct(slots.shape, slots.dtype),
        compiler_params=_params(("arbitrary",), VMEM_LIMIT),
        name="moe_experts",
    )(block_expert, nused, next_expert, slots, w1, w3, w2)


def _combine_kernel(dest_ref, x_ref, gw_ref, gf_ref, y_hbm, o_ref, ybuf, sem, *, n_tok):
    tm = x_ref.shape[0]
    i = pl.program_id(0)
    nsteps = pl.num_programs(0)

    def copy(step, r, k):
        slot = step % 2
        d = dest_ref[k * n_tok + step * tm + r]
        return pltpu.make_async_copy(y_hbm.at[pl.ds(pl.multiple_of(d * PACK_SUB, PACK_SUB), PACK_SUB)],
                                     ybuf.at[slot, k, pl.ds(pl.multiple_of(r * PACK_SUB, PACK_SUB), PACK_SUB)],
                                     sem.at[slot])

    def issue(step):
        def body(r, carry):
            for k in range(TOP_K):
                copy(step, r, k).start(priority=k)
            return carry
        lax.fori_loop(0, tm, body, 0, unroll=MOVE_UNROLL)

    @pl.when(i == 0)
    def _():
        issue(i)

    @pl.when(i + 1 < nsteps)
    def _():
        issue(i + 1)

    def drain(r, carry):
        for k in range(TOP_K):
            copy(i, r, k).wait()
        return carry
    lax.fori_loop(0, tm, drain, 0, unroll=MOVE_UNROLL)

    slot = i % 2
    gw = gw_ref[...]
    halves = [_unpack_rows(_load_slabs(ybuf.at[slot, k], tm)) for k in range(TOP_K)]
    gf = gf_ref[...]
    for part in range(2):
        cols = slice(part * PACK_W, (part + 1) * PACK_W)
        y = gw[:, 0:1] * halves[0][part] + gw[:, 1:2] * halves[1][part]
        o_ref[:, cols] = x_ref[:, cols] + gf[:, cols] * y


def _combine(x, y_slots, dest_flat, gw_t, mod3, mod_row):
    bsz, sx, _ = x.shape
    n_tok = bsz * sx
    tm = TM_MOVE
    nt = sx // tm
    row = (lambda i: i // nt) if mod_row is None else (lambda i: mod_row)
    out = pl.pallas_call(
        functools.partial(_combine_kernel, n_tok=n_tok),
        grid_spec=pltpu.PrefetchScalarGridSpec(
            num_scalar_prefetch=1,
            grid=(n_tok // tm,),
            in_specs=[pl.BlockSpec((tm, D_MODEL), lambda i, dest: (i, 0)),
                      pl.BlockSpec((tm, TOP_K), lambda i, dest: (i, 0)),
                      pl.BlockSpec((None, 1, D_MODEL), lambda i, dest: (row(i), 0, 5)),
                      pl.BlockSpec(memory_space=pl.ANY)],
            out_specs=pl.BlockSpec((tm, D_MODEL), lambda i, dest: (i, 0)),
            scratch_shapes=[pltpu.VMEM((2, TOP_K, tm * PACK_SUB, LANES), jnp.uint32),
                            pltpu.SemaphoreType.DMA((2,))],
        ),
        out_shape=jax.ShapeDtypeStruct((n_tok, D_MODEL), F32),
        compiler_params=_params(("arbitrary",)),
        name="moe_combine",
    )(dest_flat, x.reshape(n_tok, D_MODEL), gw_t, mod3, y_slots)
    return out.reshape(bsz, sx, D_MODEL)


def _layer_weights(l, w_in, na_q_g, na_k_g, mla_cq_g, w_uq, mla_ckv_g, w_ukv, mla_q_g, mla_k_g,
                   w_oa, w_ob, w_out, w_router):
    wi = w_in[l]
    c0 = C_NA + MLA_Q_LORA + MLA_KV_LORA
    kr_cols = jnp.zeros((D_MODEL, HEAD_PAD), F32).at[:, MLA_NOPE:MLA_QK_DIM].set(wi[:, c0:c0 + MLA_ROPE])
    w_in_arr = jnp.concatenate([wi[:, :c0], kr_cols, wi[:, c0 + MLA_ROPE:]], axis=1).astype(BF16)

    def pad_heads(w, width):
        w = w.reshape(w.shape[0], MLA_HEADS, width)
        return jnp.pad(w, ((0, 0), (0, 0), (0, HEAD_PAD - width))).reshape(w.shape[0], MLA_QK_PAD)

    ukv = w_ukv[l].reshape(MLA_KV_LORA, MLA_HEADS, MLA_NOPE + MLA_V_DIM)
    return {
        "w_in": w_in_arr,
        "na_q_g": jnp.tile(na_q_g[l], NA_HEADS)[None, :],
        "na_k_g": jnp.tile(na_k_g[l], NA_HEADS)[None, :],
        "cq_g": mla_cq_g[l][None, :],
        "ckv_g": mla_ckv_g[l][None, :],
        "w_uq": pad_heads(w_uq[l], MLA_QK_DIM).astype(BF16),
        "w_uk": pad_heads(ukv[:, :, :MLA_NOPE].reshape(MLA_KV_LORA, -1), MLA_NOPE).astype(BF16),
        "w_uv": ukv[:, :, MLA_NOPE:].reshape(MLA_KV_LORA, MLA_WIDTH).astype(BF16),
        "q_g": jnp.pad(mla_q_g[l], (0, HEAD_PAD - MLA_QK_DIM))[None, :],
        "k_g": jnp.pad(mla_k_g[l], (0, HEAD_PAD - MLA_QK_DIM))[None, :],
        "w_oa": w_oa[l].astype(BF16),
        "w_ob": w_ob[l].astype(BF16),
        "w_out": w_out[l].astype(BF16),
        "w_router_t": w_router.T.astype(BF16),
    }


def _rope_tables(s):
    half = MLA_ROPE // 4
    pos = jnp.arange(s, dtype=jnp.int32)
    inv = ROPE_BASE ** (-jnp.arange(half, dtype=F32) / half)
    ang_r = (pos // GRID_W).astype(F32)[:, None] * inv[None, :]
    ang_c = (pos % GRID_W).astype(F32)[:, None] * inv[None, :]
    zeros = jnp.zeros((s, half), F32)
    lead = jnp.zeros((s, MLA_NOPE), F32)
    tail = jnp.zeros((s, HEAD_PAD - MLA_QK_DIM), F32)
    cos = jnp.concatenate([lead + 1.0, jnp.cos(ang_r), jnp.cos(ang_r), jnp.cos(ang_c), jnp.cos(ang_c), tail + 1.0], 1)
    s1 = jnp.concatenate([lead, -jnp.sin(ang_r), zeros, -jnp.sin(ang_c), zeros, tail], 1)
    s2 = jnp.concatenate([lead, zeros, jnp.sin(ang_r), zeros, jnp.sin(ang_c), tail], 1)
    return cos, s1, s2


def _na_bias_tables(rpb):
    d = jnp.arange(NA_WIN_ROWS)[:, None]
    i = jnp.arange(NA_WIN_ROWS)[None, :]
    qc = jnp.arange(GRID_W)[:, None]
    kc = jnp.arange(GRID_W)[None, :]
    cs = jnp.clip(qc - NA_WIN_COLS // 2, 0, GRID_W - NA_WIN_COLS)
    cvalid = (kc >= cs) & (kc < cs + NA_WIN_COLS)
    ohr = jax.nn.one_hot(i - d + NA_WIN_ROWS - 1, 2 * NA_WIN_ROWS - 1, dtype=F32)
    ohc = jax.nn.one_hot(jnp.clip(kc - qc + NA_WIN_COLS - 1, 0, 2 * NA_WIN_COLS - 2), 2 * NA_WIN_COLS - 1, dtype=F32)
    t = jnp.einsum("dia,hab,qkb->dhqik", ohr, rpb.astype(F32), ohc, precision=lax.Precision.HIGHEST)
    t = jnp.where(cvalid[None, None, :, None, :], t * LOG2E, NEG_INF)
    return t.reshape(NA_WIN_ROWS, rpb.shape[0], GRID_W, NA_WIN_ROWS * GRID_W)


def _slot_tables(eid, rank, counts):
    bm = BM_EXP
    counts = counts[:, 0].astype(jnp.int32)
    padded = ((counts + bm - 1) // bm) * bm
    pad_end = jnp.cumsum(padded)
    pad_start = pad_end - padded
    experts = jnp.arange(N_EXPERTS, dtype=jnp.int32)
    dest = rank + jnp.sum(jnp.where(eid[..., None] == experts, pad_start, 0), axis=-1)
    m = eid.shape[1] * TOP_K
    n_blocks = -(-m // bm) + N_EXPERTS
    blk = jnp.arange(n_blocks, dtype=jnp.int32) * bm
    block_expert = jnp.minimum(jnp.sum(pad_end[None, :] <= blk[:, None], axis=1), N_EXPERTS - 1).astype(jnp.int32)
    nused = (pad_end[-1:] // bm).astype(jnp.int32)
    later = lax.cummin(jnp.where(counts > 0, experts, N_EXPERTS), axis=0, reverse=True)
    next_expert = jnp.concatenate([later[1:], jnp.full((1,), N_EXPERTS, jnp.int32)])
    next_expert = jnp.where(next_expert < N_EXPERTS, next_expert, -1).astype(jnp.int32)
    return dest.astype(jnp.int32), block_expert, nused, next_expert, n_blocks * bm


def kernel(x, c, ctx, c_ctx, w_ada, b_ada, norm_mix_g, norm_ffn_g, w_in, na_q_g, na_k_g, na_rpb,
           mla_cq_g, w_uq, mla_ckv_g, w_ukv, mla_q_g, mla_k_g, w_oa, w_ob, w_out,
           w_router, router_bias, w1, w3, w2):
    bsz, s, d = x.shape
    n_ctx = ctx.shape[1]
    ctx_row = bsz
    pad_rows = -(bsz + 1) % 8
    cvec = jnp.concatenate([c, c_ctx[None, :], jnp.zeros((pad_rows, d), F32)], axis=0)
    b_ada3 = b_ada[:, None, :]
    mix_g3 = norm_mix_g[:, None, :]
    ffn_g3 = norm_ffn_g[:, None, :]
    tabs_x = _rope_tables(s)
    tabs_c = (jnp.ones((n_ctx, LANES), F32), jnp.zeros((n_ctx, LANES), F32), jnp.zeros((n_ctx, LANES), F32))
    tri = (jnp.arange(TR_ROUTE)[:, None] < jnp.arange(TR_ROUTE)[None, :]).astype(BF16)
    bias_col = router_bias.astype(F32)[:, None]
    n_x = bsz * s

    xc = ctx
    for l in range(DEPTH):
        last = l == DEPTH - 1
        lw = _layer_weights(l, w_in, na_q_g, na_k_g, mla_cq_g, w_uq, mla_ckv_g, w_ukv, mla_q_g, mla_k_g,
                            w_oa, w_ob, w_out, w_router)
        mod3 = _ada(cvec, w_ada, b_ada3, l)[:, None, :]
        qa, ka, va, qb, kb, vb, ga, gb = _proj(x, mod3, None, mix_g3, l, lw, tabs_x)
        qa_c, ka_c, va_c, qb_c, kb_c, vb_c, ga_c, gb_c = _proj(xc, mod3, ctx_row, mix_g3, l, lw, tabs_c)
        oa = _na_attention(qa, ka, va, ka_c, va_c, _na_bias_tables(na_rpb[l]))
        ob = _mla_attention(qb, kb, vb, kb_c, vb_c)
        x, h_x, lg = _merge(x, oa, ob, ga, gb, mod3, None, ffn_g3, l, lw)
        if not last:
            oa_c, ob_c = _ctx_attention(qa_c, ka_c, va_c, qb_c, kb_c, vb_c)
            xc, h_c, lg_c = _merge(xc, oa_c, ob_c, ga_c, gb_c, mod3, ctx_row, ffn_g3, l, lw)
            lg = jnp.concatenate([lg, lg_c], axis=1)

        eid, gw, rank, counts = _route(lg, bias_col, tri)
        dest, block_expert, nused, next_expert, n_slots = _slot_tables(eid, rank, counts)
        gw_t = gw.T
        slots = jnp.zeros((n_slots * PACK_SUB, LANES), jnp.uint32)
        slots = _dispatch(h_x, dest[:, :n_x].reshape(-1), slots)
        if not last:
            slots = _dispatch(h_c, dest[:, n_x:].reshape(-1), slots)
        y_slots = _experts(slots, block_expert, nused, next_expert, w1, w3, w2, l)
        x = _combine(x, y_slots, dest[:, :n_x].reshape(-1), gw_t[:n_x], mod3, None)
        if not last:
            xc = _combine(xc, y_slots, dest[:, n_x:].reshape(-1), gw_t[n_x:], mod3, ctx_row)
    return x
```

```python
import functools

import jax
import jax.numpy as jnp
from jax import lax
from jax.experimental import pallas as pl
from jax.experimental.pallas import tpu as pltpu

F32 = jnp.float32
BF16 = jnp.bfloat16

D_MODEL = 1024
DEPTH = 2
GRID_W = 64
N_MOD = 6

NA_HEADS = 8
NA_HEAD_DIM = 64
NA_WIN_ROWS = 8
NA_WIN_COLS = 16
NA_WIDTH = NA_HEADS * NA_HEAD_DIM

MLA_HEADS = 8
MLA_NOPE = 64
MLA_ROPE = 32
MLA_QK_DIM = MLA_NOPE + MLA_ROPE
MLA_V_DIM = 64
MLA_Q_LORA = 384
MLA_KV_LORA = 256
MLA_WIDTH = MLA_HEADS * MLA_V_DIM
ROPE_BASE = 10000.0

N_EXPERTS = 64
EXPERTS_PER_GROUP = 8
N_GROUPS = N_EXPERTS // EXPERTS_PER_GROUP
TOP_K = 2

RMS_EPS = 1e-6
NEG_INF = -1e30
LOG2E = 1.4426950408889634

LANES = 128
HEAD_PAD = LANES
MLA_QK_PAD = MLA_HEADS * HEAD_PAD

C_NA = 3 * NA_WIDTH
C_CQ = C_NA + MLA_Q_LORA
C_CKV = C_CQ + MLA_KV_LORA
C_KR = C_CKV + HEAD_PAD
C_ALL = C_KR + 2 * D_MODEL

TM_PROJ = 512
TM_MERGE = 512
TQ_MLA = 512
NA_QROWS = 32
TR_ROUTE = 1024
BM_EXP = 256
PACK_W = D_MODEL // 2
PACK_SUB = PACK_W // LANES
TM_MOVE = 512
MOVE_UNROLL = 8
VMEM_LIMIT = 56 * 1024 * 1024


def _sigmoid(v):
    return 1.0 / (1.0 + jnp.exp(-v))


def _rms(v, n):
    return v * lax.rsqrt(jnp.sum(v * v, axis=-1, keepdims=True) * (1.0 / n) + RMS_EPS)


def _pack_rows(v):
    lo = pltpu.bitcast(v[:, :PACK_W].astype(BF16).astype(F32), jnp.uint32)
    hi = pltpu.bitcast(v[:, PACK_W:].astype(BF16).astype(F32), jnp.uint32)
    return (lo >> 16) | (hi & jnp.uint32(0xFFFF0000))


def _unpack_rows(w):
    return pltpu.bitcast(w << 16, F32), pltpu.bitcast(w & jnp.uint32(0xFFFF0000), F32)


def _store_slabs(ref, w):
    for c in range(PACK_SUB):
        ref[pl.ds(c, w.shape[0], stride=PACK_SUB), :] = w[:, c * LANES:(c + 1) * LANES]


def _load_slabs(ref, m):
    return jnp.concatenate([ref[pl.ds(c, m, stride=PACK_SUB), :] for c in range(PACK_SUB)], axis=1)


def _split_bf16(a):
    hi = a.astype(BF16)
    lo = (a - hi.astype(F32)).astype(BF16)
    return hi, lo


def _dot(a, b):
    return jnp.dot(a, b, preferred_element_type=F32)


def _dot_nt(a, b):
    return lax.dot_general(a, b, (((1,), (1,)), ((), ())), preferred_element_type=F32)


def _params(sem, vmem=None):
    return pltpu.CompilerParams(dimension_semantics=sem, vmem_limit_bytes=vmem)


def _ada_kernel(c_ref, w_ref, b_ref, o_ref):
    cv = c_ref[...]
    s = cv * _sigmoid(cv)
    s_hi, s_lo = _split_bf16(s)
    w_hi, w_lo = _split_bf16(w_ref[...])
    o_ref[...] = _dot(s_hi, w_hi) + _dot(s_lo, w_hi) + _dot(s_hi, w_lo) + b_ref[...]


def _ada(cvec, w_ada, b_ada3, l):
    rows = cvec.shape[0]
    n = N_MOD * D_MODEL
    tn = 512
    return pl.pallas_call(
        _ada_kernel,
        grid=(n // tn,),
        in_specs=[
            pl.BlockSpec((rows, D_MODEL), lambda j: (0, 0)),
            pl.BlockSpec((None, D_MODEL, tn), lambda j: (l, 0, j)),
            pl.BlockSpec((None, 1, tn), lambda j: (l, 0, j)),
        ],
        out_specs=pl.BlockSpec((rows, tn), lambda j: (0, j)),
        out_shape=jax.ShapeDtypeStruct((rows, n), F32),
        compiler_params=_params(("arbitrary",)),
        name="ada_mod",
    )(cvec, w_ada, b_ada3)


def _rope(v, cos, s1, s2):
    return v * cos + pltpu.roll(v, LANES - 8, axis=1) * s1 + pltpu.roll(v, 8, axis=1) * s2


def _proj_kernel(x_ref, sh_ref, sc_ref, g_ref, win_ref, naqg_ref, nakg_ref, cqg_ref, ckvg_ref,
                 wuq_ref, wuk_ref, wuv_ref, qg_ref, kg_ref, cos_ref, s1_ref, s2_ref,
                 qa_ref, ka_ref, va_ref, qb_ref, kb_ref, vb_ref, ga_ref, gb_ref):
    x = x_ref[...]
    h = _rms(x, D_MODEL) * g_ref[...]
    h = h * (1.0 + sc_ref[...]) + sh_ref[...]
    hb = h.astype(BF16)

    low = _pair_masks(x.shape[0])

    def headnorm(z, g):
        tiles = []
        for c in range(NA_WIDTH // LANES):
            zc = z[:, c * LANES:(c + 1) * LANES]
            sq = zc * zc
            lo = jnp.sum(jnp.where(low, sq, 0.0), axis=-1, keepdims=True)
            hi = jnp.sum(jnp.where(low, 0.0, sq), axis=-1, keepdims=True)
            tiles.append(zc * lax.rsqrt(jnp.where(low, lo, hi) * (1.0 / NA_HEAD_DIM) + RMS_EPS))
        return jnp.concatenate(tiles, axis=1) * g

    zq = _dot(hb, win_ref[:, 0:NA_WIDTH])
    qa_ref[...] = (headnorm(zq, naqg_ref[...]) * (NA_HEAD_DIM ** -0.5 * LOG2E)).astype(BF16)
    zk = _dot(hb, win_ref[:, NA_WIDTH:2 * NA_WIDTH])
    ka_ref[...] = headnorm(zk, nakg_ref[...]).astype(BF16)
    va_ref[...] = _dot(hb, win_ref[:, 2 * NA_WIDTH:C_NA]).astype(BF16)

    cos = cos_ref[...]
    s1 = s1_ref[...]
    s2 = s2_ref[...]

    cq = _rms(_dot(hb, win_ref[:, C_NA:C_CQ]), MLA_Q_LORA) * cqg_ref[...]
    q = _dot(cq.astype(BF16), wuq_ref[...])
    qg = qg_ref[...]
    for hh in range(MLA_HEADS):
        sl = slice(hh * HEAD_PAD, (hh + 1) * HEAD_PAD)
        qn = _rms(q[:, sl], MLA_QK_DIM) * qg
        qb_ref[:, sl] = (_rope(qn, cos, s1, s2) * (MLA_QK_DIM ** -0.5 * LOG2E)).astype(BF16)

    ckv = (_rms(_dot(hb, win_ref[:, C_CQ:C_CKV]), MLA_KV_LORA) * ckvg_ref[...]).astype(BF16)
    kn = _dot(ckv, wuk_ref[...])
    kr = _dot(hb, win_ref[:, C_CKV:C_KR])
    kg = kg_ref[...]
    for hh in range(MLA_HEADS):
        sl = slice(hh * HEAD_PAD, (hh + 1) * HEAD_PAD)
        kh = _rms(kn[:, sl] + kr, MLA_QK_DIM) * kg
        kb_ref[:, sl] = _rope(kh, cos, s1, s2).astype(BF16)
    vb_ref[...] = _dot(ckv, wuv_ref[...]).astype(BF16)

    ga_ref[...] = _sigmoid(_dot(hb, win_ref[:, C_KR:C_KR + D_MODEL])).astype(BF16)
    gb_ref[...] = _sigmoid(_dot(hb, win_ref[:, C_KR + D_MODEL:C_ALL])).astype(BF16)


def _proj(x, mod3, mod_row, gain3, l, lw, tabs):
    bsz, sx, _ = x.shape
    tm = min(TM_PROJ, sx)
    nt = sx // tm
    row = (lambda b: b) if mod_row is None else (lambda b: mod_row)

    def full(shape):
        return pl.BlockSpec(shape, lambda b, i: (0,) * len(shape))

    def tok(width):
        return pl.BlockSpec((None, tm, width), lambda b, i: (b, i, 0))

    def tab():
        return pl.BlockSpec((tm, LANES), lambda b, i: (i, 0))

    in_specs = [
        tok(D_MODEL),
        pl.BlockSpec((None, 1, D_MODEL), lambda b, i: (row(b), 0, 0)),
        pl.BlockSpec((None, 1, D_MODEL), lambda b, i: (row(b), 0, 1)),
        pl.BlockSpec((None, 1, D_MODEL), lambda b, i: (l, 0, 0)),
        full((D_MODEL, C_ALL)),
        full((1, NA_WIDTH)), full((1, NA_WIDTH)),
        full((1, MLA_Q_LORA)), full((1, MLA_KV_LORA)),
        full((MLA_Q_LORA, MLA_QK_PAD)), full((MLA_KV_LORA, MLA_QK_PAD)), full((MLA_KV_LORA, MLA_WIDTH)),
        full((1, HEAD_PAD)), full((1, HEAD_PAD)),
        tab(), tab(), tab(),
    ]
    widths = (NA_WIDTH, NA_WIDTH, NA_WIDTH, MLA_QK_PAD, MLA_QK_PAD, MLA_WIDTH, D_MODEL, D_MODEL)
    return pl.pallas_call(
        _proj_kernel,
        grid=(bsz, nt),
        in_specs=in_specs,
        out_specs=[tok(w) for w in widths],
        out_shape=[jax.ShapeDtypeStruct((bsz, sx, w), BF16) for w in widths],
        compiler_params=_params(("arbitrary", "arbitrary"), VMEM_LIMIT),
        name="mixer_inputs",
    )(x, mod3, mod3, gain3, lw["w_in"], lw["na_q_g"], lw["na_k_g"], lw["cq_g"], lw["ckv_g"],
      lw["w_uq"], lw["w_uk"], lw["w_uv"], lw["q_g"], lw["k_g"], tabs[0], tabs[1], tabs[2])


def _pair_masks(rows):
    lane = lax.broadcasted_iota(jnp.int32, (rows, LANES), 1)
    return lane < NA_HEAD_DIM


def _ones_pad(v, low):
    one = jnp.ones_like(v)
    return jnp.where(low, v, one), jnp.where(low, one, v)


def _pair_finish(o_a, o_b, low):
    num = jnp.where(low, o_a, o_b)
    den = pltpu.roll(jnp.where(low, o_b, o_a), NA_HEAD_DIM, axis=1)
    return (num / den).astype(BF16)


def _na_kernel(q_ref, k_ref, v_ref, kc_ref, vc_ref, bias_ref, o_ref, va_ref, vb_ref, vca_ref, vcb_ref):
    qb = pl.program_id(2)
    n_rows = k_ref.shape[0] // GRID_W
    nwin = NA_WIN_ROWS * GRID_W

    @pl.when(qb == 0)
    def _():
        va_ref[...], vb_ref[...] = _ones_pad(v_ref[...], _pair_masks(v_ref.shape[0]))
        vca_ref[...], vcb_ref[...] = _ones_pad(vc_ref[...], _pair_masks(vc_ref.shape[0]))

    low = _pair_masks(GRID_W)
    kc = kc_ref[...]
    rows = range(NA_QROWS)
    starts, scores = [], []
    for j in rows:
        r = qb * NA_QROWS + j
        rs = jnp.clip(r - NA_WIN_ROWS // 2, 0, n_rows - NA_WIN_ROWS)
        start = pl.multiple_of(rs * GRID_W, GRID_W)
        q = q_ref[j * GRID_W:(j + 1) * GRID_W, :]
        zero = jnp.zeros_like(q)
        qs = jnp.concatenate([jnp.where(low, q, zero), jnp.where(low, zero, q)], axis=0)
        s = _dot_nt(qs, k_ref[pl.ds(start, nwin), :]) + bias_ref[r - rs].reshape(2 * GRID_W, nwin)
        starts.append(start)
        scores.append((s, _dot_nt(qs, kc)))
    probs = []
    for s, sc in scores:
        m = jnp.maximum(jnp.max(s, axis=-1, keepdims=True), jnp.max(sc, axis=-1, keepdims=True))
        probs.append((jnp.exp2(s - m).astype(BF16), jnp.exp2(sc - m).astype(BF16)))
    outs = []
    for start, (p, pc) in zip(starts, probs):
        o_a = _dot(p[:GRID_W], va_ref[pl.ds(start, nwin), :]) + _dot(pc[:GRID_W], vca_ref[...])
        o_b = _dot(p[GRID_W:], vb_ref[pl.ds(start, nwin), :]) + _dot(pc[GRID_W:], vcb_ref[...])
        outs.append(_pair_finish(o_a, o_b, low))
    o_ref[...] = jnp.concatenate(outs, axis=0)


def _na_attention(qa, ka, va, kc, vc, bias):
    bsz, s, _ = qa.shape
    nc = kc.shape[1]
    nq = NA_QROWS * GRID_W
    npairs = NA_WIDTH // LANES
    nwin = NA_WIN_ROWS * GRID_W
    whole = pl.BlockSpec((None, s, LANES), lambda p, b, qb: (b, 0, p))
    cspec = pl.BlockSpec((None, nc, LANES), lambda p, b, qb: (b, 0, p))
    return pl.pallas_call(
        _na_kernel,
        grid=(npairs, bsz, s // nq),
        in_specs=[pl.BlockSpec((None, nq, LANES), lambda p, b, qb: (b, qb, p)),
                  whole, whole, cspec, cspec,
                  pl.BlockSpec((NA_WIN_ROWS, 2, GRID_W, nwin), lambda p, b, qb: (0, p, 0, 0))],
        out_specs=pl.BlockSpec((None, nq, LANES), lambda p, b, qb: (b, qb, p)),
        out_shape=jax.ShapeDtypeStruct((bsz, s, NA_WIDTH), BF16),
        scratch_shapes=[pltpu.VMEM((s, LANES), BF16), pltpu.VMEM((s, LANES), BF16),
                        pltpu.VMEM((nc, LANES), BF16), pltpu.VMEM((nc, LANES), BF16)],
        compiler_params=_params(("arbitrary", "arbitrary", "arbitrary"), VMEM_LIMIT),
        name="na_attention",
    )(qa, ka, va, kc, vc, bias)


def _mla_kernel(q_ref, k_ref, v_ref, kc_ref, vc_ref, o_ref, va_ref, vb_ref, vca_ref, vcb_ref):
    @pl.when(pl.program_id(2) == 0)
    def _():
        va_ref[...], vb_ref[...] = _ones_pad(v_ref[...], _pair_masks(v_ref.shape[0]))
        vca_ref[...], vcb_ref[...] = _ones_pad(vc_ref[...], _pair_masks(vc_ref.shape[0]))

    scores = []
    for hh in range(2):
        sl = slice(hh * HEAD_PAD, (hh + 1) * HEAD_PAD)
        q = q_ref[:, sl]
        scores.append((_dot_nt(q, k_ref[:, sl]), _dot_nt(q, kc_ref[:, sl])))
    probs = []
    for s, sc in scores:
        m = jnp.maximum(jnp.max(s, axis=-1, keepdims=True), jnp.max(sc, axis=-1, keepdims=True))
        probs.append((jnp.exp2(s - m).astype(BF16), jnp.exp2(sc - m).astype(BF16)))
    outs = [_dot(p, vx_ref[...]) + _dot(pc, vcx_ref[...])
            for (p, pc), (vx_ref, vcx_ref) in zip(probs, ((va_ref, vca_ref), (vb_ref, vcb_ref)))]
    o_ref[...] = _pair_finish(outs[0], outs[1], _pair_masks(q_ref.shape[0]))


def _mla_attention(qb, kb, vb, kcb, vcb):
    bsz, s, _ = qb.shape
    nc = kcb.shape[1]
    tq = TQ_MLA
    npairs = MLA_WIDTH // LANES
    return pl.pallas_call(
        _mla_kernel,
        grid=(bsz, npairs, s // tq),
        in_specs=[pl.BlockSpec((None, tq, 2 * HEAD_PAD), lambda b, p, i: (b, i, p)),
                  pl.BlockSpec((None, s, 2 * HEAD_PAD), lambda b, p, i: (b, 0, p)),
                  pl.BlockSpec((None, s, LANES), lambda b, p, i: (b, 0, p)),
                  pl.BlockSpec((None, nc, 2 * HEAD_PAD), lambda b, p, i: (b, 0, p)),
                  pl.BlockSpec((None, nc, LANES), lambda b, p, i: (b, 0, p))],
        out_specs=pl.BlockSpec((None, tq, LANES), lambda b, p, i: (b, i, p)),
        out_shape=jax.ShapeDtypeStruct((bsz, s, MLA_WIDTH), BF16),
        scratch_shapes=[pltpu.VMEM((s, LANES), BF16), pltpu.VMEM((s, LANES), BF16),
                        pltpu.VMEM((nc, LANES), BF16), pltpu.VMEM((nc, LANES), BF16)],
        compiler_params=_params(("arbitrary", "arbitrary", "arbitrary"), VMEM_LIMIT),
        name="mla_attention",
    )(qb, kb, vb, kcb, vcb)


def _softmax2_pv(s, v):
    p = jnp.exp2(s - jnp.max(s, axis=-1, keepdims=True))
    return _dot(p.astype(BF16), v) / jnp.sum(p, axis=-1, keepdims=True)


def _ctx_kernel(qa_ref, ka_ref, va_ref, qb_ref, kb_ref, vb_ref, oa_ref, ob_ref):
    n = qa_ref.shape[0]
    low = _pair_masks(n)
    qa = qa_ref[...]
    zero = jnp.zeros_like(qa)
    ka = ka_ref[...]
    va = va_ref[...]
    o_lo = _softmax2_pv(_dot_nt(jnp.where(low, qa, zero), ka), va)
    o_hi = _softmax2_pv(_dot_nt(jnp.where(low, zero, qa), ka), va)
    oa_ref[...] = jnp.where(low, o_lo, o_hi).astype(BF16)
    vb = vb_ref[...]
    outs = []
    for hh in range(2):
        sl = slice(hh * HEAD_PAD, (hh + 1) * HEAD_PAD)
        outs.append(_softmax2_pv(_dot_nt(qb_ref[:, sl], kb_ref[:, sl]), vb))
    ob_ref[...] = jnp.where(low, outs[0], outs[1]).astype(BF16)


def _ctx_attention(qa, ka, va, qb, kb, vb):
    bsz, n, _ = qa.shape
    npairs = NA_WIDTH // LANES
    narrow = pl.BlockSpec((None, n, LANES), lambda b, p: (b, 0, p))
    wide = pl.BlockSpec((None, n, 2 * HEAD_PAD), lambda b, p: (b, 0, p))
    return pl.pallas_call(
        _ctx_kernel,
        grid=(bsz, npairs),
        in_specs=[narrow, narrow, narrow, wide, wide, narrow],
        out_specs=[narrow, narrow],
        out_shape=[jax.ShapeDtypeStruct((bsz, n, NA_WIDTH), BF16),
                   jax.ShapeDtypeStruct((bsz, n, MLA_WIDTH), BF16)],
        compiler_params=_params(("arbitrary", "arbitrary")),
        name="ctx_attention",
    )(qa, ka, va, qb, kb, vb)


def _merge_kernel(x_ref, oa_ref, ob_ref, ga_ref, gb_ref, woa_ref, wob_ref, wout_ref, gm_ref,
                  g_ref, sh_ref, sc_ref, wr_ref, xo_ref, h_ref, lg_ref):
    y = (ga_ref[...].astype(F32) * _dot(oa_ref[...], woa_ref[...])
         + gb_ref[...].astype(F32) * _dot(ob_ref[...], wob_ref[...]))
    xn = x_ref[...] + gm_ref[...] * _dot(y.astype(BF16), wout_ref[...])
    xo_ref[...] = xn
    h = _rms(xn, D_MODEL) * g_ref[...]
    h = h * (1.0 + sc_ref[...]) + sh_ref[...]
    _store_slabs(h_ref, _pack_rows(h))
    lg_ref[...] = _dot_nt(wr_ref[...], h.astype(BF16))


def _merge(x, oa, ob, ga, gb, mod3, mod_row, gain3, l, lw):
    bsz, sx, _ = x.shape
    tm = min(TM_MERGE, sx)
    nt = sx // tm
    row = (lambda b: b) if mod_row is None else (lambda b: mod_row)

    def full(shape):
        return pl.BlockSpec(shape, lambda b, i: (0,) * len(shape))

    def tok(width):
        return pl.BlockSpec((None, tm, width), lambda b, i: (b, i, 0))

    def mod(chunk):
        return pl.BlockSpec((None, 1, D_MODEL), lambda b, i: (row(b), 0, chunk))

    return pl.pallas_call(
        _merge_kernel,
        grid=(bsz, nt),
        in_specs=[tok(D_MODEL), tok(NA_WIDTH), tok(MLA_WIDTH), tok(D_MODEL), tok(D_MODEL),
                  full((NA_WIDTH, D_MODEL)), full((MLA_WIDTH, D_MODEL)), full((D_MODEL, D_MODEL)),
                  mod(2),
                  pl.BlockSpec((None, 1, D_MODEL), lambda b, i: (l, 0, 0)),
                  mod(3), mod(4),
                  full((N_EXPERTS, D_MODEL))],
        out_specs=[tok(D_MODEL),
                   pl.BlockSpec((tm * PACK_SUB, LANES), lambda b, i: (b * nt + i, 0)),
                   pl.BlockSpec((N_EXPERTS, tm), lambda b, i: (0, b * nt + i))],
        out_shape=[jax.ShapeDtypeStruct((bsz, sx, D_MODEL), F32),
                   jax.ShapeDtypeStruct((bsz * sx * PACK_SUB, LANES), jnp.uint32),
                   jax.ShapeDtypeStruct((N_EXPERTS, bsz * sx), F32)],
        compiler_params=_params(("arbitrary", "arbitrary"), VMEM_LIMIT),
        name="merge_branches",
    )(x, oa, ob, ga, gb, lw["w_oa"], lw["w_ob"], lw["w_out"], mod3, gain3, mod3, mod3,
      lw["w_router_t"])


def _route_kernel(lg_ref, bias_ref, tri_ref, eid_ref, gw_ref, rank_ref, cnt_ref, run_ref):
    @pl.when(pl.program_id(0) == 0)
    def _():
        run_ref[...] = jnp.zeros_like(run_ref)

    tr = lg_ref.shape[1]
    s = _sigmoid(lg_ref[...])
    biased = s + bias_ref[...]
    sub = lax.broadcasted_iota(jnp.int32, (EXPERTS_PER_GROUP, tr), 0).astype(F32)
    none = float(EXPERTS_PER_GROUP)
    best = jnp.full((1, tr), -jnp.inf, F32)
    e1 = jnp.zeros((1, tr), F32)
    e2 = jnp.zeros((1, tr), F32)
    for g in range(N_GROUPS):
        bg = biased[g * EXPERTS_PER_GROUP:(g + 1) * EXPERTS_PER_GROUP]
        m1 = jnp.max(bg, axis=0, keepdims=True)
        i1 = jnp.min(jnp.where(bg == m1, sub, none), axis=0, keepdims=True)
        rest = jnp.where(sub == i1, -jnp.inf, bg)
        m2 = jnp.max(rest, axis=0, keepdims=True)
        i2 = jnp.min(jnp.where(rest == m2, sub, none), axis=0, keepdims=True)
        score = m1 + m2
        better = score > best
        best = jnp.where(better, score, best)
        e1 = jnp.where(better, g * EXPERTS_PER_GROUP + i1, e1)
        e2 = jnp.where(better, g * EXPERTS_PER_GROUP + i2, e2)

    rowid = lax.broadcasted_iota(jnp.int32, (N_EXPERTS, tr), 0).astype(F32)
    is1 = rowid == e1
    is2 = rowid == e2
    w1 = jnp.sum(jnp.where(is1, s, 0.0), axis=0, keepdims=True)
    w2 = jnp.sum(jnp.where(is2, s, 0.0), axis=0, keepdims=True)
    tot = w1 + w2
    gw_ref[0:1, :] = w1 / tot
    gw_ref[1:2, :] = w2 / tot
    eid_ref[0:1, :] = e1.astype(jnp.int32)
    eid_ref[1:2, :] = e2.astype(jnp.int32)

    onehot = jnp.where(is1 | is2, 1.0, 0.0)
    before = _dot(onehot.astype(BF16), tri_ref[...]) + run_ref[...]
    rank_ref[0:1, :] = jnp.sum(jnp.where(is1, before, 0.0), axis=0, keepdims=True).astype(jnp.int32)
    rank_ref[1:2, :] = jnp.sum(jnp.where(is2, before, 0.0), axis=0, keepdims=True).astype(jnp.int32)
    run_ref[...] = run_ref[...] + jnp.sum(onehot, axis=1, keepdims=True)
    cnt_ref[...] = run_ref[...]


def _route(logits, bias_col, tri):
    t = logits.shape[1]
    tr = TR_ROUTE
    pair = pl.BlockSpec((TOP_K, tr), lambda i: (0, i))
    return pl.pallas_call(
        _route_kernel,
        grid=(t // tr,),
        in_specs=[pl.BlockSpec((N_EXPERTS, tr), lambda i: (0, i)),
                  pl.BlockSpec((N_EXPERTS, 1), lambda i: (0, 0)),
                  pl.BlockSpec((tr, tr), lambda i: (0, 0))],
        out_specs=[pair, pair, pair, pl.BlockSpec((N_EXPERTS, 1), lambda i: (0, 0))],
        out_shape=[jax.ShapeDtypeStruct((TOP_K, t), jnp.int32),
                   jax.ShapeDtypeStruct((TOP_K, t), F32),
                   jax.ShapeDtypeStruct((TOP_K, t), jnp.int32),
                   jax.ShapeDtypeStruct((N_EXPERTS, 1), F32)],
        scratch_shapes=[pltpu.VMEM((N_EXPERTS, 1), F32)],
        compiler_params=_params(("arbitrary",)),
        name="route",
    )(logits, bias_col, tri)


def _dispatch_kernel(dest_ref, h_ref, slots_in_ref, slots_ref, sem, *, n_tok):
    del slots_in_ref
    tm = h_ref.shape[0] // PACK_SUB
    base = pl.program_id(0) * tm

    def copy(r, k):
        d = dest_ref[k * n_tok + base + r]
        return pltpu.make_async_copy(h_ref.at[pl.ds(pl.multiple_of(r * PACK_SUB, PACK_SUB), PACK_SUB)],
                                     slots_ref.at[pl.ds(pl.multiple_of(d * PACK_SUB, PACK_SUB), PACK_SUB)], sem)

    def issue(r, carry):
        for k in range(TOP_K):
            copy(r, k).start(priority=k)
        return carry

    def drain(r, carry):
        for k in range(TOP_K):
            copy(r, k).wait()
        return carry

    lax.fori_loop(0, tm, issue, 0, unroll=MOVE_UNROLL)
    lax.fori_loop(0, tm, drain, 0, unroll=MOVE_UNROLL)


def _dispatch(h, dest_flat, slots):
    n_tok = h.shape[0] // PACK_SUB
    tm = TM_MOVE
    return pl.pallas_call(
        functools.partial(_dispatch_kernel, n_tok=n_tok),
        grid_spec=pltpu.PrefetchScalarGridSpec(
            num_scalar_prefetch=1,
            grid=(n_tok // tm,),
            in_specs=[pl.BlockSpec((tm * PACK_SUB, LANES), lambda i, dest: (i, 0)),
                      pl.BlockSpec(memory_space=pl.ANY)],
            out_specs=pl.BlockSpec(memory_space=pl.ANY),
            scratch_shapes=[pltpu.SemaphoreType.DMA],
        ),
        out_shape=jax.ShapeDtypeStruct(slots.shape, slots.dtype),
        input_output_aliases={2: 0},
        compiler_params=_params(("arbitrary",)),
        name="moe_dispatch",
    )(dest_flat, h, slots)


def _expert_kernel(be_ref, nused_ref, nxt_ref, x_ref, w1_hbm, w3_hbm, w2_hbm, y_ref,
                   st1, st3, st2, w1b, w3b, w2b, sem, *, layer):
    i = pl.program_id(0)
    used = i < nused_ref[0]
    stages = ((w1_hbm, st1, w1b), (w3_hbm, st3, w3b), (w2_hbm, st2, w2b))

    def fetch(e):
        return [pltpu.make_async_copy(w_hbm.at[layer, e], st, sem.at[j]) for j, (w_hbm, st, _) in enumerate(stages)]

    @pl.when(used)
    def _():
        e = be_ref[i]
        prev = be_ref[jnp.maximum(i - 1, 0)]

        @pl.when(i == 0)
        def _():
            for c in fetch(e):
                c.start()

        @pl.when((i == 0) | (e != prev))
        def _():
            for c, (_, st, wb) in zip(fetch(e), stages):
                c.wait()
                wb[...] = st[...].astype(BF16)
            nxt = nxt_ref[e]

            @pl.when(nxt >= 0)
            def _():
                for c in fetch(nxt):
                    c.start()

        bm = x_ref.shape[0] // PACK_SUB
        x_lo, x_hi = _unpack_rows(_load_slabs(x_ref, bm))
        xb = jnp.concatenate([x_lo.astype(BF16), x_hi.astype(BF16)], axis=1)
        a = _dot(xb, w1b[...])
        b = _dot(xb, w3b[...])
        _store_slabs(y_ref, _pack_rows(_dot((a * _sigmoid(a) * b).astype(BF16), w2b[...])))

    @pl.when(jnp.logical_not(used))
    def _():
        y_ref[...] = jnp.zeros_like(y_ref)


def _experts(slots, block_expert, nused, next_expert, w1, w3, w2, l):
    n_slots = slots.shape[0] // PACK_SUB
    bm = BM_EXP
    de = w1.shape[-1]
    rows = pl.BlockSpec((bm * PACK_SUB, LANES), lambda i, be, nu, nx: (i, 0))
    hbm = pl.BlockSpec(memory_space=pl.ANY)
    return pl.pallas_call(
        functools.partial(_expert_kernel, layer=l),
        grid_spec=pltpu.PrefetchScalarGridSpec(
            num_scalar_prefetch=3,
            grid=(n_slots // bm,),
            in_specs=[rows, hbm, hbm, hbm],
            out_specs=rows,
            scratch_shapes=[pltpu.VMEM((D_MODEL, de), F32), pltpu.VMEM((D_MODEL, de), F32),
                            pltpu.VMEM((de, D_MODEL), F32),
                            pltpu.VMEM((D_MODEL, de), BF16), pltpu.VMEM((D_MODEL, de), BF16),
                            pltpu.VMEM((de, D_MODEL), BF16),
                            pltpu.SemaphoreType.DMA((3,))],
        ),
        out_shape=jax.ShapeDtypeStruct(slots.shape, slots.dtype),
        compiler_params=_params(("arbitrary",), VMEM_LIMIT),
        name="moe_experts",
    )(block_expert, nused, next_expert, slots, w1, w3, w2)


def _combine_kernel(dest_ref, x_ref, gw_ref, gf_ref, y_hbm, o_ref, ybuf, sem, *, n_tok):
    tm = x_ref.shape[0]
    i = pl.program_id(0)
    nsteps = pl.num_programs(0)

    def copy(step, r, k):
        slot = step % 2
        d = dest_ref[k * n_tok + step * tm + r]
        return pltpu.make_async_copy(y_hbm.at[pl.ds(pl.multiple_of(d * PACK_SUB, PACK_SUB), PACK_SUB)],
                                     ybuf.at[slot, k, pl.ds(pl.multiple_of(r * PACK_SUB, PACK_SUB), PACK_SUB)],
                                     sem.at[slot])

    def issue(step):
        def body(r, carry):
            for k in range(TOP_K):
                copy(step, r, k).start(priority=k)
            return carry
        lax.fori_loop(0, tm, body, 0, unroll=MOVE_UNROLL)

    @pl.when(i == 0)
    def _():
        issue(i)

    @pl.when(i + 1 < nsteps)
    def _():
        issue(i + 1)

    def drain(r, carry):
        for k in range(TOP_K):
            copy(i, r, k).wait()
        return carry
    lax.fori_loop(0, tm, drain, 0, unroll=MOVE_UNROLL)

    slot = i % 2
    gw = gw_ref[...]
    halves = [_unpack_rows(_load_slabs(ybuf.at[slot, k], tm)) for k in range(TOP_K)]
    gf = gf_ref[...]
    for part in range(2):
        cols = slice(part * PACK_W, (part + 1) * PACK_W)
        y = gw[:, 0:1] * halves[0][part] + gw[:, 1:2] * halves[1][part]
        o_ref[:, cols] = x_ref[:, cols] + gf[:, cols] * y


def _combine(x, y_slots, dest_flat, gw_t, mod3, mod_row):
    bsz, sx, _ = x.shape
    n_tok = bsz * sx
    tm = TM_MOVE
    nt = sx // tm
    row = (lambda i: i // nt) if mod_row is None else (lambda i: mod_row)
    out = pl.pallas_call(
        functools.partial(_combine_kernel, n_tok=n_tok),
        grid_spec=pltpu.PrefetchScalarGridSpec(
            num_scalar_prefetch=1,
            grid=(n_tok // tm,),
            in_specs=[pl.BlockSpec((tm, D_MODEL), lambda i, dest: (i, 0)),
                      pl.BlockSpec((tm, TOP_K), lambda i, dest: (i, 0)),
                      pl.BlockSpec((None, 1, D_MODEL), lambda i, dest: (row(i), 0, 5)),
                      pl.BlockSpec(memory_space=pl.ANY)],
            out_specs=pl.BlockSpec((tm, D_MODEL), lambda i, dest: (i, 0)),
            scratch_shapes=[pltpu.VMEM((2, TOP_K, tm * PACK_SUB, LANES), jnp.uint32),
                            pltpu.SemaphoreType.DMA((2,))],
        ),
        out_shape=jax.ShapeDtypeStruct((n_tok, D_MODEL), F32),
        compiler_params=_params(("arbitrary",)),
        name="moe_combine",
    )(dest_flat, x.reshape(n_tok, D_MODEL), gw_t, mod3, y_slots)
    return out.reshape(bsz, sx, D_MODEL)


def _layer_weights(l, w_in, na_q_g, na_k_g, mla_cq_g, w_uq, mla_ckv_g, w_ukv, mla_q_g, mla_k_g,
                   w_oa, w_ob, w_out, w_router):
    wi = w_in[l]
    c0 = C_NA + MLA_Q_LORA + MLA_KV_LORA
    kr_cols = jnp.zeros((D_MODEL, HEAD_PAD), F32).at[:, MLA_NOPE:MLA_QK_DIM].set(wi[:, c0:c0 + MLA_ROPE])
    w_in_arr = jnp.concatenate([wi[:, :c0], kr_cols, wi[:, c0 + MLA_ROPE:]], axis=1).astype(BF16)

    def pad_heads(w, width):
        w = w.reshape(w.shape[0], MLA_HEADS, width)
        return jnp.pad(w, ((0, 0), (0, 0), (0, HEAD_PAD - width))).reshape(w.shape[0], MLA_QK_PAD)

    ukv = w_ukv[l].reshape(MLA_KV_LORA, MLA_HEADS, MLA_NOPE + MLA_V_DIM)
    return {
        "w_in": w_in_arr,
        "na_q_g": jnp.tile(na_q_g[l], NA_HEADS)[None, :],
        "na_k_g": jnp.tile(na_k_g[l], NA_HEADS)[None, :],
        "cq_g": mla_cq_g[l][None, :],
        "ckv_g": mla_ckv_g[l][None, :],
        "w_uq": pad_heads(w_uq[l], MLA_QK_DIM).astype(BF16),
        "w_uk": pad_heads(ukv[:, :, :MLA_NOPE].reshape(MLA_KV_LORA, -1), MLA_NOPE).astype(BF16),
        "w_uv": ukv[:, :, MLA_NOPE:].reshape(MLA_KV_LORA, MLA_WIDTH).astype(BF16),
        "q_g": jnp.pad(mla_q_g[l], (0, HEAD_PAD - MLA_QK_DIM))[None, :],
        "k_g": jnp.pad(mla_k_g[l], (0, HEAD_PAD - MLA_QK_DIM))[None, :],
        "w_oa": w_oa[l].astype(BF16),
        "w_ob": w_ob[l].astype(BF16),
        "w_out": w_out[l].astype(BF16),
        "w_router_t": w_router.T.astype(BF16),
    }


def _rope_tables(s):
    half = MLA_ROPE // 4
    pos = jnp.arange(s, dtype=jnp.int32)
    inv = ROPE_BASE ** (-jnp.arange(half, dtype=F32) / half)
    ang_r = (pos // GRID_W).astype(F32)[:, None] * inv[None, :]
    ang_c = (pos % GRID_W).astype(F32)[:, None] * inv[None, :]
    zeros = jnp.zeros((s, half), F32)
    lead = jnp.zeros((s, MLA_NOPE), F32)
    tail = jnp.zeros((s, HEAD_PAD - MLA_QK_DIM), F32)
    cos = jnp.concatenate([lead + 1.0, jnp.cos(ang_r), jnp.cos(ang_r), jnp.cos(ang_c), jnp.cos(ang_c), tail + 1.0], 1)
    s1 = jnp.concatenate([lead, -jnp.sin(ang_r), zeros, -jnp.sin(ang_c), zeros, tail], 1)
    s2 = jnp.concatenate([lead, zeros, jnp.sin(ang_r), zeros, jnp.sin(ang_c), tail], 1)
    return cos, s1, s2


def _na_bias_tables(rpb):
    d = jnp.arange(NA_WIN_ROWS)[:, None]
    i = jnp.arange(NA_WIN_ROWS)[None, :]
    qc = jnp.arange(GRID_W)[:, None]
    kc = jnp.arange(GRID_W)[None, :]
    cs = jnp.clip(qc - NA_WIN_COLS // 2, 0, GRID_W - NA_WIN_COLS)
    cvalid = (kc >= cs) & (kc < cs + NA_WIN_COLS)
    ohr = jax.nn.one_hot(i - d + NA_WIN_ROWS - 1, 2 * NA_WIN_ROWS - 1, dtype=F32)
    ohc = jax.nn.one_hot(jnp.clip(kc - qc + NA_WIN_COLS - 1, 0, 2 * NA_WIN_COLS - 2), 2 * NA_WIN_COLS - 1, dtype=F32)
    t = jnp.einsum("dia,hab,qkb->dhqik", ohr, rpb.astype(F32), ohc, precision=lax.Precision.HIGHEST)
    t = jnp.where(cvalid[None, None, :, None, :], t * LOG2E, NEG_INF)
    return t.reshape(NA_WIN_ROWS, rpb.shape[0], GRID_W, NA_WIN_ROWS * GRID_W)


def _slot_tables(eid, rank, counts):
    bm = BM_EXP
    counts = counts[:, 0].astype(jnp.int32)
    padded = ((counts + bm - 1) // bm) * bm
    pad_end = jnp.cumsum(padded)
    pad_start = pad_end - padded
    experts = jnp.arange(N_EXPERTS, dtype=jnp.int32)
    dest = rank + jnp.sum(jnp.where(eid[..., None] == experts, pad_start, 0), axis=-1)
    m = eid.shape[1] * TOP_K
    n_blocks = -(-m // bm) + N_EXPERTS
    blk = jnp.arange(n_blocks, dtype=jnp.int32) * bm
    block_expert = jnp.minimum(jnp.sum(pad_end[None, :] <= blk[:, None], axis=1), N_EXPERTS - 1).astype(jnp.int32)
    nused = (pad_end[-1:] // bm).astype(jnp.int32)
    later = lax.cummin(jnp.where(counts > 0, experts, N_EXPERTS), axis=0, reverse=True)
    next_expert = jnp.concatenate([later[1:], jnp.full((1,), N_EXPERTS, jnp.int32)])
    next_expert = jnp.where(next_expert < N_EXPERTS, next_expert, -1).astype(jnp.int32)
    return dest.astype(jnp.int32), block_expert, nused, next_expert, n_blocks * bm


def kernel(x, c, ctx, c_ctx, w_ada, b_ada, norm_mix_g, norm_ffn_g, w_in, na_q_g, na_k_g, na_rpb,
           mla_cq_g, w_uq, mla_ckv_g, w_ukv, mla_q_g, mla_k_g, w_oa, w_ob, w_out,
           w_router, router_bias, w1, w3, w2):
    bsz, s, d = x.shape
    n_ctx = ctx.shape[1]
    ctx_row = bsz
    pad_rows = -(bsz + 1) % 8
    cvec = jnp.concatenate([c, c_ctx[None, :], jnp.zeros((pad_rows, d), F32)], axis=0)
    b_ada3 = b_ada[:, None, :]
    mix_g3 = norm_mix_g[:, None, :]
    ffn_g3 = norm_ffn_g[:, None, :]
    tabs_x = _rope_tables(s)
    tabs_c = (jnp.ones((n_ctx, LANES), F32), jnp.zeros((n_ctx, LANES), F32), jnp.zeros((n_ctx, LANES), F32))
    tri = (jnp.arange(TR_ROUTE)[:, None] < jnp.arange(TR_ROUTE)[None, :]).astype(BF16)
    bias_col = router_bias.astype(F32)[:, None]
    n_x = bsz * s

    xc = ctx
    for l in range(DEPTH):
        last = l == DEPTH - 1
        lw = _layer_weights(l, w_in, na_q_g, na_k_g, mla_cq_g, w_uq, mla_ckv_g, w_ukv, mla_q_g, mla_k_g,
                            w_oa, w_ob, w_out, w_router)
        mod3 = _ada(cvec, w_ada, b_ada3, l)[:, None, :]
        qa, ka, va, qb, kb, vb, ga, gb = _proj(x, mod3, None, mix_g3, l, lw, tabs_x)
        qa_c, ka_c, va_c, qb_c, kb_c, vb_c, ga_c, gb_c = _proj(xc, mod3, ctx_row, mix_g3, l, lw, tabs_c)
        oa = _na_attention(qa, ka, va, ka_c, va_c, _na_bias_tables(na_rpb[l]))
        ob = _mla_attention(qb, kb, vb, kb_c, vb_c)
        x, h_x, lg = _merge(x, oa, ob, ga, gb, mod3, None, ffn_g3, l, lw)
        if not last:
            oa_c, ob_c = _ctx_attention(qa_c, ka_c, va_c, qb_c, kb_c, vb_c)
            xc, h_c, lg_c = _merge(xc, oa_c, ob_c, ga_c, gb_c, mod3, ctx_row, ffn_g3, l, lw)
            lg = jnp.concatenate([lg, lg_c], axis=1)

        eid, gw, rank, counts = _route(lg, bias_col, tri)
        dest, block_expert, nused, next_expert, n_slots = _slot_tables(eid, rank, counts)
        gw_t = gw.T
        slots = jnp.zeros((n_slots * PACK_SUB, LANES), jnp.uint32)
        slots = _dispatch(h_x, dest[:, :n_x].reshape(-1), slots)
        if not last:
            slots = _dispatch(h_c, dest[:, n_x:].reshape(-1), slots)
        y_slots = _experts(slots, block_expert, nused, next_expert, w1, w3, w2, l)
        x = _combine(x, y_slots, dest[:, :n_x].reshape(-1), gw_t[:n_x], mod3, None)
        if not last:
            xc = _combine(xc, y_slots, dest[:, n_x:].reshape(-1), gw_t[n_x:], mod3, ctx_row)
    return x
```

```python
import functools

import jax
import jax.numpy as jnp
from jax import lax
from jax.experimental import pallas as pl
from jax.experimental.pallas import tpu as pltpu

F32 = jnp.float32
BF16 = jnp.bfloat16

D_MODEL = 1024
DEPTH = 2
GRID_W = 64
N_MOD = 6

NA_HEADS = 8
NA_HEAD_DIM = 64
NA_WIN_ROWS = 8
NA_WIN_COLS = 16
NA_WIDTH = NA_HEADS * NA_HEAD_DIM

MLA_HEADS = 8
MLA_NOPE = 64
MLA_ROPE = 32
MLA_QK_DIM = MLA_NOPE + MLA_ROPE
MLA_V_DIM = 64
MLA_Q_LORA = 384
MLA_KV_LORA = 256
MLA_WIDTH = MLA_HEADS * MLA_V_DIM
ROPE_BASE = 10000.0

N_EXPERTS = 64
EXPERTS_PER_GROUP = 8
N_GROUPS = N_EXPERTS // EXPERTS_PER_GROUP
TOP_K = 2

RMS_EPS = 1e-6
NEG_INF = -1e30
LOG2E = 1.4426950408889634

LANES = 128
HEAD_PAD = LANES
MLA_QK_PAD = MLA_HEADS * HEAD_PAD

C_NA = 3 * NA_WIDTH
C_CQ = C_NA + MLA_Q_LORA
C_CKV = C_CQ + MLA_KV_LORA
C_KR = C_CKV + HEAD_PAD
C_ALL = C_KR + 2 * D_MODEL

TM_PROJ = 512
TM_MERGE = 512
TQ_MLA = 1024
NA_QROWS = 32
TR_ROUTE = 1024
BM_EXP = 256
PACK_W = D_MODEL // 2
PACK_SUB = PACK_W // LANES
TM_MOVE = 512
MOVE_UNROLL = 8
VMEM_LIMIT = 56 * 1024 * 1024


def _sigmoid(v):
    return 1.0 / (1.0 + jnp.exp(-v))


def _rms(v, n):
    return v * lax.rsqrt(jnp.sum(v * v, axis=-1, keepdims=True) * (1.0 / n) + RMS_EPS)


def _pack_rows(v):
    lo = pltpu.bitcast(v[:, :PACK_W].astype(BF16).astype(F32), jnp.uint32)
    hi = pltpu.bitcast(v[:, PACK_W:].astype(BF16).astype(F32), jnp.uint32)
    return (lo >> 16) | (hi & jnp.uint32(0xFFFF0000))


def _unpack_rows(w):
    return pltpu.bitcast(w << 16, F32), pltpu.bitcast(w & jnp.uint32(0xFFFF0000), F32)


def _store_slabs(ref, w):
    for c in range(PACK_SUB):
        ref[pl.ds(c, w.shape[0], stride=PACK_SUB), :] = w[:, c * LANES:(c + 1) * LANES]


def _load_slabs(ref, m):
    return jnp.concatenate([ref[pl.ds(c, m, stride=PACK_SUB), :] for c in range(PACK_SUB)], axis=1)


def _split_bf16(a):
    hi = a.astype(BF16)
    lo = (a - hi.astype(F32)).astype(BF16)
    return hi, lo


def _dot(a, b):
    return jnp.dot(a, b, preferred_element_type=F32)


def _dot_nt(a, b):
    return lax.dot_general(a, b, (((1,), (1,)), ((), ())), preferred_element_type=F32)


def _params(sem, vmem=None):
    return pltpu.CompilerParams(dimension_semantics=sem, vmem_limit_bytes=vmem)


def _ada_kernel(c_ref, w_ref, b_ref, o_ref):
    cv = c_ref[...]
    s = cv * _sigmoid(cv)
    s_hi, s_lo = _split_bf16(s)
    w_hi, w_lo = _split_bf16(w_ref[...])
    o_ref[...] = _dot(s_hi, w_hi) + _dot(s_lo, w_hi) + _dot(s_hi, w_lo) + b_ref[...]


def _ada(cvec, w_ada, b_ada3, l):
    rows = cvec.shape[0]
    n = N_MOD * D_MODEL
    tn = 512
    return pl.pallas_call(
        _ada_kernel,
        grid=(n // tn,),
        in_specs=[
            pl.BlockSpec((rows, D_MODEL), lambda j: (0, 0)),
            pl.BlockSpec((None, D_MODEL, tn), lambda j: (l, 0, j)),
            pl.BlockSpec((None, 1, tn), lambda j: (l, 0, j)),
        ],
        out_specs=pl.BlockSpec((rows, tn), lambda j: (0, j)),
        out_shape=jax.ShapeDtypeStruct((rows, n), F32),
        compiler_params=_params(("arbitrary",)),
        name="ada_mod",
    )(cvec, w_ada, b_ada3)


def _rope(v, cos, s1, s2):
    return v * cos + pltpu.roll(v, LANES - 8, axis=1) * s1 + pltpu.roll(v, 8, axis=1) * s2


def _proj_kernel(x_ref, sh_ref, sc_ref, g_ref, win_ref, naqg_ref, nakg_ref, cqg_ref, ckvg_ref,
                 wuq_ref, wuk_ref, wuv_ref, qg_ref, kg_ref, cos_ref, s1_ref, s2_ref,
                 qa_ref, ka_ref, vaa_ref, vab_ref, qb_ref, kb_ref, vba_ref, vbb_ref, ga_ref, gb_ref):
    x = x_ref[...]
    h = _rms(x, D_MODEL) * g_ref[...]
    h = h * (1.0 + sc_ref[...]) + sh_ref[...]
    hb = h.astype(BF16)

    low = _pair_masks(x.shape[0])

    def headnorm(z, g):
        tiles = []
        for c in range(NA_WIDTH // LANES):
            zc = z[:, c * LANES:(c + 1) * LANES]
            sq = zc * zc
            lo = jnp.sum(jnp.where(low, sq, 0.0), axis=-1, keepdims=True)
            hi = jnp.sum(jnp.where(low, 0.0, sq), axis=-1, keepdims=True)
            tiles.append(zc * lax.rsqrt(jnp.where(low, lo, hi) * (1.0 / NA_HEAD_DIM) + RMS_EPS))
        return jnp.concatenate(tiles, axis=1) * g

    zq = _dot(hb, win_ref[:, 0:NA_WIDTH])
    qa_ref[...] = (headnorm(zq, naqg_ref[...]) * (NA_HEAD_DIM ** -0.5 * LOG2E)).astype(BF16)
    zk = _dot(hb, win_ref[:, NA_WIDTH:2 * NA_WIDTH])
    ka_ref[...] = headnorm(zk, nakg_ref[...]).astype(BF16)
    pair_low = (lax.broadcasted_iota(jnp.int32, (x.shape[0], NA_WIDTH), 1) & (LANES - 1)) < NA_HEAD_DIM
    vaa_ref[...], vab_ref[...] = _ones_pad(_dot(hb, win_ref[:, 2 * NA_WIDTH:C_NA]).astype(BF16), pair_low)

    cos = cos_ref[...]
    s1 = s1_ref[...]
    s2 = s2_ref[...]

    cq = _rms(_dot(hb, win_ref[:, C_NA:C_CQ]), MLA_Q_LORA) * cqg_ref[...]
    q = _dot(cq.astype(BF16), wuq_ref[...])
    qg = qg_ref[...]
    for hh in range(MLA_HEADS):
        sl = slice(hh * HEAD_PAD, (hh + 1) * HEAD_PAD)
        qn = _rms(q[:, sl], MLA_QK_DIM) * qg
        qb_ref[:, sl] = (_rope(qn, cos, s1, s2) * (MLA_QK_DIM ** -0.5 * LOG2E)).astype(BF16)

    ckv = (_rms(_dot(hb, win_ref[:, C_CQ:C_CKV]), MLA_KV_LORA) * ckvg_ref[...]).astype(BF16)
    kn = _dot(ckv, wuk_ref[...])
    kr = _dot(hb, win_ref[:, C_CKV:C_KR])
    kg = kg_ref[...]
    for hh in range(MLA_HEADS):
        sl = slice(hh * HEAD_PAD, (hh + 1) * HEAD_PAD)
        kh = _rms(kn[:, sl] + kr, MLA_QK_DIM) * kg
        kb_ref[:, sl] = _rope(kh, cos, s1, s2).astype(BF16)
    vba_ref[...], vbb_ref[...] = _ones_pad(_dot(ckv, wuv_ref[...]).astype(BF16), pair_low)

    ga_ref[...] = _sigmoid(_dot(hb, win_ref[:, C_KR:C_KR + D_MODEL])).astype(BF16)
    gb_ref[...] = _sigmoid(_dot(hb, win_ref[:, C_KR + D_MODEL:C_ALL])).astype(BF16)


def _proj(x, mod3, mod_row, gain3, l, lw, tabs):
    bsz, sx, _ = x.shape
    tm = min(TM_PROJ, sx)
    nt = sx // tm
    row = (lambda b: b) if mod_row is None else (lambda b: mod_row)

    def full(shape):
        return pl.BlockSpec(shape, lambda b, i: (0,) * len(shape))

    def tok(width):
        return pl.BlockSpec((None, tm, width), lambda b, i: (b, i, 0))

    def tab():
        return pl.BlockSpec((tm, LANES), lambda b, i: (i, 0))

    in_specs = [
        tok(D_MODEL),
        pl.BlockSpec((None, 1, D_MODEL), lambda b, i: (row(b), 0, 0)),
        pl.BlockSpec((None, 1, D_MODEL), lambda b, i: (row(b), 0, 1)),
        pl.BlockSpec((None, 1, D_MODEL), lambda b, i: (l, 0, 0)),
        full((D_MODEL, C_ALL)),
        full((1, NA_WIDTH)), full((1, NA_WIDTH)),
        full((1, MLA_Q_LORA)), full((1, MLA_KV_LORA)),
        full((MLA_Q_LORA, MLA_QK_PAD)), full((MLA_KV_LORA, MLA_QK_PAD)), full((MLA_KV_LORA, MLA_WIDTH)),
        full((1, HEAD_PAD)), full((1, HEAD_PAD)),
        tab(), tab(), tab(),
    ]
    widths = (NA_WIDTH, NA_WIDTH, NA_WIDTH, NA_WIDTH, MLA_QK_PAD, MLA_QK_PAD, MLA_WIDTH, MLA_WIDTH, D_MODEL, D_MODEL)
    return pl.pallas_call(
        _proj_kernel,
        grid=(bsz, nt),
        in_specs=in_specs,
        out_specs=[tok(w) for w in widths],
        out_shape=[jax.ShapeDtypeStruct((bsz, sx, w), BF16) for w in widths],
        compiler_params=_params(("arbitrary", "arbitrary"), VMEM_LIMIT),
        name="mixer_inputs",
    )(x, mod3, mod3, gain3, lw["w_in"], lw["na_q_g"], lw["na_k_g"], lw["cq_g"], lw["ckv_g"],
      lw["w_uq"], lw["w_uk"], lw["w_uv"], lw["q_g"], lw["k_g"], tabs[0], tabs[1], tabs[2])


def _pair_masks(rows):
    lane = lax.broadcasted_iota(jnp.int32, (rows, LANES), 1)
    return lane < NA_HEAD_DIM


def _ones_pad(v, low):
    one = jnp.ones_like(v)
    return jnp.where(low, v, one), jnp.where(low, one, v)


def _pair_finish(o_a, o_b, low):
    num = jnp.where(low, o_a, o_b)
    den = pltpu.roll(jnp.where(low, o_b, o_a), NA_HEAD_DIM, axis=1)
    return (num / den).astype(BF16)


def _na_kernel(q_ref, k_ref, va_ref, vb_ref, kc_ref, vca_ref, vcb_ref, bias_ref, o_ref):
    qb = pl.program_id(2)
    n_rows = k_ref.shape[0] // GRID_W
    nwin = NA_WIN_ROWS * GRID_W

    low = _pair_masks(GRID_W)
    kc = kc_ref[...]
    rows = range(NA_QROWS)
    starts, scores = [], []
    for j in rows:
        r = qb * NA_QROWS + j
        rs = jnp.clip(r - NA_WIN_ROWS // 2, 0, n_rows - NA_WIN_ROWS)
        start = pl.multiple_of(rs * GRID_W, GRID_W)
        q = q_ref[j * GRID_W:(j + 1) * GRID_W, :]
        zero = jnp.zeros_like(q)
        qs = jnp.concatenate([jnp.where(low, q, zero), jnp.where(low, zero, q)], axis=0)
        s = _dot_nt(qs, k_ref[pl.ds(start, nwin), :]) + bias_ref[r - rs].reshape(2 * GRID_W, nwin)
        starts.append(start)
        scores.append((s, _dot_nt(qs, kc)))
    probs = []
    for s, sc in scores:
        m = jnp.maximum(jnp.max(s, axis=-1, keepdims=True), jnp.max(sc, axis=-1, keepdims=True))
        probs.append((jnp.exp2(s - m).astype(BF16), jnp.exp2(sc - m).astype(BF16)))
    outs = []
    for start, (p, pc) in zip(starts, probs):
        o_a = _dot(p[:GRID_W], va_ref[pl.ds(start, nwin), :]) + _dot(pc[:GRID_W], vca_ref[...])
        o_b = _dot(p[GRID_W:], vb_ref[pl.ds(start, nwin), :]) + _dot(pc[GRID_W:], vcb_ref[...])
        outs.append(_pair_finish(o_a, o_b, low))
    o_ref[...] = jnp.concatenate(outs, axis=0)


def _na_attention(qa, ka, va_a, va_b, kc, vc_a, vc_b, bias):
    bsz, s, _ = qa.shape
    nc = kc.shape[1]
    nq = NA_QROWS * GRID_W
    npairs = NA_WIDTH // LANES
    nwin = NA_WIN_ROWS * GRID_W
    whole = pl.BlockSpec((None, s, LANES), lambda p, b, qb: (b, 0, p))
    cspec = pl.BlockSpec((None, nc, LANES), lambda p, b, qb: (b, 0, p))
    return pl.pallas_call(
        _na_kernel,
        grid=(npairs, bsz, s // nq),
        in_specs=[pl.BlockSpec((None, nq, LANES), lambda p, b, qb: (b, qb, p)),
                  whole, whole, whole, cspec, cspec, cspec,
                  pl.BlockSpec((NA_WIN_ROWS, 2, GRID_W, nwin), lambda p, b, qb: (0, p, 0, 0))],
        out_specs=pl.BlockSpec((None, nq, LANES), lambda p, b, qb: (b, qb, p)),
        out_shape=jax.ShapeDtypeStruct((bsz, s, NA_WIDTH), BF16),
        compiler_params=_params(("arbitrary", "arbitrary", "arbitrary"), VMEM_LIMIT),
        name="na_attention",
    )(qa, ka, va_a, va_b, kc, vc_a, vc_b, bias)


def _mla_kernel(q_ref, k_ref, va_ref, vb_ref, kc_ref, vca_ref, vcb_ref, o_ref):
    scores = []
    for hh in range(2):
        sl = slice(hh * HEAD_PAD, (hh + 1) * HEAD_PAD)
        q = q_ref[:, sl]
        scores.append((_dot_nt(q, k_ref[:, sl]), _dot_nt(q, kc_ref[:, sl])))
    probs = []
    for s, sc in scores:
        m = jnp.maximum(jnp.max(s, axis=-1, keepdims=True), jnp.max(sc, axis=-1, keepdims=True))
        probs.append((jnp.exp2(s - m).astype(BF16), jnp.exp2(sc - m).astype(BF16)))
    outs = [_dot(p, vx_ref[...]) + _dot(pc, vcx_ref[...])
            for (p, pc), (vx_ref, vcx_ref) in zip(probs, ((va_ref, vca_ref), (vb_ref, vcb_ref)))]
    o_ref[...] = _pair_finish(outs[0], outs[1], _pair_masks(q_ref.shape[0]))


def _mla_attention(qb, kb, vb_a, vb_b, kcb, vcb_a, vcb_b):
    bsz, s, _ = qb.shape
    nc = kcb.shape[1]
    tq = TQ_MLA
    npairs = MLA_WIDTH // LANES
    return pl.pallas_call(
        _mla_kernel,
        grid=(bsz, npairs, s // tq),
        in_specs=[pl.BlockSpec((None, tq, 2 * HEAD_PAD), lambda b, p, i: (b, i, p)),
                  pl.BlockSpec((None, s, 2 * HEAD_PAD), lambda b, p, i: (b, 0, p)),
                  pl.BlockSpec((None, s, LANES), lambda b, p, i: (b, 0, p)),
                  pl.BlockSpec((None, s, LANES), lambda b, p, i: (b, 0, p)),
                  pl.BlockSpec((None, nc, 2 * HEAD_PAD), lambda b, p, i: (b, 0, p)),
                  pl.BlockSpec((None, nc, LANES), lambda b, p, i: (b, 0, p)),
                  pl.BlockSpec((None, nc, LANES), lambda b, p, i: (b, 0, p))],
        out_specs=pl.BlockSpec((None, tq, LANES), lambda b, p, i: (b, i, p)),
        out_shape=jax.ShapeDtypeStruct((bsz, s, MLA_WIDTH), BF16),
        compiler_params=_params(("arbitrary", "arbitrary", "arbitrary"), VMEM_LIMIT),
        name="mla_attention",
    )(qb, kb, vb_a, vb_b, kcb, vcb_a, vcb_b)


def _softmax2_pv(s, v):
    p = jnp.exp2(s - jnp.max(s, axis=-1, keepdims=True))
    return _dot(p.astype(BF16), v) / jnp.sum(p, axis=-1, keepdims=True)


def _ctx_kernel(qa_ref, ka_ref, vaa_ref, vab_ref, qb_ref, kb_ref, vba_ref, vbb_ref, oa_ref, ob_ref):
    n = qa_ref.shape[0]
    low = _pair_masks(n)
    qa = qa_ref[...]
    zero = jnp.zeros_like(qa)
    ka = ka_ref[...]
    va = jnp.where(low, vaa_ref[...], vab_ref[...])
    o_lo = _softmax2_pv(_dot_nt(jnp.where(low, qa, zero), ka), va)
    o_hi = _softmax2_pv(_dot_nt(jnp.where(low, zero, qa), ka), va)
    oa_ref[...] = jnp.where(low, o_lo, o_hi).astype(BF16)
    vb = jnp.where(low, vba_ref[...], vbb_ref[...])
    outs = []
    for hh in range(2):
        sl = slice(hh * HEAD_PAD, (hh + 1) * HEAD_PAD)
        outs.append(_softmax2_pv(_dot_nt(qb_ref[:, sl], kb_ref[:, sl]), vb))
    ob_ref[...] = jnp.where(low, outs[0], outs[1]).astype(BF16)


def _ctx_attention(qa, ka, va_a, va_b, qb, kb, vb_a, vb_b):
    bsz, n, _ = qa.shape
    npairs = NA_WIDTH // LANES
    narrow = pl.BlockSpec((None, n, LANES), lambda b, p: (b, 0, p))
    wide = pl.BlockSpec((None, n, 2 * HEAD_PAD), lambda b, p: (b, 0, p))
    return pl.pallas_call(
        _ctx_kernel,
        grid=(bsz, npairs),
        in_specs=[narrow, narrow, narrow, narrow, wide, wide, narrow, narrow],
        out_specs=[narrow, narrow],
        out_shape=[jax.ShapeDtypeStruct((bsz, n, NA_WIDTH), BF16),
                   jax.ShapeDtypeStruct((bsz, n, MLA_WIDTH), BF16)],
        compiler_params=_params(("arbitrary", "arbitrary")),
        name="ctx_attention",
    )(qa, ka, va_a, va_b, qb, kb, vb_a, vb_b)


def _merge_kernel(x_ref, oa_ref, ob_ref, ga_ref, gb_ref, woa_ref, wob_ref, wout_ref, gm_ref,
                  g_ref, sh_ref, sc_ref, wr_ref, xo_ref, h_ref, lg_ref):
    y = (ga_ref[...].astype(F32) * _dot(oa_ref[...], woa_ref[...])
         + gb_ref[...].astype(F32) * _dot(ob_ref[...], wob_ref[...]))
    xn = x_ref[...] + gm_ref[...] * _dot(y.astype(BF16), wout_ref[...])
    xo_ref[...] = xn
    h = _rms(xn, D_MODEL) * g_ref[...]
    h = h * (1.0 + sc_ref[...]) + sh_ref[...]
    _store_slabs(h_ref, _pack_rows(h))
    lg_ref[...] = _dot_nt(wr_ref[...], h.astype(BF16))


def _merge(x, oa, ob, ga, gb, mod3, mod_row, gain3, l, lw):
    bsz, sx, _ = x.shape
    tm = min(TM_MERGE, sx)
    nt = sx // tm
    row = (lambda b: b) if mod_row is None else (lambda b: mod_row)

    def full(shape):
        return pl.BlockSpec(shape, lambda b, i: (0,) * len(shape))

    def tok(width):
        return pl.BlockSpec((None, tm, width), lambda b, i: (b, i, 0))

    def mod(chunk):
        return pl.BlockSpec((None, 1, D_MODEL), lambda b, i: (row(b), 0, chunk))

    return pl.pallas_call(
        _merge_kernel,
        grid=(bsz, nt),
        in_specs=[tok(D_MODEL), tok(NA_WIDTH), tok(MLA_WIDTH), tok(D_MODEL), tok(D_MODEL),
                  full((NA_WIDTH, D_MODEL)), full((MLA_WIDTH, D_MODEL)), full((D_MODEL, D_MODEL)),
                  mod(2),
                  pl.BlockSpec((None, 1, D_MODEL), lambda b, i: (l, 0, 0)),
                  mod(3), mod(4),
                  full((N_EXPERTS, D_MODEL))],
        out_specs=[tok(D_MODEL),
                   pl.BlockSpec((tm * PACK_SUB, LANES), lambda b, i: (b * nt + i, 0)),
                   pl.BlockSpec((N_EXPERTS, tm), lambda b, i: (0, b * nt + i))],
        out_shape=[jax.ShapeDtypeStruct((bsz, sx, D_MODEL), F32),
                   jax.ShapeDtypeStruct((bsz * sx * PACK_SUB, LANES), jnp.uint32),
                   jax.ShapeDtypeStruct((N_EXPERTS, bsz * sx), F32)],
        compiler_params=_params(("arbitrary", "arbitrary"), VMEM_LIMIT),
        name="merge_branches",
    )(x, oa, ob, ga, gb, lw["w_oa"], lw["w_ob"], lw["w_out"], mod3, gain3, mod3, mod3,
      lw["w_router_t"])


def _route_kernel(lg_ref, bias_ref, tri_ref, eid_ref, gw_ref, rank_ref, cnt_ref, run_ref):
    @pl.when(pl.program_id(0) == 0)
    def _():
        run_ref[...] = jnp.zeros_like(run_ref)

    tr = lg_ref.shape[1]
    s = _sigmoid(lg_ref[...])
    biased = s + bias_ref[...]
    sub = lax.broadcasted_iota(jnp.int32, (EXPERTS_PER_GROUP, tr), 0).astype(F32)
    none = float(EXPERTS_PER_GROUP)
    best = jnp.full((1, tr), -jnp.inf, F32)
    e1 = jnp.zeros((1, tr), F32)
    e2 = jnp.zeros((1, tr), F32)
    for g in range(N_GROUPS):
        bg = biased[g * EXPERTS_PER_GROUP:(g + 1) * EXPERTS_PER_GROUP]
        m1 = jnp.max(bg, axis=0, keepdims=True)
        i1 = jnp.min(jnp.where(bg == m1, sub, none), axis=0, keepdims=True)
        rest = jnp.where(sub == i1, -jnp.inf, bg)
        m2 = jnp.max(rest, axis=0, keepdims=True)
        i2 = jnp.min(jnp.where(rest == m2, sub, none), axis=0, keepdims=True)
        score = m1 + m2
        better = score > best
        best = jnp.where(better, score, best)
        e1 = jnp.where(better, g * EXPERTS_PER_GROUP + i1, e1)
        e2 = jnp.where(better, g * EXPERTS_PER_GROUP + i2, e2)

    rowid = lax.broadcasted_iota(jnp.int32, (N_EXPERTS, tr), 0).astype(F32)
    is1 = rowid == e1
    is2 = rowid == e2
    w1 = jnp.sum(jnp.where(is1, s, 0.0), axis=0, keepdims=True)
    w2 = jnp.sum(jnp.where(is2, s, 0.0), axis=0, keepdims=True)
    tot = w1 + w2
    gw_ref[0:1, :] = w1 / tot
    gw_ref[1:2, :] = w2 / tot
    eid_ref[0:1, :] = e1.astype(jnp.int32)
    eid_ref[1:2, :] = e2.astype(jnp.int32)

    onehot = jnp.where(is1 | is2, 1.0, 0.0)
    before = _dot(onehot.astype(BF16), tri_ref[...]) + run_ref[...]
    rank_ref[0:1, :] = jnp.sum(jnp.where(is1, before, 0.0), axis=0, keepdims=True).astype(jnp.int32)
    rank_ref[1:2, :] = jnp.sum(jnp.where(is2, before, 0.0), axis=0, keepdims=True).astype(jnp.int32)
    run_ref[...] = run_ref[...] + jnp.sum(onehot, axis=1, keepdims=True)
    cnt_ref[...] = run_ref[...]


def _route(logits, bias_col, tri):
    t = logits.shape[1]
    tr = TR_ROUTE
    pair = pl.BlockSpec((TOP_K, tr), lambda i: (0, i))
    return pl.pallas_call(
        _route_kernel,
        grid=(t // tr,),
        in_specs=[pl.BlockSpec((N_EXPERTS, tr), lambda i: (0, i)),
                  pl.BlockSpec((N_EXPERTS, 1), lambda i: (0, 0)),
                  pl.BlockSpec((tr, tr), lambda i: (0, 0))],
        out_specs=[pair, pair, pair, pl.BlockSpec((N_EXPERTS, 1), lambda i: (0, 0))],
        out_shape=[jax.ShapeDtypeStruct((TOP_K, t), jnp.int32),
                   jax.ShapeDtypeStruct((TOP_K, t), F32),
                   jax.ShapeDtypeStruct((TOP_K, t), jnp.int32),
                   jax.ShapeDtypeStruct((N_EXPERTS, 1), F32)],
        scratch_shapes=[pltpu.VMEM((N_EXPERTS, 1), F32)],
        compiler_params=_params(("arbitrary",)),
        name="route",
    )(logits, bias_col, tri)


def _dispatch_kernel(dest_ref, h_ref, slots_in_ref, slots_ref, sem, *, n_tok):
    del slots_in_ref
    tm = h_ref.shape[0] // PACK_SUB
    base = pl.program_id(0) * tm

    def copy(r, k):
        d = dest_ref[k * n_tok + base + r]
        return pltpu.make_async_copy(h_ref.at[pl.ds(pl.multiple_of(r * PACK_SUB, PACK_SUB), PACK_SUB)],
                                     slots_ref.at[pl.ds(pl.multiple_of(d * PACK_SUB, PACK_SUB), PACK_SUB)], sem)

    def issue(r, carry):
        for k in range(TOP_K):
            copy(r, k).start(priority=k)
        return carry

    def drain(r, carry):
        for k in range(TOP_K):
            copy(r, k).wait()
        return carry

    lax.fori_loop(0, tm, issue, 0, unroll=MOVE_UNROLL)
    lax.fori_loop(0, tm, drain, 0, unroll=MOVE_UNROLL)


def _dispatch(h, dest_flat, slots):
    n_tok = h.shape[0] // PACK_SUB
    tm = TM_MOVE
    return pl.pallas_call(
        functools.partial(_dispatch_kernel, n_tok=n_tok),
        grid_spec=pltpu.PrefetchScalarGridSpec(
            num_scalar_prefetch=1,
            grid=(n_tok // tm,),
            in_specs=[pl.BlockSpec((tm * PACK_SUB, LANES), lambda i, dest: (i, 0)),
                      pl.BlockSpec(memory_space=pl.ANY)],
            out_specs=pl.BlockSpec(memory_space=pl.ANY),
            scratch_shapes=[pltpu.SemaphoreType.DMA],
        ),
        out_shape=jax.ShapeDtypeStruct(slots.shape, slots.dtype),
        input_output_aliases={2: 0},
        compiler_params=_params(("arbitrary",)),
        name="moe_dispatch",
    )(dest_flat, h, slots)


def _expert_kernel(be_ref, nused_ref, nxt_ref, x_ref, w1_hbm, w3_hbm, w2_hbm, y_ref,
                   st1, st3, st2, w1b, w3b, w2b, sem, *, layer):
    i = pl.program_id(0)
    used = i < nused_ref[0]
    stages = ((w1_hbm, st1, w1b), (w3_hbm, st3, w3b), (w2_hbm, st2, w2b))

    def fetch(e):
        return [pltpu.make_async_copy(w_hbm.at[layer, e], st, sem.at[j]) for j, (w_hbm, st, _) in enumerate(stages)]

    @pl.when(used)
    def _():
        e = be_ref[i]
        prev = be_ref[jnp.maximum(i - 1, 0)]

        @pl.when(i == 0)
        def _():
            for c in fetch(e):
                c.start()

        @pl.when((i == 0) | (e != prev))
        def _():
            for c, (_, st, wb) in zip(fetch(e), stages):
                c.wait()
                wb[...] = st[...].astype(BF16)
            nxt = nxt_ref[e]

            @pl.when(nxt >= 0)
            def _():
                for c in fetch(nxt):
                    c.start()

        bm = x_ref.shape[0] // PACK_SUB
        x_lo, x_hi = _unpack_rows(_load_slabs(x_ref, bm))
        xb = jnp.concatenate([x_lo.astype(BF16), x_hi.astype(BF16)], axis=1)
        a = _dot(xb, w1b[...])
        b = _dot(xb, w3b[...])
        _store_slabs(y_ref, _pack_rows(_dot((a * _sigmoid(a) * b).astype(BF16), w2b[...])))

    @pl.when(jnp.logical_not(used))
    def _():
        y_ref[...] = jnp.zeros_like(y_ref)


def _experts(slots, block_expert, nused, next_expert, w1, w3, w2, l):
    n_slots = slots.shape[0] // PACK_SUB
    bm = BM_EXP
    de = w1.shape[-1]
    rows = pl.BlockSpec((bm * PACK_SUB, LANES), lambda i, be, nu, nx: (i, 0))
    hbm = pl.BlockSpec(memory_space=pl.ANY)
    return pl.pallas_call(
        functools.partial(_expert_kernel, layer=l),
        grid_spec=pltpu.PrefetchScalarGridSpec(
            num_scalar_prefetch=3,
            grid=(n_slots // bm,),
            in_specs=[rows, hbm, hbm, hbm],
            out_specs=rows,
            scratch_shapes=[pltpu.VMEM((D_MODEL, de), F32), pltpu.VMEM((D_MODEL, de), F32),
                            pltpu.VMEM((de, D_MODEL), F32),
                            pltpu.VMEM((D_MODEL, de), BF16), pltpu.VMEM((D_MODEL, de), BF16),
                            pltpu.VMEM((de, D_MODEL), BF16),
                            pltpu.SemaphoreType.DMA((3,))],
        ),
        out_shape=jax.ShapeDtypeStruct(slots.shape, slots.dtype),
        compiler_params=_params(("arbitrary",), VMEM_LIMIT),
        name="moe_experts",
    )(block_expert, nused, next_expert, slots, w1, w3, w2)


def _combine_kernel(dest_ref, x_ref, gw_ref, gf_ref, y_hbm, o_ref, ybuf, sem, *, n_tok):
    tm = x_ref.shape[0]
    i = pl.program_id(0)
    nsteps = pl.num_programs(0)

    def copy(step, r, k):
        slot = step % 2
        d = dest_ref[k * n_tok + step * tm + r]
        return pltpu.make_async_copy(y_hbm.at[pl.ds(pl.multiple_of(d * PACK_SUB, PACK_SUB), PACK_SUB)],
                                     ybuf.at[slot, k, pl.ds(pl.multiple_of(r * PACK_SUB, PACK_SUB), PACK_SUB)],
                                     sem.at[slot])

    def issue(step):
        def body(r, carry):
            for k in range(TOP_K):
                copy(step, r, k).start(priority=k)
            return carry
        lax.fori_loop(0, tm, body, 0, unroll=MOVE_UNROLL)

    @pl.when(i == 0)
    def _():
        issue(i)

    @pl.when(i + 1 < nsteps)
    def _():
        issue(i + 1)

    def drain(r, carry):
        for k in range(TOP_K):
            copy(i, r, k).wait()
        return carry
    lax.fori_loop(0, tm, drain, 0, unroll=MOVE_UNROLL)

    slot = i % 2
    gw = gw_ref[...]
    halves = [_unpack_rows(_load_slabs(ybuf.at[slot, k], tm)) for k in range(TOP_K)]
    gf = gf_ref[...]
    for part in range(2):
        cols = slice(part * PACK_W, (part + 1) * PACK_W)
        y = gw[:, 0:1] * halves[0][part] + gw[:, 1:2] * halves[1][part]
        o_ref[:, cols] = x_ref[:, cols] + gf[:, cols] * y


def _combine(x, y_slots, dest_flat, gw_t, mod3, mod_row):
    bsz, sx, _ = x.shape
    n_tok = bsz * sx
    tm = TM_MOVE
    nt = sx // tm
    row = (lambda i: i // nt) if mod_row is None else (lambda i: mod_row)
    out = pl.pallas_call(
        functools.partial(_combine_kernel, n_tok=n_tok),
        grid_spec=pltpu.PrefetchScalarGridSpec(
            num_scalar_prefetch=1,
            grid=(n_tok // tm,),
            in_specs=[pl.BlockSpec((tm, D_MODEL), lambda i, dest: (i, 0)),
                      pl.BlockSpec((tm, TOP_K), lambda i, dest: (i, 0)),
                      pl.BlockSpec((None, 1, D_MODEL), lambda i, dest: (row(i), 0, 5)),
                      pl.BlockSpec(memory_space=pl.ANY)],
            out_specs=pl.BlockSpec((tm, D_MODEL), lambda i, dest: (i, 0)),
            scratch_shapes=[pltpu.VMEM((2, TOP_K, tm * PACK_SUB, LANES), jnp.uint32),
                            pltpu.SemaphoreType.DMA((2,))],
        ),
        out_shape=jax.ShapeDtypeStruct((n_tok, D_MODEL), F32),
        compiler_params=_params(("arbitrary",)),
        name="moe_combine",
    )(dest_flat, x.reshape(n_tok, D_MODEL), gw_t, mod3, y_slots)
    return out.reshape(bsz, sx, D_MODEL)


def _layer_weights(l, w_in, na_q_g, na_k_g, mla_cq_g, w_uq, mla_ckv_g, w_ukv, mla_q_g, mla_k_g,
                   w_oa, w_ob, w_out, w_router):
    wi = w_in[l]
    c0 = C_NA + MLA_Q_LORA + MLA_KV_LORA
    kr_cols = jnp.zeros((D_MODEL, HEAD_PAD), F32).at[:, MLA_NOPE:MLA_QK_DIM].set(wi[:, c0:c0 + MLA_ROPE])
    w_in_arr = jnp.concatenate([wi[:, :c0], kr_cols, wi[:, c0 + MLA_ROPE:]], axis=1).astype(BF16)

    def pad_heads(w, width):
        w = w.reshape(w.shape[0], MLA_HEADS, width)
        return jnp.pad(w, ((0, 0), (0, 0), (0, HEAD_PAD - width))).reshape(w.shape[0], MLA_QK_PAD)

    ukv = w_ukv[l].reshape(MLA_KV_LORA, MLA_HEADS, MLA_NOPE + MLA_V_DIM)
    return {
        "w_in": w_in_arr,
        "na_q_g": jnp.tile(na_q_g[l], NA_HEADS)[None, :],
        "na_k_g": jnp.tile(na_k_g[l], NA_HEADS)[None, :],
        "cq_g": mla_cq_g[l][None, :],
        "ckv_g": mla_ckv_g[l][None, :],
        "w_uq": pad_heads(w_uq[l], MLA_QK_DIM).astype(BF16),
        "w_uk": pad_heads(ukv[:, :, :MLA_NOPE].reshape(MLA_KV_LORA, -1), MLA_NOPE).astype(BF16),
        "w_uv": ukv[:, :, MLA_NOPE:].reshape(MLA_KV_LORA, MLA_WIDTH).astype(BF16),
        "q_g": jnp.pad(mla_q_g[l], (0, HEAD_PAD - MLA_QK_DIM))[None, :],
        "k_g": jnp.pad(mla_k_g[l], (0, HEAD_PAD - MLA_QK_DIM))[None, :],
        "w_oa": w_oa[l].astype(BF16),
        "w_ob": w_ob[l].astype(BF16),
        "w_out": w_out[l].astype(BF16),
        "w_router_t": w_router.T.astype(BF16),
    }


def _rope_tables(s):
    half = MLA_ROPE // 4
    pos = jnp.arange(s, dtype=jnp.int32)
    inv = ROPE_BASE ** (-jnp.arange(half, dtype=F32) / half)
    ang_r = (pos // GRID_W).astype(F32)[:, None] * inv[None, :]
    ang_c = (pos % GRID_W).astype(F32)[:, None] * inv[None, :]
    zeros = jnp.zeros((s, half), F32)
    lead = jnp.zeros((s, MLA_NOPE), F32)
    tail = jnp.zeros((s, HEAD_PAD - MLA_QK_DIM), F32)
    cos = jnp.concatenate([lead + 1.0, jnp.cos(ang_r), jnp.cos(ang_r), jnp.cos(ang_c), jnp.cos(ang_c), tail + 1.0], 1)
    s1 = jnp.concatenate([lead, -jnp.sin(ang_r), zeros, -jnp.sin(ang_c), zeros, tail], 1)
    s2 = jnp.concatenate([lead, zeros, jnp.sin(ang_r), zeros, jnp.sin(ang_c), tail], 1)
    return cos, s1, s2


def _na_bias_tables(rpb):
    d = jnp.arange(NA_WIN_ROWS)[:, None]
    i = jnp.arange(NA_WIN_ROWS)[None, :]
    qc = jnp.arange(GRID_W)[:, None]
    kc = jnp.arange(GRID_W)[None, :]
    cs = jnp.clip(qc - NA_WIN_COLS // 2, 0, GRID_W - NA_WIN_COLS)
    cvalid = (kc >= cs) & (kc < cs + NA_WIN_COLS)
    ohr = jax.nn.one_hot(i - d + NA_WIN_ROWS - 1, 2 * NA_WIN_ROWS - 1, dtype=F32)
    ohc = jax.nn.one_hot(jnp.clip(kc - qc + NA_WIN_COLS - 1, 0, 2 * NA_WIN_COLS - 2), 2 * NA_WIN_COLS - 1, dtype=F32)
    t = jnp.einsum("dia,hab,qkb->dhqik", ohr, rpb.astype(F32), ohc, precision=lax.Precision.HIGHEST)
    t = jnp.where(cvalid[None, None, :, None, :], t * LOG2E, NEG_INF)
    return t.reshape(NA_WIN_ROWS, rpb.shape[0], GRID_W, NA_WIN_ROWS * GRID_W)


def _slot_tables(eid, rank, counts):
    bm = BM_EXP
    counts = counts[:, 0].astype(jnp.int32)
    padded = ((counts + bm - 1) // bm) * bm
    pad_end = jnp.cumsum(padded)
    pad_start = pad_end - padded
    experts = jnp.arange(N_EXPERTS, dtype=jnp.int32)
    dest = rank + jnp.sum(jnp.where(eid[..., None] == experts, pad_start, 0), axis=-1)
    m = eid.shape[1] * TOP_K
    n_blocks = -(-m // bm) + N_EXPERTS
    blk = jnp.arange(n_blocks, dtype=jnp.int32) * bm
    block_expert = jnp.minimum(jnp.sum(pad_end[None, :] <= blk[:, None], axis=1), N_EXPERTS - 1).astype(jnp.int32)
    nused = (pad_end[-1:] // bm).astype(jnp.int32)
    later = lax.cummin(jnp.where(counts > 0, experts, N_EXPERTS), axis=0, reverse=True)
    next_expert = jnp.concatenate([later[1:], jnp.full((1,), N_EXPERTS, jnp.int32)])
    next_expert = jnp.where(next_expert < N_EXPERTS, next_expert, -1).astype(jnp.int32)
    return dest.astype(jnp.int32), block_expert, nused, next_expert, n_blocks * bm


def kernel(x, c, ctx, c_ctx, w_ada, b_ada, norm_mix_g, norm_ffn_g, w_in, na_q_g, na_k_g, na_rpb,
           mla_cq_g, w_uq, mla_ckv_g, w_ukv, mla_q_g, mla_k_g, w_oa, w_ob, w_out,
           w_router, router_bias, w1, w3, w2):
    bsz, s, d = x.shape
    n_ctx = ctx.shape[1]
    ctx_row = bsz
    pad_rows = -(bsz + 1) % 8
    cvec = jnp.concatenate([c, c_ctx[None, :], jnp.zeros((pad_rows, d), F32)], axis=0)
    b_ada3 = b_ada[:, None, :]
    mix_g3 = norm_mix_g[:, None, :]
    ffn_g3 = norm_ffn_g[:, None, :]
    tabs_x = _rope_tables(s)
    tabs_c = (jnp.ones((n_ctx, LANES), F32), jnp.zeros((n_ctx, LANES), F32), jnp.zeros((n_ctx, LANES), F32))
    tri = (jnp.arange(TR_ROUTE)[:, None] < jnp.arange(TR_ROUTE)[None, :]).astype(BF16)
    bias_col = router_bias.astype(F32)[:, None]
    n_x = bsz * s

    xc = ctx
    for l in range(DEPTH):
        last = l == DEPTH - 1
        lw = _layer_weights(l, w_in, na_q_g, na_k_g, mla_cq_g, w_uq, mla_ckv_g, w_ukv, mla_q_g, mla_k_g,
                            w_oa, w_ob, w_out, w_router)
        mod3 = _ada(cvec, w_ada, b_ada3, l)[:, None, :]
        qa, ka, va_a, va_b, qb, kb, vb_a, vb_b, ga, gb = _proj(x, mod3, None, mix_g3, l, lw, tabs_x)
        (qa_c, ka_c, va_ca, va_cb, qb_c, kb_c, vb_ca, vb_cb,
         ga_c, gb_c) = _proj(xc, mod3, ctx_row, mix_g3, l, lw, tabs_c)
        oa = _na_attention(qa, ka, va_a, va_b, ka_c, va_ca, va_cb, _na_bias_tables(na_rpb[l]))
        ob = _mla_attention(qb, kb, vb_a, vb_b, kb_c, vb_ca, vb_cb)
        x, h_x, lg = _merge(x, oa, ob, ga, gb, mod3, None, ffn_g3, l, lw)
        if not last:
            oa_c, ob_c = _ctx_attention(qa_c, ka_c, va_ca, va_cb, qb_c, kb_c, vb_ca, vb_cb)
            xc, h_c, lg_c = _merge(xc, oa_c, ob_c, ga_c, gb_c, mod3, ctx_row, ffn_g3, l, lw)
            lg = jnp.concatenate([lg, lg_c], axis=1)

        eid, gw, rank, counts = _route(lg, bias_col, tri)
        dest, block_expert, nused, next_expert, n_slots = _slot_tables(eid, rank, counts)
        gw_t = gw.T
        slots = jnp.zeros((n_slots * PACK_SUB, LANES), jnp.uint32)
        slots = _dispatch(h_x, dest[:, :n_x].reshape(-1), slots)
        if not last:
            slots = _dispatch(h_c, dest[:, n_x:].reshape(-1), slots)
        y_slots = _experts(slots, block_expert, nused, next_expert, w1, w3, w2, l)
        x = _combine(x, y_slots, dest[:, :n_x].reshape(-1), gw_t[:n_x], mod3, None)
        if not last:
            xc = _combine(xc, y_slots, dest[:, n_x:].reshape(-1), gw_t[n_x:], mod3, ctx_row)
    return x
```

```python
import functools

import jax
import jax.numpy as jnp
from jax import lax
from jax.experimental import pallas as pl
from jax.experimental.pallas import tpu as pltpu

F32 = jnp.float32
BF16 = jnp.bfloat16

D_MODEL = 1024
DEPTH = 2
GRID_W = 64
N_MOD = 6

NA_HEADS = 8
NA_HEAD_DIM = 64
NA_WIN_ROWS = 8
NA_WIN_COLS = 16
NA_WIDTH = NA_HEADS * NA_HEAD_DIM

MLA_HEADS = 8
MLA_NOPE = 64
MLA_ROPE = 32
MLA_QK_DIM = MLA_NOPE + MLA_ROPE
MLA_V_DIM = 64
MLA_Q_LORA = 384
MLA_KV_LORA = 256
MLA_WIDTH = MLA_HEADS * MLA_V_DIM
ROPE_BASE = 10000.0

N_EXPERTS = 64
EXPERTS_PER_GROUP = 8
N_GROUPS = N_EXPERTS // EXPERTS_PER_GROUP
TOP_K = 2

RMS_EPS = 1e-6
NEG_INF = -1e30
LOG2E = 1.4426950408889634

LANES = 128
HEAD_PAD = LANES
MLA_QK_PAD = MLA_HEADS * HEAD_PAD

C_NA = 3 * NA_WIDTH
C_CQ = C_NA + MLA_Q_LORA
C_CKV = C_CQ + MLA_KV_LORA
C_KR = C_CKV + HEAD_PAD
C_ALL = C_KR + 2 * D_MODEL

TM_PROJ = 512
TM_MERGE = 1024
TQ_MLA = 1024
NA_QROWS = 32
TR_ROUTE = 1024
BM_EXP = 256
PACK_W = D_MODEL // 2
PACK_SUB = PACK_W // LANES
TM_MOVE = 512
MOVE_UNROLL = 8
VMEM_LIMIT = 56 * 1024 * 1024


def _sigmoid(v):
    return 1.0 / (1.0 + jnp.exp(-v))


def _rms(v, n):
    return v * lax.rsqrt(jnp.sum(v * v, axis=-1, keepdims=True) * (1.0 / n) + RMS_EPS)


def _pack_rows(v):
    lo = pltpu.bitcast(v[:, :PACK_W].astype(BF16).astype(F32), jnp.uint32)
    hi = pltpu.bitcast(v[:, PACK_W:].astype(BF16).astype(F32), jnp.uint32)
    return (lo >> 16) | (hi & jnp.uint32(0xFFFF0000))


def _unpack_rows(w):
    return pltpu.bitcast(w << 16, F32), pltpu.bitcast(w & jnp.uint32(0xFFFF0000), F32)


def _store_slabs(ref, w):
    for c in range(PACK_SUB):
        ref[pl.ds(c, w.shape[0], stride=PACK_SUB), :] = w[:, c * LANES:(c + 1) * LANES]


def _load_slabs(ref, m):
    return jnp.concatenate([ref[pl.ds(c, m, stride=PACK_SUB), :] for c in range(PACK_SUB)], axis=1)


def _split_bf16(a):
    hi = a.astype(BF16)
    lo = (a - hi.astype(F32)).astype(BF16)
    return hi, lo


def _dot(a, b):
    return jnp.dot(a, b, preferred_element_type=F32)


def _dot_nt(a, b):
    return lax.dot_general(a, b, (((1,), (1,)), ((), ())), preferred_element_type=F32)


def _params(sem, vmem=None):
    return pltpu.CompilerParams(dimension_semantics=sem, vmem_limit_bytes=vmem)


def _ada_kernel(c_ref, w_ref, b_ref, o_ref):
    cv = c_ref[...]
    s = cv * _sigmoid(cv)
    s_hi, s_lo = _split_bf16(s)
    w_hi, w_lo = _split_bf16(w_ref[...])
    o_ref[...] = _dot(s_hi, w_hi) + _dot(s_lo, w_hi) + _dot(s_hi, w_lo) + b_ref[...]


def _ada(cvec, w_ada, b_ada3, l):
    rows = cvec.shape[0]
    n = N_MOD * D_MODEL
    tn = 512
    return pl.pallas_call(
        _ada_kernel,
        grid=(n // tn,),
        in_specs=[
            pl.BlockSpec((rows, D_MODEL), lambda j: (0, 0)),
            pl.BlockSpec((None, D_MODEL, tn), lambda j: (l, 0, j)),
            pl.BlockSpec((None, 1, tn), lambda j: (l, 0, j)),
        ],
        out_specs=pl.BlockSpec((rows, tn), lambda j: (0, j)),
        out_shape=jax.ShapeDtypeStruct((rows, n), F32),
        compiler_params=_params(("arbitrary",)),
        name="ada_mod",
    )(cvec, w_ada, b_ada3)


def _rope(v, cos, s1, s2):
    return v * cos + pltpu.roll(v, LANES - 8, axis=1) * s1 + pltpu.roll(v, 8, axis=1) * s2


def _proj_kernel(x_ref, sh_ref, sc_ref, g_ref, win_ref, naqg_ref, nakg_ref, cqg_ref, ckvg_ref,
                 wuq_ref, wuk_ref, wuv_ref, qg_ref, kg_ref, cos_ref, s1_ref, s2_ref,
                 qa_ref, ka_ref, vaa_ref, vab_ref, qb_ref, kb_ref, vba_ref, vbb_ref, ga_ref, gb_ref):
    x = x_ref[...]
    h = _rms(x, D_MODEL) * g_ref[...]
    h = h * (1.0 + sc_ref[...]) + sh_ref[...]
    hb = h.astype(BF16)

    low = _pair_masks(x.shape[0])

    def headnorm(z, g):
        tiles = []
        for c in range(NA_WIDTH // LANES):
            zc = z[:, c * LANES:(c + 1) * LANES]
            sq = zc * zc
            lo = jnp.sum(jnp.where(low, sq, 0.0), axis=-1, keepdims=True)
            hi = jnp.sum(jnp.where(low, 0.0, sq), axis=-1, keepdims=True)
            tiles.append(zc * lax.rsqrt(jnp.where(low, lo, hi) * (1.0 / NA_HEAD_DIM) + RMS_EPS))
        return jnp.concatenate(tiles, axis=1) * g

    zq = _dot(hb, win_ref[:, 0:NA_WIDTH])
    qa_ref[...] = (headnorm(zq, naqg_ref[...]) * (NA_HEAD_DIM ** -0.5 * LOG2E)).astype(BF16)
    zk = _dot(hb, win_ref[:, NA_WIDTH:2 * NA_WIDTH])
    ka_ref[...] = headnorm(zk, nakg_ref[...]).astype(BF16)
    pair_low = (lax.broadcasted_iota(jnp.int32, (x.shape[0], NA_WIDTH), 1) & (LANES - 1)) < NA_HEAD_DIM
    vaa_ref[...], vab_ref[...] = _ones_pad(_dot(hb, win_ref[:, 2 * NA_WIDTH:C_NA]).astype(BF16), pair_low)

    cos = cos_ref[...]
    s1 = s1_ref[...]
    s2 = s2_ref[...]

    cq = _rms(_dot(hb, win_ref[:, C_NA:C_CQ]), MLA_Q_LORA) * cqg_ref[...]
    q = _dot(cq.astype(BF16), wuq_ref[...])
    qg = qg_ref[...]
    for hh in range(MLA_HEADS):
        sl = slice(hh * HEAD_PAD, (hh + 1) * HEAD_PAD)
        qn = _rms(q[:, sl], MLA_QK_DIM) * qg
        qb_ref[:, sl] = (_rope(qn, cos, s1, s2) * (MLA_QK_DIM ** -0.5 * LOG2E)).astype(BF16)

    ckv = (_rms(_dot(hb, win_ref[:, C_CQ:C_CKV]), MLA_KV_LORA) * ckvg_ref[...]).astype(BF16)
    kn = _dot(ckv, wuk_ref[...])
    kr = _dot(hb, win_ref[:, C_CKV:C_KR])
    kg = kg_ref[...]
    for hh in range(MLA_HEADS):
        sl = slice(hh * HEAD_PAD, (hh + 1) * HEAD_PAD)
        kh = _rms(kn[:, sl] + kr, MLA_QK_DIM) * kg
        kb_ref[:, sl] = _rope(kh, cos, s1, s2).astype(BF16)
    vba_ref[...], vbb_ref[...] = _ones_pad(_dot(ckv, wuv_ref[...]).astype(BF16), pair_low)

    ga_ref[...] = _sigmoid(_dot(hb, win_ref[:, C_KR:C_KR + D_MODEL])).astype(BF16)
    gb_ref[...] = _sigmoid(_dot(hb, win_ref[:, C_KR + D_MODEL:C_ALL])).astype(BF16)


def _proj(x, mod3, mod_row, gain3, l, lw, tabs):
    bsz, sx, _ = x.shape
    tm = min(TM_PROJ, sx)
    nt = sx // tm
    row = (lambda b: b) if mod_row is None else (lambda b: mod_row)

    def full(shape):
        return pl.BlockSpec(shape, lambda b, i: (0,) * len(shape))

    def tok(width):
        return pl.BlockSpec((None, tm, width), lambda b, i: (b, i, 0))

    def tab():
        return pl.BlockSpec((tm, LANES), lambda b, i: (i, 0))

    in_specs = [
        tok(D_MODEL),
        pl.BlockSpec((None, 1, D_MODEL), lambda b, i: (row(b), 0, 0)),
        pl.BlockSpec((None, 1, D_MODEL), lambda b, i: (row(b), 0, 1)),
        pl.BlockSpec((None, 1, D_MODEL), lambda b, i: (l, 0, 0)),
        full((D_MODEL, C_ALL)),
        full((1, NA_WIDTH)), full((1, NA_WIDTH)),
        full((1, MLA_Q_LORA)), full((1, MLA_KV_LORA)),
        full((MLA_Q_LORA, MLA_QK_PAD)), full((MLA_KV_LORA, MLA_QK_PAD)), full((MLA_KV_LORA, MLA_WIDTH)),
        full((1, HEAD_PAD)), full((1, HEAD_PAD)),
        tab(), tab(), tab(),
    ]
    widths = (NA_WIDTH, NA_WIDTH, NA_WIDTH, NA_WIDTH, MLA_QK_PAD, MLA_QK_PAD, MLA_WIDTH, MLA_WIDTH, D_MODEL, D_MODEL)
    return pl.pallas_call(
        _proj_kernel,
        grid=(bsz, nt),
        in_specs=in_specs,
        out_specs=[tok(w) for w in widths],
        out_shape=[jax.ShapeDtypeStruct((bsz, sx, w), BF16) for w in widths],
        compiler_params=_params(("arbitrary", "arbitrary"), VMEM_LIMIT),
        name="mixer_inputs",
    )(x, mod3, mod3, gain3, lw["w_in"], lw["na_q_g"], lw["na_k_g"], lw["cq_g"], lw["ckv_g"],
      lw["w_uq"], lw["w_uk"], lw["w_uv"], lw["q_g"], lw["k_g"], tabs[0], tabs[1], tabs[2])


def _pair_masks(rows):
    lane = lax.broadcasted_iota(jnp.int32, (rows, LANES), 1)
    return lane < NA_HEAD_DIM


def _ones_pad(v, low):
    one = jnp.ones_like(v)
    return jnp.where(low, v, one), jnp.where(low, one, v)


def _pair_finish(o_a, o_b, low):
    num = jnp.where(low, o_a, o_b)
    den = pltpu.roll(jnp.where(low, o_b, o_a), NA_HEAD_DIM, axis=1)
    return (num / den).astype(BF16)


def _na_kernel(q_ref, k_ref, va_ref, vb_ref, kc_ref, vca_ref, vcb_ref, bias_ref, o_ref):
    qb = pl.program_id(2)
    n_rows = k_ref.shape[0] // GRID_W
    nwin = NA_WIN_ROWS * GRID_W

    low = _pair_masks(GRID_W)
    kc = kc_ref[...]
    rows = range(NA_QROWS)
    starts, scores = [], []
    for j in rows:
        r = qb * NA_QROWS + j
        rs = jnp.clip(r - NA_WIN_ROWS // 2, 0, n_rows - NA_WIN_ROWS)
        start = pl.multiple_of(rs * GRID_W, GRID_W)
        q = q_ref[j * GRID_W:(j + 1) * GRID_W, :]
        zero = jnp.zeros_like(q)
        qs = jnp.concatenate([jnp.where(low, q, zero), jnp.where(low, zero, q)], axis=0)
        s = _dot_nt(qs, k_ref[pl.ds(start, nwin), :]) + bias_ref[r - rs].reshape(2 * GRID_W, nwin)
        starts.append(start)
        scores.append((s, _dot_nt(qs, kc)))
    probs = []
    for s, sc in scores:
        m = jnp.maximum(jnp.max(s, axis=-1, keepdims=True), jnp.max(sc, axis=-1, keepdims=True))
        probs.append((jnp.exp2(s - m).astype(BF16), jnp.exp2(sc - m).astype(BF16)))
    outs = []
    for start, (p, pc) in zip(starts, probs):
        o_a = _dot(p[:GRID_W], va_ref[pl.ds(start, nwin), :]) + _dot(pc[:GRID_W], vca_ref[...])
        o_b = _dot(p[GRID_W:], vb_ref[pl.ds(start, nwin), :]) + _dot(pc[GRID_W:], vcb_ref[...])
        outs.append(_pair_finish(o_a, o_b, low))
    o_ref[...] = jnp.concatenate(outs, axis=0)


def _na_attention(qa, ka, va_a, va_b, kc, vc_a, vc_b, bias):
    bsz, s, _ = qa.shape
    nc = kc.shape[1]
    nq = NA_QROWS * GRID_W
    npairs = NA_WIDTH // LANES
    nwin = NA_WIN_ROWS * GRID_W
    whole = pl.BlockSpec((None, s, LANES), lambda p, b, qb: (b, 0, p))
    cspec = pl.BlockSpec((None, nc, LANES), lambda p, b, qb: (b, 0, p))
    return pl.pallas_call(
        _na_kernel,
        grid=(npairs, bsz, s // nq),
        in_specs=[pl.BlockSpec((None, nq, LANES), lambda p, b, qb: (b, qb, p)),
                  whole, whole, whole, cspec, cspec, cspec,
                  pl.BlockSpec((NA_WIN_ROWS, 2, GRID_W, nwin), lambda p, b, qb: (0, p, 0, 0))],
        out_specs=pl.BlockSpec((None, nq, LANES), lambda p, b, qb: (b, qb, p)),
        out_shape=jax.ShapeDtypeStruct((bsz, s, NA_WIDTH), BF16),
        compiler_params=_params(("arbitrary", "arbitrary", "arbitrary"), VMEM_LIMIT),
        name="na_attention",
    )(qa, ka, va_a, va_b, kc, vc_a, vc_b, bias)


def _mla_kernel(q_ref, k_ref, va_ref, vb_ref, kc_ref, vca_ref, vcb_ref, o_ref):
    scores = []
    for hh in range(2):
        sl = slice(hh * HEAD_PAD, (hh + 1) * HEAD_PAD)
        q = q_ref[:, sl]
        scores.append((_dot_nt(q, k_ref[:, sl]), _dot_nt(q, kc_ref[:, sl])))
    probs = []
    for s, sc in scores:
        m = jnp.maximum(jnp.max(s, axis=-1, keepdims=True), jnp.max(sc, axis=-1, keepdims=True))
        probs.append((jnp.exp2(s - m).astype(BF16), jnp.exp2(sc - m).astype(BF16)))
    outs = [_dot(p, vx_ref[...]) + _dot(pc, vcx_ref[...])
            for (p, pc), (vx_ref, vcx_ref) in zip(probs, ((va_ref, vca_ref), (vb_ref, vcb_ref)))]
    o_ref[...] = _pair_finish(outs[0], outs[1], _pair_masks(q_ref.shape[0]))


def _mla_attention(qb, kb, vb_a, vb_b, kcb, vcb_a, vcb_b):
    bsz, s, _ = qb.shape
    nc = kcb.shape[1]
    tq = TQ_MLA
    npairs = MLA_WIDTH // LANES
    return pl.pallas_call(
        _mla_kernel,
        grid=(bsz, npairs, s // tq),
        in_specs=[pl.BlockSpec((None, tq, 2 * HEAD_PAD), lambda b, p, i: (b, i, p)),
                  pl.BlockSpec((None, s, 2 * HEAD_PAD), lambda b, p, i: (b, 0, p)),
                  pl.BlockSpec((None, s, LANES), lambda b, p, i: (b, 0, p)),
                  pl.BlockSpec((None, s, LANES), lambda b, p, i: (b, 0, p)),
                  pl.BlockSpec((None, nc, 2 * HEAD_PAD), lambda b, p, i: (b, 0, p)),
                  pl.BlockSpec((None, nc, LANES), lambda b, p, i: (b, 0, p)),
                  pl.BlockSpec((None, nc, LANES), lambda b, p, i: (b, 0, p))],
        out_specs=pl.BlockSpec((None, tq, LANES), lambda b, p, i: (b, i, p)),
        out_shape=jax.ShapeDtypeStruct((bsz, s, MLA_WIDTH), BF16),
        compiler_params=_params(("arbitrary", "arbitrary", "arbitrary"), VMEM_LIMIT),
        name="mla_attention",
    )(qb, kb, vb_a, vb_b, kcb, vcb_a, vcb_b)


def _softmax2_pv(s, v):
    p = jnp.exp2(s - jnp.max(s, axis=-1, keepdims=True))
    return _dot(p.astype(BF16), v) / jnp.sum(p, axis=-1, keepdims=True)


def _ctx_kernel(qa_ref, ka_ref, vaa_ref, vab_ref, qb_ref, kb_ref, vba_ref, vbb_ref, oa_ref, ob_ref):
    n = qa_ref.shape[0]
    low = _pair_masks(n)
    qa = qa_ref[...]
    zero = jnp.zeros_like(qa)
    ka = ka_ref[...]
    va = jnp.where(low, vaa_ref[...], vab_ref[...])
    o_lo = _softmax2_pv(_dot_nt(jnp.where(low, qa, zero), ka), va)
    o_hi = _softmax2_pv(_dot_nt(jnp.where(low, zero, qa), ka), va)
    oa_ref[...] = jnp.where(low, o_lo, o_hi).astype(BF16)
    vb = jnp.where(low, vba_ref[...], vbb_ref[...])
    outs = []
    for hh in range(2):
        sl = slice(hh * HEAD_PAD, (hh + 1) * HEAD_PAD)
        outs.append(_softmax2_pv(_dot_nt(qb_ref[:, sl], kb_ref[:, sl]), vb))
    ob_ref[...] = jnp.where(low, outs[0], outs[1]).astype(BF16)


def _ctx_attention(qa, ka, va_a, va_b, qb, kb, vb_a, vb_b):
    bsz, n, _ = qa.shape
    npairs = NA_WIDTH // LANES
    narrow = pl.BlockSpec((None, n, LANES), lambda b, p: (b, 0, p))
    wide = pl.BlockSpec((None, n, 2 * HEAD_PAD), lambda b, p: (b, 0, p))
    return pl.pallas_call(
        _ctx_kernel,
        grid=(bsz, npairs),
        in_specs=[narrow, narrow, narrow, narrow, wide, wide, narrow, narrow],
        out_specs=[narrow, narrow],
        out_shape=[jax.ShapeDtypeStruct((bsz, n, NA_WIDTH), BF16),
                   jax.ShapeDtypeStruct((bsz, n, MLA_WIDTH), BF16)],
        compiler_params=_params(("arbitrary", "arbitrary")),
        name="ctx_attention",
    )(qa, ka, va_a, va_b, qb, kb, vb_a, vb_b)


def _merge_kernel(x_ref, oa_ref, ob_ref, ga_ref, gb_ref, woa_ref, wob_ref, wout_ref, gm_ref,
                  g_ref, sh_ref, sc_ref, wr_ref, xo_ref, h_ref, lg_ref):
    y = (ga_ref[...].astype(F32) * _dot(oa_ref[...], woa_ref[...])
         + gb_ref[...].astype(F32) * _dot(ob_ref[...], wob_ref[...]))
    xn = x_ref[...] + gm_ref[...] * _dot(y.astype(BF16), wout_ref[...])
    xo_ref[...] = xn
    h = _rms(xn, D_MODEL) * g_ref[...]
    h = h * (1.0 + sc_ref[...]) + sh_ref[...]
    _store_slabs(h_ref, _pack_rows(h))
    lg_ref[...] = _dot_nt(wr_ref[...], h.astype(BF16))


def _merge(x, oa, ob, ga, gb, mod3, mod_row, gain3, l, lw):
    bsz, sx, _ = x.shape
    tm = min(TM_MERGE, sx)
    nt = sx // tm
    row = (lambda b: b) if mod_row is None else (lambda b: mod_row)

    def full(shape):
        return pl.BlockSpec(shape, lambda b, i: (0,) * len(shape))

    def tok(width):
        return pl.BlockSpec((None, tm, width), lambda b, i: (b, i, 0))

    def mod(chunk):
        return pl.BlockSpec((None, 1, D_MODEL), lambda b, i: (row(b), 0, chunk))

    return pl.pallas_call(
        _merge_kernel,
        grid=(bsz, nt),
        in_specs=[tok(D_MODEL), tok(NA_WIDTH), tok(MLA_WIDTH), tok(D_MODEL), tok(D_MODEL),
                  full((NA_WIDTH, D_MODEL)), full((MLA_WIDTH, D_MODEL)), full((D_MODEL, D_MODEL)),
                  mod(2),
                  pl.BlockSpec((None, 1, D_MODEL), lambda b, i: (l, 0, 0)),
                  mod(3), mod(4),
                  full((N_EXPERTS, D_MODEL))],
        out_specs=[tok(D_MODEL),
                   pl.BlockSpec((tm * PACK_SUB, LANES), lambda b, i: (b * nt + i, 0)),
                   pl.BlockSpec((N_EXPERTS, tm), lambda b, i: (0, b * nt + i))],
        out_shape=[jax.ShapeDtypeStruct((bsz, sx, D_MODEL), F32),
                   jax.ShapeDtypeStruct((bsz * sx * PACK_SUB, LANES), jnp.uint32),
                   jax.ShapeDtypeStruct((N_EXPERTS, bsz * sx), F32)],
        compiler_params=_params(("arbitrary", "arbitrary"), VMEM_LIMIT),
        name="merge_branches",
    )(x, oa, ob, ga, gb, lw["w_oa"], lw["w_ob"], lw["w_out"], mod3, gain3, mod3, mod3,
      lw["w_router_t"])


def _route_kernel(lg_ref, bias_ref, tri_ref, eid_ref, gw_ref, rank_ref, cnt_ref, run_ref):
    @pl.when(pl.program_id(0) == 0)
    def _():
        run_ref[...] = jnp.zeros_like(run_ref)

    tr = lg_ref.shape[1]
    s = _sigmoid(lg_ref[...])
    biased = s + bias_ref[...]
    sub = lax.broadcasted_iota(jnp.int32, (EXPERTS_PER_GROUP, tr), 0).astype(F32)
    none = float(EXPERTS_PER_GROUP)
    best = jnp.full((1, tr), -jnp.inf, F32)
    e1 = jnp.zeros((1, tr), F32)
    e2 = jnp.zeros((1, tr), F32)
    for g in range(N_GROUPS):
        bg = biased[g * EXPERTS_PER_GROUP:(g + 1) * EXPERTS_PER_GROUP]
        m1 = jnp.max(bg, axis=0, keepdims=True)
        i1 = jnp.min(jnp.where(bg == m1, sub, none), axis=0, keepdims=True)
        rest = jnp.where(sub == i1, -jnp.inf, bg)
        m2 = jnp.max(rest, axis=0, keepdims=True)
        i2 = jnp.min(jnp.where(rest == m2, sub, none), axis=0, keepdims=True)
        score = m1 + m2
        better = score > best
        best = jnp.where(better, score, best)
        e1 = jnp.where(better, g * EXPERTS_PER_GROUP + i1, e1)
        e2 = jnp.where(better, g * EXPERTS_PER_GROUP + i2, e2)

    rowid = lax.broadcasted_iota(jnp.int32, (N_EXPERTS, tr), 0).astype(F32)
    is1 = rowid == e1
    is2 = rowid == e2
    w1 = jnp.sum(jnp.where(is1, s, 0.0), axis=0, keepdims=True)
    w2 = jnp.sum(jnp.where(is2, s, 0.0), axis=0, keepdims=True)
    tot = w1 + w2
    gw_ref[0:1, :] = w1 / tot
    gw_ref[1:2, :] = w2 / tot
    eid_ref[0:1, :] = e1.astype(jnp.int32)
    eid_ref[1:2, :] = e2.astype(jnp.int32)

    onehot = jnp.where(is1 | is2, 1.0, 0.0)
    before = _dot(onehot.astype(BF16), tri_ref[...]) + run_ref[...]
    rank_ref[0:1, :] = jnp.sum(jnp.where(is1, before, 0.0), axis=0, keepdims=True).astype(jnp.int32)
    rank_ref[1:2, :] = jnp.sum(jnp.where(is2, before, 0.0), axis=0, keepdims=True).astype(jnp.int32)
    run_ref[...] = run_ref[...] + jnp.sum(onehot, axis=1, keepdims=True)
    cnt_ref[...] = run_ref[...]


def _route(logits, bias_col, tri):
    t = logits.shape[1]
    tr = TR_ROUTE
    pair = pl.BlockSpec((TOP_K, tr), lambda i: (0, i))
    return pl.pallas_call(
        _route_kernel,
        grid=(t // tr,),
        in_specs=[pl.BlockSpec((N_EXPERTS, tr), lambda i: (0, i)),
                  pl.BlockSpec((N_EXPERTS, 1), lambda i: (0, 0)),
                  pl.BlockSpec((tr, tr), lambda i: (0, 0))],
        out_specs=[pair, pair, pair, pl.BlockSpec((N_EXPERTS, 1), lambda i: (0, 0))],
        out_shape=[jax.ShapeDtypeStruct((TOP_K, t), jnp.int32),
                   jax.ShapeDtypeStruct((TOP_K, t), F32),
                   jax.ShapeDtypeStruct((TOP_K, t), jnp.int32),
                   jax.ShapeDtypeStruct((N_EXPERTS, 1), F32)],
        scratch_shapes=[pltpu.VMEM((N_EXPERTS, 1), F32)],
        compiler_params=_params(("arbitrary",)),
        name="route",
    )(logits, bias_col, tri)


def _dispatch_kernel(dest_ref, h_ref, slots_in_ref, slots_ref, sem, *, n_tok):
    del slots_in_ref
    tm = h_ref.shape[0] // PACK_SUB
    base = pl.program_id(0) * tm

    def copy(r, k):
        d = dest_ref[k * n_tok + base + r]
        return pltpu.make_async_copy(h_ref.at[pl.ds(pl.multiple_of(r * PACK_SUB, PACK_SUB), PACK_SUB)],
                                     slots_ref.at[pl.ds(pl.multiple_of(d * PACK_SUB, PACK_SUB), PACK_SUB)], sem)

    def issue(r, carry):
        for k in range(TOP_K):
            copy(r, k).start(priority=k)
        return carry

    def drain(r, carry):
        for k in range(TOP_K):
            copy(r, k).wait()
        return carry

    lax.fori_loop(0, tm, issue, 0, unroll=MOVE_UNROLL)
    lax.fori_loop(0, tm, drain, 0, unroll=MOVE_UNROLL)


def _dispatch(h, dest_flat, slots):
    n_tok = h.shape[0] // PACK_SUB
    tm = TM_MOVE
    return pl.pallas_call(
        functools.partial(_dispatch_kernel, n_tok=n_tok),
        grid_spec=pltpu.PrefetchScalarGridSpec(
            num_scalar_prefetch=1,
            grid=(n_tok // tm,),
            in_specs=[pl.BlockSpec((tm * PACK_SUB, LANES), lambda i, dest: (i, 0)),
                      pl.BlockSpec(memory_space=pl.ANY)],
            out_specs=pl.BlockSpec(memory_space=pl.ANY),
            scratch_shapes=[pltpu.SemaphoreType.DMA],
        ),
        out_shape=jax.ShapeDtypeStruct(slots.shape, slots.dtype),
        input_output_aliases={2: 0},
        compiler_params=_params(("arbitrary",)),
        name="moe_dispatch",
    )(dest_flat, h, slots)


def _expert_kernel(be_ref, nused_ref, nxt_ref, nv_ref, x_ref, w1_hbm, w3_hbm, w2_hbm, y_ref,
                   st1, st3, st2, w1b, w3b, w2b, sem, *, layer):
    i = pl.program_id(0)
    used = i < nused_ref[0]
    stages = ((w1_hbm, st1, w1b), (w3_hbm, st3, w3b), (w2_hbm, st2, w2b))

    def fetch(e):
        return [pltpu.make_async_copy(w_hbm.at[layer, e], st, sem.at[j]) for j, (w_hbm, st, _) in enumerate(stages)]

    @pl.when(used)
    def _():
        e = be_ref[i]
        prev = be_ref[jnp.maximum(i - 1, 0)]

        @pl.when(i == 0)
        def _():
            for c in fetch(e):
                c.start()

        @pl.when((i == 0) | (e != prev))
        def _():
            for c, (_, st, wb) in zip(fetch(e), stages):
                c.wait()
                wb[...] = st[...].astype(BF16)
            nxt = nxt_ref[e]

            @pl.when(nxt >= 0)
            def _():
                for c in fetch(nxt):
                    c.start()

        bm = x_ref.shape[0] // PACK_SUB
        half = bm // 2

        def ffn(m):
            x_lo, x_hi = _unpack_rows(_load_slabs(x_ref, m))
            xb = jnp.concatenate([x_lo.astype(BF16), x_hi.astype(BF16)], axis=1)
            a = _dot(xb, w1b[...])
            b = _dot(xb, w3b[...])
            _store_slabs(y_ref, _pack_rows(_dot((a * _sigmoid(a) * b).astype(BF16), w2b[...])))

        real_rows = nv_ref[i]

        @pl.when(real_rows > half)
        def _():
            ffn(bm)

        @pl.when(real_rows <= half)
        def _():
            ffn(half)
            y_ref[pl.ds(half * PACK_SUB, half * PACK_SUB), :] = jnp.zeros((half * PACK_SUB, LANES), y_ref.dtype)

    @pl.when(jnp.logical_not(used))
    def _():
        y_ref[...] = jnp.zeros_like(y_ref)


def _experts(slots, block_expert, nused, next_expert, block_rows, w1, w3, w2, l):
    n_slots = slots.shape[0] // PACK_SUB
    bm = BM_EXP
    de = w1.shape[-1]
    rows = pl.BlockSpec((bm * PACK_SUB, LANES), lambda i, be, nu, nx, nv: (i, 0))
    hbm = pl.BlockSpec(memory_space=pl.ANY)
    return pl.pallas_call(
        functools.partial(_expert_kernel, layer=l),
        grid_spec=pltpu.PrefetchScalarGridSpec(
            num_scalar_prefetch=4,
            grid=(n_slots // bm,),
            in_specs=[rows, hbm, hbm, hbm],
            out_specs=rows,
            scratch_shapes=[pltpu.VMEM((D_MODEL, de), F32), pltpu.VMEM((D_MODEL, de), F32),
                            pltpu.VMEM((de, D_MODEL), F32),
                            pltpu.VMEM((D_MODEL, de), BF16), pltpu.VMEM((D_MODEL, de), BF16),
                            pltpu.VMEM((de, D_MODEL), BF16),
                            pltpu.SemaphoreType.DMA((3,))],
        ),
        out_shape=jax.ShapeDtypeStruct(slots.shape, slots.dtype),
        compiler_params=_params(("arbitrary",), VMEM_LIMIT),
        name="moe_experts",
    )(block_expert, nused, next_expert, block_rows, slots, w1, w3, w2)


def _combine_kernel(dest_ref, x_ref, gw_ref, gf_ref, y_hbm, o_ref, ybuf, sem, *, n_tok):
    tm = x_ref.shape[0]
    i = pl.program_id(0)
    nsteps = pl.num_programs(0)

    def copy(step, r, k):
        slot = step % 2
        d = dest_ref[k * n_tok + step * tm + r]
        return pltpu.make_async_copy(y_hbm.at[pl.ds(pl.multiple_of(d * PACK_SUB, PACK_SUB), PACK_SUB)],
                                     ybuf.at[slot, k, pl.ds(pl.multiple_of(r * PACK_SUB, PACK_SUB), PACK_SUB)],
                                     sem.at[slot])

    def issue(step):
        def body(r, carry):
            for k in range(TOP_K):
                copy(step, r, k).start(priority=k)
            return carry
        lax.fori_loop(0, tm, body, 0, unroll=MOVE_UNROLL)

    @pl.when(i == 0)
    def _():
        issue(i)

    @pl.when(i + 1 < nsteps)
    def _():
        issue(i + 1)

    def drain(r, carry):
        for k in range(TOP_K):
            copy(i, r, k).wait()
        return carry
    lax.fori_loop(0, tm, drain, 0, unroll=MOVE_UNROLL)

    slot = i % 2
    gw = gw_ref[...]
    halves = [_unpack_rows(_load_slabs(ybuf.at[slot, k], tm)) for k in range(TOP_K)]
    gf = gf_ref[...]
    for part in range(2):
        cols = slice(part * PACK_W, (part + 1) * PACK_W)
        y = gw[:, 0:1] * halves[0][part] + gw[:, 1:2] * halves[1][part]
        o_ref[:, cols] = x_ref[:, cols] + gf[:, cols] * y


def _combine(x, y_slots, dest_flat, gw_t, mod3, mod_row):
    bsz, sx, _ = x.shape
    n_tok = bsz * sx
    tm = TM_MOVE
    nt = sx // tm
    row = (lambda i: i // nt) if mod_row is None else (lambda i: mod_row)
    out = pl.pallas_call(
        functools.partial(_combine_kernel, n_tok=n_tok),
        grid_spec=pltpu.PrefetchScalarGridSpec(
            num_scalar_prefetch=1,
            grid=(n_tok // tm,),
            in_specs=[pl.BlockSpec((tm, D_MODEL), lambda i, dest: (i, 0)),
                      pl.BlockSpec((tm, TOP_K), lambda i, dest: (i, 0)),
                      pl.BlockSpec((None, 1, D_MODEL), lambda i, dest: (row(i), 0, 5)),
                      pl.BlockSpec(memory_space=pl.ANY)],
            out_specs=pl.BlockSpec((tm, D_MODEL), lambda i, dest: (i, 0)),
            scratch_shapes=[pltpu.VMEM((2, TOP_K, tm * PACK_SUB, LANES), jnp.uint32),
                            pltpu.SemaphoreType.DMA((2,))],
        ),
        out_shape=jax.ShapeDtypeStruct((n_tok, D_MODEL), F32),
        compiler_params=_params(("arbitrary",)),
        name="moe_combine",
    )(dest_flat, x.reshape(n_tok, D_MODEL), gw_t, mod3, y_slots)
    return out.reshape(bsz, sx, D_MODEL)


def _layer_weights(l, w_in, na_q_g, na_k_g, mla_cq_g, w_uq, mla_ckv_g, w_ukv, mla_q_g, mla_k_g,
                   w_oa, w_ob, w_out, w_router):
    wi = w_in[l]
    c0 = C_NA + MLA_Q_LORA + MLA_KV_LORA
    kr_cols = jnp.zeros((D_MODEL, HEAD_PAD), F32).at[:, MLA_NOPE:MLA_QK_DIM].set(wi[:, c0:c0 + MLA_ROPE])
    w_in_arr = jnp.concatenate([wi[:, :c0], kr_cols, wi[:, c0 + MLA_ROPE:]], axis=1).astype(BF16)

    def pad_heads(w, width):
        w = w.reshape(w.shape[0], MLA_HEADS, width)
        return jnp.pad(w, ((0, 0), (0, 0), (0, HEAD_PAD - width))).reshape(w.shape[0], MLA_QK_PAD)

    ukv = w_ukv[l].reshape(MLA_KV_LORA, MLA_HEADS, MLA_NOPE + MLA_V_DIM)
    return {
        "w_in": w_in_arr,
        "na_q_g": jnp.tile(na_q_g[l], NA_HEADS)[None, :],
        "na_k_g": jnp.tile(na_k_g[l], NA_HEADS)[None, :],
        "cq_g": mla_cq_g[l][None, :],
        "ckv_g": mla_ckv_g[l][None, :],
        "w_uq": pad_heads(w_uq[l], MLA_QK_DIM).astype(BF16),
        "w_uk": pad_heads(ukv[:, :, :MLA_NOPE].reshape(MLA_KV_LORA, -1), MLA_NOPE).astype(BF16),
        "w_uv": ukv[:, :, MLA_NOPE:].reshape(MLA_KV_LORA, MLA_WIDTH).astype(BF16),
        "q_g": jnp.pad(mla_q_g[l], (0, HEAD_PAD - MLA_QK_DIM))[None, :],
        "k_g": jnp.pad(mla_k_g[l], (0, HEAD_PAD - MLA_QK_DIM))[None, :],
        "w_oa": w_oa[l].astype(BF16),
        "w_ob": w_ob[l].astype(BF16),
        "w_out": w_out[l].astype(BF16),
        "w_router_t": w_router.T.astype(BF16),
    }


def _rope_tables(s):
    half = MLA_ROPE // 4
    pos = jnp.arange(s, dtype=jnp.int32)
    inv = ROPE_BASE ** (-jnp.arange(half, dtype=F32) / half)
    ang_r = (pos // GRID_W).astype(F32)[:, None] * inv[None, :]
    ang_c = (pos % GRID_W).astype(F32)[:, None] * inv[None, :]
    zeros = jnp.zeros((s, half), F32)
    lead = jnp.zeros((s, MLA_NOPE), F32)
    tail = jnp.zeros((s, HEAD_PAD - MLA_QK_DIM), F32)
    cos = jnp.concatenate([lead + 1.0, jnp.cos(ang_r), jnp.cos(ang_r), jnp.cos(ang_c), jnp.cos(ang_c), tail + 1.0], 1)
    s1 = jnp.concatenate([lead, -jnp.sin(ang_r), zeros, -jnp.sin(ang_c), zeros, tail], 1)
    s2 = jnp.concatenate([lead, zeros, jnp.sin(ang_r), zeros, jnp.sin(ang_c), tail], 1)
    return cos, s1, s2


def _na_bias_tables(rpb):
    d = jnp.arange(NA_WIN_ROWS)[:, None]
    i = jnp.arange(NA_WIN_ROWS)[None, :]
    qc = jnp.arange(GRID_W)[:, None]
    kc = jnp.arange(GRID_W)[None, :]
    cs = jnp.clip(qc - NA_WIN_COLS // 2, 0, GRID_W - NA_WIN_COLS)
    cvalid = (kc >= cs) & (kc < cs + NA_WIN_COLS)
    ohr = jax.nn.one_hot(i - d + NA_WIN_ROWS - 1, 2 * NA_WIN_ROWS - 1, dtype=F32)
    ohc = jax.nn.one_hot(jnp.clip(kc - qc + NA_WIN_COLS - 1, 0, 2 * NA_WIN_COLS - 2), 2 * NA_WIN_COLS - 1, dtype=F32)
    t = jnp.einsum("dia,hab,qkb->dhqik", ohr, rpb.astype(F32), ohc, precision=lax.Precision.HIGHEST)
    t = jnp.where(cvalid[None, None, :, None, :], t * LOG2E, NEG_INF)
    return t.reshape(NA_WIN_ROWS, rpb.shape[0], GRID_W, NA_WIN_ROWS * GRID_W)


def _slot_tables(eid, rank, counts):
    bm = BM_EXP
    counts = counts[:, 0].astype(jnp.int32)
    padded = ((counts + bm - 1) // bm) * bm
    pad_end = jnp.cumsum(padded)
    pad_start = pad_end - padded
    experts = jnp.arange(N_EXPERTS, dtype=jnp.int32)
    dest = rank + jnp.sum(jnp.where(eid[..., None] == experts, pad_start, 0), axis=-1)
    m = eid.shape[1] * TOP_K
    n_blocks = -(-m // bm) + N_EXPERTS
    blk = jnp.arange(n_blocks, dtype=jnp.int32) * bm
    block_expert = jnp.minimum(jnp.sum(pad_end[None, :] <= blk[:, None], axis=1), N_EXPERTS - 1).astype(jnp.int32)
    nused = (pad_end[-1:] // bm).astype(jnp.int32)
    later = lax.cummin(jnp.where(counts > 0, experts, N_EXPERTS), axis=0, reverse=True)
    next_expert = jnp.concatenate([later[1:], jnp.full((1,), N_EXPERTS, jnp.int32)])
    next_expert = jnp.where(next_expert < N_EXPERTS, next_expert, -1).astype(jnp.int32)
    real_end = jnp.sum(jnp.where(block_expert[:, None] == experts[None, :], pad_start + counts, 0), axis=1)
    block_rows = jnp.clip(real_end - blk, 0, bm).astype(jnp.int32)
    return dest.astype(jnp.int32), block_expert, nused, next_expert, block_rows, n_blocks * bm


def kernel(x, c, ctx, c_ctx, w_ada, b_ada, norm_mix_g, norm_ffn_g, w_in, na_q_g, na_k_g, na_rpb,
           mla_cq_g, w_uq, mla_ckv_g, w_ukv, mla_q_g, mla_k_g, w_oa, w_ob, w_out,
           w_router, router_bias, w1, w3, w2):
    bsz, s, d = x.shape
    n_ctx = ctx.shape[1]
    ctx_row = bsz
    pad_rows = -(bsz + 1) % 8
    cvec = jnp.concatenate([c, c_ctx[None, :], jnp.zeros((pad_rows, d), F32)], axis=0)
    b_ada3 = b_ada[:, None, :]
    mix_g3 = norm_mix_g[:, None, :]
    ffn_g3 = norm_ffn_g[:, None, :]
    tabs_x = _rope_tables(s)
    tabs_c = (jnp.ones((n_ctx, LANES), F32), jnp.zeros((n_ctx, LANES), F32), jnp.zeros((n_ctx, LANES), F32))
    tri = (jnp.arange(TR_ROUTE)[:, None] < jnp.arange(TR_ROUTE)[None, :]).astype(BF16)
    bias_col = router_bias.astype(F32)[:, None]
    n_x = bsz * s

    xc = ctx
    for l in range(DEPTH):
        last = l == DEPTH - 1
        lw = _layer_weights(l, w_in, na_q_g, na_k_g, mla_cq_g, w_uq, mla_ckv_g, w_ukv, mla_q_g, mla_k_g,
                            w_oa, w_ob, w_out, w_router)
        mod3 = _ada(cvec, w_ada, b_ada3, l)[:, None, :]
        qa, ka, va_a, va_b, qb, kb, vb_a, vb_b, ga, gb = _proj(x, mod3, None, mix_g3, l, lw, tabs_x)
        (qa_c, ka_c, va_ca, va_cb, qb_c, kb_c, vb_ca, vb_cb,
         ga_c, gb_c) = _proj(xc, mod3, ctx_row, mix_g3, l, lw, tabs_c)
        oa = _na_attention(qa, ka, va_a, va_b, ka_c, va_ca, va_cb, _na_bias_tables(na_rpb[l]))
        ob = _mla_attention(qb, kb, vb_a, vb_b, kb_c, vb_ca, vb_cb)
        x, h_x, lg = _merge(x, oa, ob, ga, gb, mod3, None, ffn_g3, l, lw)
        if not last:
            oa_c, ob_c = _ctx_attention(qa_c, ka_c, va_ca, va_cb, qb_c, kb_c, vb_ca, vb_cb)
            xc, h_c, lg_c = _merge(xc, oa_c, ob_c, ga_c, gb_c, mod3, ctx_row, ffn_g3, l, lw)
            lg = jnp.concatenate([lg, lg_c], axis=1)

        eid, gw, rank, counts = _route(lg, bias_col, tri)
        dest, block_expert, nused, next_expert, block_rows, n_slots = _slot_tables(eid, rank, counts)
        gw_t = gw.T
        slots = jnp.zeros((n_slots * PACK_SUB, LANES), jnp.uint32)
        slots = _dispatch(h_x, dest[:, :n_x].reshape(-1), slots)
        if not last:
            slots = _dispatch(h_c, dest[:, n_x:].reshape(-1), slots)
        y_slots = _experts(slots, block_expert, nused, next_expert, block_rows, w1, w3, w2, l)
        x = _combine(x, y_slots, dest[:, :n_x].reshape(-1), gw_t[:n_x], mod3, None)
        if not last:
            xc = _combine(xc, y_slots, dest[:, n_x:].reshape(-1), gw_t[n_x:], mod3, ctx_row)
    return x
```
